```python
import jax, jax.numpy as jnp
from jax import lax
import numpy as np

D_MODEL = 1024
BATCH = 8
SEQ = 2048
DEPTH = 4
DEC_BATCH = 32
DEC_SEQ = 64
PAST_LEN = 4096

CHUNK = 64
N_MIXERS = 2
D_RNN = D_MODEL
CONV_WIDTH = 4
LRU_BLOCKS = 16
LRU_BLOCK_W = D_RNN // LRU_BLOCKS
LRU_C = 8.0
N_HEADS = 16
N_KV_HEADS = 2
HEAD_DIM = 64
KV_GROUP = N_HEADS // N_KV_HEADS
WINDOW = 128
WINDOW_CHUNKS = WINDOW // CHUNK
ROPE_THETA = 10000.0
QKV_DIM = (N_HEADS + 2 * N_KV_HEADS) * HEAD_DIM
N_EXPERTS = 16
N_GROUPS = 4
EXPERTS_PER_GROUP = N_EXPERTS // N_GROUPS
TOP_K = 2
D_EXPERT = 512
DEEPNORM_ALPHA = (2.0 * DEPTH) ** 0.25
DEEPNORM_BETA = (8.0 * DEPTH) ** -0.25
LN_EPS = 1e-5
N_LRU_LAYERS = (DEPTH + 1) // 2
N_ATTN_LAYERS = DEPTH // 2

kernel_name = "hybrid_rglru_swa_sink_grouped_moe_deepnorm_step"


def layer_norm(x, g, b):
    xf = x.astype(jnp.float32)
    mu = jnp.mean(xf, -1, keepdims=True)
    var = jnp.mean(jnp.square(xf - mu), -1, keepdims=True)
    return ((xf - mu) * lax.rsqrt(var + LN_EPS) * g + b).astype(x.dtype)


def _lin_combine(c1, c2):
    a1, b1 = c1
    a2, b2 = c2
    return a1 * a2, a2 * b1 + b2


def rglru_mixer(x, conv_state, h0, w_in, conv_w, conv_b, w_gates, b_gates, lam, w_out):
    B, T, _ = x.shape
    gate_in, rec_in = jnp.split(x @ w_in, 2, axis=-1)
    xpad = jnp.concatenate([conv_state.astype(rec_in.dtype), rec_in], axis=1)
    xc = conv_b
    for k in range(CONV_WIDTH):
        xc = xc + xpad[:, k:k + T] * conv_w[k]
    new_conv_state = xpad[:, T:]
    xb = xc.reshape(B, T, LRU_BLOCKS, LRU_BLOCK_W)
    gates = jnp.einsum("btnk,nkj->btnj", xb, w_gates) + b_gates
    r, i = jnp.split(gates.astype(jnp.float32), 2, axis=-1)
    r = jax.nn.sigmoid(r).reshape(B, T, D_RNN)
    i = jax.nn.sigmoid(i).reshape(B, T, D_RNN)
    log_a = -LRU_C * r * jax.nn.softplus(-lam.astype(jnp.float32))
    a = jnp.exp(log_a)
    b = jnp.sqrt(-jnp.expm1(2.0 * log_a)) * (i * xc.astype(jnp.float32))
    b = b.at[:, 0].add(a[:, 0] * h0.astype(jnp.float32))
    _, h = lax.associative_scan(_lin_combine, (a, b), axis=1)
    y = h.astype(x.dtype) * jax.nn.gelu(gate_in, approximate=True)
    return y @ w_out, new_conv_state, h[:, -1].astype(h0.dtype)


def rope(x, pos):
    half = HEAD_DIM // 2
    inv_freq = ROPE_THETA ** (-jnp.arange(half, dtype=jnp.float32) / half)
    ang = pos.astype(jnp.float32)[:, None] * inv_freq[None, :]
    cos = jnp.cos(ang)[:, None, :]
    sin = jnp.sin(ang)[:, None, :]
    xf = x.astype(jnp.float32)
    x1, x2 = xf[..., :half], xf[..., half:]
    return jnp.concatenate([x1 * cos - x2 * sin, x2 * cos + x1 * sin], -1).astype(x.dtype)


def qkv_proj(x, pos, w_qkv, b_qkv):
    B, T, _ = x.shape
    qkv = x @ w_qkv + b_qkv
    q, k, v = jnp.split(qkv, [N_HEADS * HEAD_DIM, (N_HEADS + N_KV_HEADS) * HEAD_DIM], axis=-1)
    q = rope(q.reshape(B, T, N_HEADS, HEAD_DIM), pos)
    k = rope(k.reshape(B, T, N_KV_HEADS, HEAD_DIM), pos)
    return q, k, v.reshape(B, T, N_KV_HEADS, HEAD_DIM)


def sink_attention(q, k, v, valid, sinks):
    B, N, Tq = q.shape[:3]
    qg = q.reshape(B, N, Tq, N_KV_HEADS, KV_GROUP, HEAD_DIM)
    s = jnp.einsum("bnqkgd,bnskd->bnkgqs", qg, k).astype(jnp.float32) * (HEAD_DIM ** -0.5)
    s = jnp.where(valid[None, :, None, None, None, :], s, -jnp.inf)
    sink = sinks.astype(jnp.float32).reshape(1, 1, N_KV_HEADS, KV_GROUP, 1, 1)
    m = jnp.maximum(jnp.max(s, -1, keepdims=True), sink)
    p = jnp.exp(s - m)
    p = (p / (jnp.sum(p, -1, keepdims=True) + jnp.exp(sink - m))).astype(v.dtype)
    o = jnp.einsum("bnkgqs,bnskd->bnqkgd", p, v)
    return o.reshape(B, N, Tq, N_HEADS * HEAD_DIM)


def swa_prompt(x, w_qkv, b_qkv, sinks, w_o):
    B, S, _ = x.shape
    n_chunks = S // CHUNK
    q, k, v = qkv_proj(x, jnp.arange(S), w_qkv, b_qkv)

    def band(t):
        tp = jnp.pad(t, ((0, 0), (WINDOW_CHUNKS * CHUNK, 0), (0, 0), (0, 0)))
        tp = tp.reshape(B, n_chunks + WINDOW_CHUNKS, CHUNK, N_KV_HEADS, HEAD_DIM)
        return jnp.concatenate([tp[:, j:j + n_chunks] for j in range(WINDOW_CHUNKS + 1)], axis=2)

    chunk_id = jnp.arange(n_chunks)[:, None]
    key_sub = jnp.arange((WINDOW_CHUNKS + 1) * CHUNK)[None, :] // CHUNK
    valid = chunk_id - WINDOW_CHUNKS + key_sub >= 0
    o = sink_attention(q.reshape(B, n_chunks, CHUNK, N_HEADS, HEAD_DIM), band(k), band(v), valid, sinks)
    return o.reshape(B, S, N_HEADS * HEAD_DIM) @ w_o, k[:, S - WINDOW:], v[:, S - WINDOW:]


def swa_sample(x, cache_k, cache_v, w_qkv, b_qkv, sinks, w_o):
    B, T, _ = x.shape
    q, k, v = qkv_proj(x, PAST_LEN + jnp.arange(T), w_qkv, b_qkv)
    keys = jnp.concatenate([cache_k.astype(k.dtype), k], axis=1)
    vals = jnp.concatenate([cache_v.astype(v.dtype), v], axis=1)
    valid = jnp.ones((1, keys.shape[1]), dtype=bool)
    o = sink_attention(q[:, None], keys[:, None], vals[:, None], valid, sinks)
    keep = cache_k.shape[1]
    return o.reshape(B, T, N_HEADS * HEAD_DIM) @ w_o, keys[:, -keep:], vals[:, -keep:]


def shared_router(x2d, w_router, b_router):
    probs = jax.nn.softmax((x2d @ w_router).astype(jnp.float32), axis=-1)
    sel = probs + b_router.astype(jnp.float32)
    group_score = jnp.sum(lax.top_k(sel.reshape(-1, N_GROUPS, EXPERTS_PER_GROUP), TOP_K)[0], -1)
    best_group = jnp.argmax(group_score, -1)
    in_group = (jnp.arange(N_EXPERTS) // EXPERTS_PER_GROUP)[None, :] == best_group[:, None]
    _, idx = lax.top_k(jnp.where(in_group, sel, -jnp.inf), TOP_K)
    w = jnp.take_along_axis(probs, idx, -1)
    w = w / jnp.sum(w, -1, keepdims=True)
    return jnp.sum(jax.nn.one_hot(idx, N_EXPERTS, dtype=jnp.float32) * w[..., None], axis=1)


def moe_ffn(x, w_router, b_router, w_gate, w_up, w_down):
    B, T, D = x.shape
    xf = x.reshape(B * T, D)
    gates = shared_router(xf, w_router, b_router).astype(x.dtype)
    y = jnp.zeros_like(xf)
    for e in range(N_EXPERTS):
        h = jax.nn.silu(xf @ w_gate[e]) * (xf @ w_up[e])
        y = y + gates[:, e:e + 1] * (h @ w_down[e])
    return y.reshape(B, T, D)


def setup_inputs(seed: int = 0) -> dict:
    key = jax.random.key(seed)
    ks = jax.random.split(key, 32)

    def nrm(k, shape, scale):
        return jax.random.normal(k, shape, jnp.float32) * scale

    keep = min(WINDOW, PAST_LEN)
    u = jax.random.uniform(ks[29], (N_LRU_LAYERS, D_RNN), jnp.float32, 0.9, 0.999)
    a_base = u ** (1.0 / LRU_C)
    lam = jnp.log(a_base) - jnp.log1p(-a_base)
    return {
        "x_prompt": nrm(ks[0], (BATCH, SEQ, D_MODEL), 1.0),
        "x_sample": nrm(ks[1], (DEC_BATCH, DEC_SEQ, D_MODEL), 1.0),
        "state_conv_0": nrm(ks[2], (DEC_BATCH, CONV_WIDTH - 1, D_RNN), 1.0),
        "state_h_0": nrm(ks[3], (DEC_BATCH, D_RNN), 0.5),
        "cache_k_1": nrm(ks[4], (DEC_BATCH, keep, N_KV_HEADS, HEAD_DIM), 1.0),
        "cache_v_1": nrm(ks[5], (DEC_BATCH, keep, N_KV_HEADS, HEAD_DIM), 1.0),
        "state_conv_2": nrm(ks[6], (DEC_BATCH, CONV_WIDTH - 1, D_RNN), 1.0),
        "state_h_2": nrm(ks[7], (DEC_BATCH, D_RNN), 0.5),
        "cache_k_3": nrm(ks[8], (DEC_BATCH, keep, N_KV_HEADS, HEAD_DIM), 1.0),
        "cache_v_3": nrm(ks[9], (DEC_BATCH, keep, N_KV_HEADS, HEAD_DIM), 1.0),
        "w_router": nrm(ks[10], (D_MODEL, N_EXPERTS), D_MODEL ** -0.5),
        "b_router": nrm(ks[11], (N_EXPERTS,), 0.01),
        "ln_g": 1.0 + nrm(ks[12], (DEPTH, 2, D_MODEL), 0.02),
        "ln_b": nrm(ks[13], (DEPTH, 2, D_MODEL), 0.02),
        "lru_w_in": nrm(ks[14], (N_LRU_LAYERS, D_MODEL, 2 * D_RNN), D_MODEL ** -0.5),
        "lru_conv_w": nrm(ks[15], (N_LRU_LAYERS, CONV_WIDTH, D_RNN), CONV_WIDTH ** -0.5),
        "lru_conv_b": nrm(ks[16], (N_LRU_LAYERS, D_RNN), 0.02),
        "lru_w_gates": nrm(ks[17], (N_LRU_LAYERS, LRU_BLOCKS, LRU_BLOCK_W, 2 * LRU_BLOCK_W), LRU_BLOCK_W ** -0.5),
        "lru_b_gates": nrm(ks[18], (N_LRU_LAYERS, LRU_BLOCKS, 2 * LRU_BLOCK_W), 0.02),
        "lru_lambda": lam,
        "lru_w_out": nrm(ks[19], (N_LRU_LAYERS, D_RNN, D_MODEL), DEEPNORM_BETA * D_RNN ** -0.5),
        "attn_w_qkv": nrm(ks[20], (N_ATTN_LAYERS, D_MODEL, QKV_DIM), D_MODEL ** -0.5),
        "attn_b_qkv": nrm(ks[21], (N_ATTN_LAYERS, QKV_DIM), 0.02),
        "attn_sinks": nrm(ks[22], (N_ATTN_LAYERS, N_HEADS), 1.0),
        "attn_w_o": nrm(ks[23], (N_ATTN_LAYERS, N_HEADS * HEAD_DIM, D_MODEL), DEEPNORM_BETA * (N_HEADS * HEAD_DIM) ** -0.5),
        "moe_w_gate": nrm(ks[24], (DEPTH, N_EXPERTS, D_MODEL, D_EXPERT), D_MODEL ** -0.5),
        "moe_w_up": nrm(ks[25], (DEPTH, N_EXPERTS, D_MODEL, D_EXPERT), D_MODEL ** -0.5),
        "moe_w_down": nrm(ks[26], (DEPTH, N_EXPERTS, D_EXPERT, D_MODEL), DEEPNORM_BETA * D_EXPERT ** -0.5),
    }


def reference(x_prompt, x_sample, state_conv_0, state_h_0, cache_k_1, cache_v_1,
              state_conv_2, state_h_2, cache_k_3, cache_v_3,
              w_router, b_router, ln_g, ln_b,
              lru_w_in, lru_conv_w, lru_conv_b, lru_w_gates, lru_b_gates, lru_lambda, lru_w_out,
              attn_w_qkv, attn_b_qkv, attn_sinks, attn_w_o,
              moe_w_gate, moe_w_up, moe_w_down):
    in_state = {0: (state_conv_0, state_h_0), 1: (cache_k_1, cache_v_1),
                2: (state_conv_2, state_h_2), 3: (cache_k_3, cache_v_3)}
    new_p, new_s = {}, {}
    xp, xs = x_prompt, x_sample
    for i in range(DEPTH):
        slot = i // N_MIXERS
        st_a, st_b = in_state[i]
        if i % N_MIXERS == 0:
            lru = (lru_w_in[slot], lru_conv_w[slot], lru_conv_b[slot], lru_w_gates[slot],
                   lru_b_gates[slot], lru_lambda[slot], lru_w_out[slot])
            zero_conv = jnp.zeros((xp.shape[0], CONV_WIDTH - 1, D_RNN), xp.dtype)
            zero_h = jnp.zeros((xp.shape[0], D_RNN), st_b.dtype)
            yp, sp_a, sp_b = rglru_mixer(xp, zero_conv, zero_h, *lru)
            ys, ss_a, ss_b = rglru_mixer(xs, st_a, st_b, *lru)
        else:
            attn = (attn_w_qkv[slot], attn_b_qkv[slot], attn_sinks[slot], attn_w_o[slot])
            yp, sp_a, sp_b = swa_prompt(xp, *attn)
            ys, ss_a, ss_b = swa_sample(xs, st_a, st_b, *attn)
        new_p[i] = (sp_a, sp_b)
        new_s[i] = (ss_a, ss_b)
        xp = layer_norm(DEEPNORM_ALPHA * xp + yp, ln_g[i, 0], ln_b[i, 0])
        xs = layer_norm(DEEPNORM_ALPHA * xs + ys, ln_g[i, 0], ln_b[i, 0])
        moe_w = (w_router, b_router, moe_w_gate[i], moe_w_up[i], moe_w_down[i])
        xp = layer_norm(DEEPNORM_ALPHA * xp + moe_ffn(xp, *moe_w), ln_g[i, 1], ln_b[i, 1])
        xs = layer_norm(DEEPNORM_ALPHA * xs + moe_ffn(xs, *moe_w), ln_g[i, 1], ln_b[i, 1])
    return (xp, xs,
            new_p[0][0], new_s[0][0], new_p[0][1], new_s[0][1],
            new_p[1][0], new_s[1][0], new_p[1][1], new_s[1][1],
            new_p[2][0], new_s[2][0], new_p[2][1], new_s[2][1],
            new_p[3][0], new_s[3][0], new_p[3][1], new_s[3][1])
```

```python
import functools

import jax
import jax.numpy as jnp
from jax import lax
from jax.experimental import pallas as pl
from jax.experimental.pallas import tpu as pltpu

D_MODEL = 1024
BATCH = 8
SEQ = 2048
DEPTH = 4
DEC_BATCH = 32
DEC_SEQ = 64
PAST_LEN = 4096
CHUNK = 64
D_RNN = D_MODEL
CONV_WIDTH = 4
LRU_BLOCKS = 16
LRU_BLOCK_W = D_RNN // LRU_BLOCKS
LRU_C = 8.0
N_HEADS = 16
N_KV_HEADS = 2
HEAD_DIM = 64
KV_GROUP = N_HEADS // N_KV_HEADS
WINDOW = 128
ROPE_THETA = 10000.0
QKV_DIM = (N_HEADS + 2 * N_KV_HEADS) * HEAD_DIM
N_EXPERTS = 16
N_GROUPS = 4
EXPERTS_PER_GROUP = N_EXPERTS // N_GROUPS
D_EXPERT = 512
DEEPNORM_ALPHA = (2.0 * DEPTH) ** 0.25
LN_EPS = 1e-5

MXU_EDGE = 256
LANES = 128
SUBLANES = 8

HALF = D_MODEL // 2
ROW_TILE = 256
MOE_CHUNK = 2048
MOE_TM = 256
MOE_ROWS = 2 * MOE_CHUNK + N_EXPERTS * MOE_TM
MOE_TILES = MOE_ROWS // MOE_TM
SAMPLE_SEGS = ROW_TILE // DEC_SEQ
NEG_BIG = -1e30
VMEM_LIMIT = 48 * 1024 * 1024


def _layer_norm(x, g, b):
    mu = jnp.mean(x, -1, keepdims=True)
    xc = x - mu
    var = jnp.mean(xc * xc, -1, keepdims=True)
    return xc * lax.rsqrt(var + LN_EPS) * g + b


def _bits(x):
    return lax.bitcast_convert_type(x, jnp.int32)


def _pack_rows(x):
    hi = _bits(x[:, :HALF].astype(jnp.bfloat16).astype(jnp.float32))
    lo = _bits(x[:, HALF:].astype(jnp.bfloat16).astype(jnp.float32))
    return hi | lax.shift_right_logical(lo, 16)


def _unpack_rows(p):
    hi = lax.bitcast_convert_type(p & jnp.int32(-65536), jnp.float32)
    lo = lax.bitcast_convert_type(lax.shift_left(p, 16), jnp.float32)
    return hi, lo


def _finish_mixer(x, y, g_ref, b_ref, x1_ref, x1p_ref):
    x1 = _layer_norm(DEEPNORM_ALPHA * x + y, g_ref[...], b_ref[...])
    x1_ref[...] = x1
    x1p_ref[...] = _pack_rows(x1)


def _lru_tile(x, ext_scr, h_in, n_seg, seg, win_ref, cw_ref, cb_ref, wr_ref, wi_ref,
              br_ref, bi_ref, lam_ref, wout_ref):
    rows = n_seg * seg
    gr = jnp.dot(x.astype(jnp.bfloat16), win_ref[...], preferred_element_type=jnp.float32)
    gate_in = gr[:, :D_RNN]
    rec = gr[:, D_RNN:]
    cw = cw_ref[...]
    xcs = []
    for s in range(n_seg):
        ext_scr[s, SUBLANES:SUBLANES + seg, :] = rec[s * seg:(s + 1) * seg]
        xc = cb_ref[...] + cw[3:4] * rec[s * seg:(s + 1) * seg]
        for k in range(CONV_WIDTH - 1):
            off = SUBLANES - (CONV_WIDTH - 1) + k
            xc = xc + cw[k:k + 1] * ext_scr[s, off:off + seg, :]
        xcs.append(xc)
    xc = xcs[0] if n_seg == 1 else jnp.concatenate(xcs, axis=0)

    xcb = xc.astype(jnp.bfloat16)
    r_parts, i_parts = [], []
    for q in range(D_RNN // MXU_EDGE):
        blk = xcb[:, q * MXU_EDGE:(q + 1) * MXU_EDGE]
        r_parts.append(jnp.dot(blk, wr_ref[q], preferred_element_type=jnp.float32))
        i_parts.append(jnp.dot(blk, wi_ref[q], preferred_element_type=jnp.float32))
    r = jax.nn.sigmoid(jnp.concatenate(r_parts, axis=1) + br_ref[...])
    ig = jax.nn.sigmoid(jnp.concatenate(i_parts, axis=1) + bi_ref[...])

    z = -lam_ref[...]
    softplus = jnp.maximum(z, 0.0) + jnp.log1p(jnp.exp(-jnp.abs(z)))
    log_a = (-LRU_C * r) * softplus
    a = jnp.exp(log_a)
    b = jnp.sqrt(-jnp.tanh(log_a) * (a * a + 1.0)) * (ig * xc)

    sub = lax.broadcasted_iota(jnp.int32, (rows, D_RNN), 0) & (SUBLANES - 1)
    step = 1
    while step < SUBLANES:
        keep = sub >= step
        a_prev = pltpu.roll(a, step, 0)
        b_prev = pltpu.roll(b, step, 0)
        b = jnp.where(keep, a * b_prev + b, b)
        a = jnp.where(keep, a * a_prev, a)
        step *= 2
    groups_per_seg = seg // SUBLANES
    h_blocks, h_last = [], []
    carry = None
    for gi in range(rows // SUBLANES):
        if gi % groups_per_seg == 0:
            carry = h_in[gi // groups_per_seg]
        lo = gi * SUBLANES
        hb = b[lo:lo + SUBLANES] + a[lo:lo + SUBLANES] * carry
        carry = hb[SUBLANES - 1:SUBLANES]
        h_blocks.append(hb)
        if gi % groups_per_seg == groups_per_seg - 1:
            h_last.append(carry)
    h = jnp.concatenate(h_blocks, axis=0)

    y = h * jax.nn.gelu(gate_in, approximate=True)
    out = jnp.dot(y.astype(jnp.bfloat16), wout_ref[...], preferred_element_type=jnp.float32)
    return out, rec, h_last


def _lru_prompt_kernel(x_ref, win_ref, cw_ref, cb_ref, wr_ref, wi_ref, br_ref, bi_ref, lam_ref,
                       wout_ref, g_ref, b_ref, x1_ref, x1p_ref, conv_ref, hout_ref, ext_scr, h_scr):
    t = pl.program_id(1)

    @pl.when(t == 0)
    def _():
        ext_scr[0, 0:SUBLANES, :] = jnp.zeros((SUBLANES, D_RNN), jnp.float32)
        h_scr[...] = jnp.zeros_like(h_scr)

    x = x_ref[...]
    out, rec, h_last = _lru_tile(x, ext_scr, [h_scr[...]], 1, ROW_TILE, win_ref, cw_ref, cb_ref,
                                 wr_ref, wi_ref, br_ref, bi_ref, lam_ref, wout_ref)
    ext_scr[0, 0:SUBLANES, :] = rec[ROW_TILE - SUBLANES:]
    h_scr[...] = h_last[0]
    conv_ref[...] = rec[ROW_TILE - (CONV_WIDTH - 1):]
    hout_ref[...] = h_last[0]
    _finish_mixer(x, out, g_ref, b_ref, x1_ref, x1p_ref)


def _lru_sample_kernel(x_ref, conv_in_ref, h0_ref, win_ref, cw_ref, cb_ref, wr_ref, wi_ref, br_ref,
                       bi_ref, lam_ref, wout_ref, g_ref, b_ref, x1_ref, x1p_ref, conv_ref, hout_ref,
                       ext_scr):
    for s in range(SAMPLE_SEGS):
        ext_scr[s, 0:SUBLANES, :] = conv_in_ref[s]
    h_in = [h0_ref[s:s + 1, :] for s in range(SAMPLE_SEGS)]
    x = x_ref[...]
    out, rec, h_last = _lru_tile(x, ext_scr, h_in, SAMPLE_SEGS, DEC_SEQ, win_ref, cw_ref, cb_ref,
                                 wr_ref, wi_ref, br_ref, bi_ref, lam_ref, wout_ref)
    for s in range(SAMPLE_SEGS):
        end = (s + 1) * DEC_SEQ
        conv_ref[s] = rec[end - (CONV_WIDTH - 1):end]
        hout_ref[s:s + 1, :] = h_last[s]
    _finish_mixer(x, out, g_ref, b_ref, x1_ref, x1p_ref)


def _const_spec(shape):
    nd = len(shape)
    return pl.BlockSpec(shape, lambda *_: (0,) * nd)


def _lru_weight_specs():
    return [
        _const_spec((D_MODEL, 2 * D_RNN)),
        _const_spec((CONV_WIDTH, D_RNN)),
        _const_spec((1, D_RNN)),
        _const_spec((D_RNN // MXU_EDGE, MXU_EDGE, MXU_EDGE)),
        _const_spec((D_RNN // MXU_EDGE, MXU_EDGE, MXU_EDGE)),
        _const_spec((1, D_RNN)),
        _const_spec((1, D_RNN)),
        _const_spec((1, D_RNN)),
        _const_spec((D_RNN, D_MODEL)),
        _const_spec((1, D_MODEL)),
        _const_spec((1, D_MODEL)),
    ]


def _lru_prompt(x, w, ln_g, ln_b):
    nt = SEQ // ROW_TILE
    row = lambda b, t: (b, t, 0)
    per_b = lambda b, t: (b, 0, 0)
    return pl.pallas_call(
        _lru_prompt_kernel,
        grid=(BATCH, nt),
        in_specs=[pl.BlockSpec((None, ROW_TILE, D_MODEL), row)] + _lru_weight_specs(),
        out_specs=[
            pl.BlockSpec((None, ROW_TILE, D_MODEL), row),
            pl.BlockSpec((None, ROW_TILE, HALF), row),
            pl.BlockSpec((None, CONV_WIDTH - 1, D_RNN), per_b),
            pl.BlockSpec((None, 1, D_RNN), per_b),
        ],
        out_shape=[
            jax.ShapeDtypeStruct((BATCH, SEQ, D_MODEL), jnp.float32),
            jax.ShapeDtypeStruct((BATCH, SEQ, HALF), jnp.int32),
            jax.ShapeDtypeStruct((BATCH, CONV_WIDTH - 1, D_RNN), jnp.float32),
            jax.ShapeDtypeStruct((BATCH, 1, D_RNN), jnp.float32),
        ],
        scratch_shapes=[
            pltpu.VMEM((1, SUBLANES + ROW_TILE, D_RNN), jnp.float32),
            pltpu.VMEM((1, D_RNN), jnp.float32),
        ],
        compiler_params=pltpu.CompilerParams(
            dimension_semantics=("parallel", "arbitrary"), vmem_limit_bytes=VMEM_LIMIT),
        name="lru_prompt",
    )(x, *w, ln_g, ln_b)


def _lru_sample(x, conv_pad, h0, w, ln_g, ln_b):
    nt = DEC_BATCH // SAMPLE_SEGS
    row = lambda t: (t, 0, 0)
    row4 = lambda t: (t, 0, 0, 0)
    return pl.pallas_call(
        _lru_sample_kernel,
        grid=(nt,),
        in_specs=[
            pl.BlockSpec((None, ROW_TILE, D_MODEL), row),
            pl.BlockSpec((None, SAMPLE_SEGS, SUBLANES, D_RNN), row4),
            pl.BlockSpec((None, SAMPLE_SEGS, D_RNN), row),
        ] + _lru_weight_specs(),
        out_specs=[
            pl.BlockSpec((None, ROW_TILE, D_MODEL), row),
            pl.BlockSpec((None, ROW_TILE, HALF), row),
            pl.BlockSpec((None, SAMPLE_SEGS, CONV_WIDTH - 1, D_RNN), row4),
            pl.BlockSpec((None, SAMPLE_SEGS, D_RNN), row),
        ],
        out_shape=[
            jax.ShapeDtypeStruct((nt, ROW_TILE, D_MODEL), jnp.float32),
            jax.ShapeDtypeStruct((nt, ROW_TILE, HALF), jnp.int32),
            jax.ShapeDtypeStruct((nt, SAMPLE_SEGS, CONV_WIDTH - 1, D_RNN), jnp.float32),
            jax.ShapeDtypeStruct((nt, SAMPLE_SEGS, D_RNN), jnp.float32),
        ],
        scratch_shapes=[pltpu.VMEM((SAMPLE_SEGS, SUBLANES + DEC_SEQ, D_RNN), jnp.float32)],
        compiler_params=pltpu.CompilerParams(
            dimension_semantics=("parallel",), vmem_limit_bytes=VMEM_LIMIT),
        name="lru_sample",
    )(x, conv_pad, h0, *w, ln_g, ln_b)


def _rope(x, cos, sin_signed):
    lane = lax.broadcasted_iota(jnp.int32, (x.shape[0], LANES), 1)
    first_half = (lane & (HEAD_DIM - 1)) < HEAD_DIM // 2
    cols = []
    for c in range(x.shape[1] // LANES):
        xc = x[:, c * LANES:(c + 1) * LANES]
        partner = jnp.where(first_half, pltpu.roll(xc, LANES - HEAD_DIM // 2, 1),
                            pltpu.roll(xc, HEAD_DIM // 2, 1))
        cols.append(xc * cos + partner * sin_signed)
    return cols[0] if len(cols) == 1 else jnp.concatenate(cols, axis=1)


def _dup_heads(kv):
    lane = lax.broadcasted_iota(jnp.int32, kv.shape, 1)
    low = lane < HEAD_DIM
    swapped = pltpu.roll(kv, HEAD_DIM, 1)
    return (jnp.where(low, kv, swapped).astype(jnp.bfloat16),
            jnp.where(low, swapped, kv).astype(jnp.bfloat16))


def _attend_chunk(q, k_wins, v_wins, valid, sinks_ref):
    lane = lax.broadcasted_iota(jnp.int32, (CHUNK, LANES), 1)
    low = lane < HEAD_DIM
    zero = jnp.zeros((CHUNK, LANES), jnp.bfloat16)
    cols_per_group = KV_GROUP * HEAD_DIM // LANES
    out_cols = []
    for g in range(N_KV_HEADS):
        stacked = []
        for c in range(g * cols_per_group, (g + 1) * cols_per_group):
            qc = q[:, c * LANES:(c + 1) * LANES]
            stacked.append(jnp.where(low, qc, zero))
            stacked.append(jnp.where(low, zero, qc))
        qst = jnp.concatenate(stacked, axis=0)
        s = lax.dot_general(qst, k_wins[g], (((1,), (1,)), ((), ())),
                            preferred_element_type=jnp.float32)
        if valid is not None:
            s = jnp.where(valid, s, NEG_BIG)
        probs = []
        for hh in range(KV_GROUP):
            sh = s[hh * CHUNK:(hh + 1) * CHUNK]
            sink = sinks_ref[g * KV_GROUP + hh]
            m = jnp.maximum(jnp.max(sh, -1, keepdims=True), sink)
            p = jnp.exp(sh - m)
            denom = jnp.sum(p, -1, keepdims=True) + jnp.exp(sink - m)
            probs.append((p * (1.0 / denom)).astype(jnp.bfloat16))
        pst = jnp.concatenate(probs, axis=0)
        o = jnp.dot(pst, v_wins[g], preferred_element_type=jnp.float32)
        for j in range(cols_per_group):
            oa = o[(2 * j) * CHUNK:(2 * j + 1) * CHUNK]
            ob = o[(2 * j + 1) * CHUNK:(2 * j + 2) * CHUNK]
            out_cols.append(jnp.where(low, oa, ob))
    return jnp.concatenate(out_cols, axis=1)


def _project_qkv(x, wqkv_ref, bqkv_ref, cos_ref, sin_ref):
    qkv = jnp.dot(x.astype(jnp.bfloat16), wqkv_ref[...], preferred_element_type=jnp.float32)
    qkv = qkv + bqkv_ref[...]
    nq = N_HEADS * HEAD_DIM
    nk = N_KV_HEADS * HEAD_DIM
    cos = cos_ref[...]
    sin = sin_ref[...]
    q = _rope(qkv[:, :nq], cos, sin)
    k = _rope(qkv[:, nq:nq + nk], cos, sin)
    v = qkv[:, nq + nk:]
    qs = (q * (HEAD_DIM ** -0.5)).astype(jnp.bfloat16)
    return qs, k, v


def _swa_prompt_kernel(x_ref, wqkv_ref, bqkv_ref, cos_ref, sin_ref, sinks_ref, wo_ref, g_ref, b_ref,
                       x1_ref, x1p_ref, kout_ref, vout_ref, k0_scr, k1_scr, v0_scr, v1_scr):
    t = pl.program_id(1)
    scrs = (k0_scr, k1_scr, v0_scr, v1_scr)

    @pl.when(t == 0)
    def _():
        for scr in scrs:
            scr[0:WINDOW, :] = jnp.zeros((WINDOW, LANES), jnp.bfloat16)

    x = x_ref[...]
    qs, k, v = _project_qkv(x, wqkv_ref, bqkv_ref, cos_ref, sin_ref)
    kout_ref[...] = k[ROW_TILE - WINDOW:]
    vout_ref[...] = v[ROW_TILE - WINDOW:]
    k0, k1 = _dup_heads(k)
    v0, v1 = _dup_heads(v)
    for scr, val in zip(scrs, (k0, k1, v0, v1)):
        scr[WINDOW:WINDOW + ROW_TILE, :] = val

    span = WINDOW + CHUNK
    key_pos = lax.broadcasted_iota(jnp.int32, (1, span), 1)
    chunks = []
    for ci in range(ROW_TILE // CHUNK):
        lo = ci * CHUNK
        valid = (t * ROW_TILE + lo - WINDOW + key_pos) >= 0
        k_wins = (k0_scr[lo:lo + span, :], k1_scr[lo:lo + span, :])
        v_wins = (v0_scr[lo:lo + span, :], v1_scr[lo:lo + span, :])
        chunks.append(_attend_chunk(qs[lo:lo + CHUNK], k_wins, v_wins, valid, sinks_ref))
    o = jnp.concatenate(chunks, axis=0)
    for scr in scrs:
        scr[0:WINDOW, :] = scr[ROW_TILE:ROW_TILE + WINDOW, :]
    out = jnp.dot(o.astype(jnp.bfloat16), wo_ref[...], preferred_element_type=jnp.float32)
    _finish_mixer(x, out, g_ref, b_ref, x1_ref, x1p_ref)


def _swa_sample_kernel(x_ref, ck_ref, cv_ref, wqkv_ref, bqkv_ref, cos_ref, sin_ref, sinks_ref, wo_ref,
                       g_ref, b_ref, x1_ref, x1p_ref, kout_ref, vout_ref):
    x = x_ref[...]
    qs, k, v = _project_qkv(x, wqkv_ref, bqkv_ref, cos_ref, sin_ref)
    knew = _dup_heads(k)
    vnew = _dup_heads(v)
    chunks = []
    for s in range(SAMPLE_SEGS):
        lo = s * DEC_SEQ
        ck = ck_ref[s]
        cv = cv_ref[s]
        kold = _dup_heads(ck)
        vold = _dup_heads(cv)
        k_wins = [jnp.concatenate([kold[g], knew[g][lo:lo + DEC_SEQ]], axis=0) for g in range(N_KV_HEADS)]
        v_wins = [jnp.concatenate([vold[g], vnew[g][lo:lo + DEC_SEQ]], axis=0) for g in range(N_KV_HEADS)]
        chunks.append(_attend_chunk(qs[lo:lo + DEC_SEQ], k_wins, v_wins, None, sinks_ref))
        kout_ref[s, 0:WINDOW - DEC_SEQ, :] = ck[DEC_SEQ:]
        kout_ref[s, WINDOW - DEC_SEQ:WINDOW, :] = k[lo:lo + DEC_SEQ]
        vout_ref[s, 0:WINDOW - DEC_SEQ, :] = cv[DEC_SEQ:]
        vout_ref[s, WINDOW - DEC_SEQ:WINDOW, :] = v[lo:lo + DEC_SEQ]
    o = jnp.concatenate(chunks, axis=0)
    out = jnp.dot(o.astype(jnp.bfloat16), wo_ref[...], preferred_element_type=jnp.float32)
    _finish_mixer(x, out, g_ref, b_ref, x1_ref, x1p_ref)


def _swa_weight_specs(cos_spec):
    return [
        _const_spec((D_MODEL, QKV_DIM)),
        _const_spec((1, QKV_DIM)),
        cos_spec,
        cos_spec,
        pl.BlockSpec(memory_space=pltpu.SMEM),
        _const_spec((N_HEADS * HEAD_DIM, D_MODEL)),
        _const_spec((1, D_MODEL)),
        _const_spec((1, D_MODEL)),
    ]


def _swa_prompt(x, w, cos, sin, ln_g, ln_b):
    wqkv, bqkv, sinks, wo = w
    nt = SEQ // ROW_TILE
    row = lambda b, t: (b, t, 0)
    per_b = lambda b, t: (b, 0, 0)
    kv_lanes = N_KV_HEADS * HEAD_DIM
    return pl.pallas_call(
        _swa_prompt_kernel,
        grid=(BATCH, nt),
        in_specs=[pl.BlockSpec((None, ROW_TILE, D_MODEL), row)]
        + _swa_weight_specs(pl.BlockSpec((ROW_TILE, LANES), lambda b, t: (t, 0))),
        out_specs=[
            pl.BlockSpec((None, ROW_TILE, D_MODEL), row),
            pl.BlockSpec((None, ROW_TILE, HALF), row),
            pl.BlockSpec((None, WINDOW, kv_lanes), per_b),
            pl.BlockSpec((None, WINDOW, kv_lanes), per_b),
        ],
        out_shape=[
            jax.ShapeDtypeStruct((BATCH, SEQ, D_MODEL), jnp.float32),
            jax.ShapeDtypeStruct((BATCH, SEQ, HALF), jnp.int32),
            jax.ShapeDtypeStruct((BATCH, WINDOW, kv_lanes), jnp.float32),
            jax.ShapeDtypeStruct((BATCH, WINDOW, kv_lanes), jnp.float32),
        ],
        scratch_shapes=[pltpu.VMEM((WINDOW + ROW_TILE, LANES), jnp.bfloat16) for _ in range(4)],
        compiler_params=pltpu.CompilerParams(
            dimension_semantics=("parallel", "arbitrary"), vmem_limit_bytes=VMEM_LIMIT),
        name="swa_prompt",
    )(x, wqkv, bqkv, cos, sin, sinks, wo, ln_g, ln_b)


def _swa_sample(x, ck, cv, w, cos, sin, ln_g, ln_b):
    wqkv, bqkv, sinks, wo = w
    nt = DEC_BATCH // SAMPLE_SEGS
    row = lambda t: (t, 0, 0)
    row4 = lambda t: (t, 0, 0, 0)
    kv_lanes = N_KV_HEADS * HEAD_DIM
    cache_spec = pl.BlockSpec((None, SAMPLE_SEGS, WINDOW, kv_lanes), row4)
    return pl.pallas_call(
        _swa_sample_kernel,
        grid=(nt,),
        in_specs=[pl.BlockSpec((None, ROW_TILE, D_MODEL), row), cache_spec, cache_spec]
        + _swa_weight_specs(_const_spec((ROW_TILE, LANES))),
        out_specs=[
            pl.BlockSpec((None, ROW_TILE, D_MODEL), row),
            pl.BlockSpec((None, ROW_TILE, HALF), row),
            cache_spec,
            cache_spec,
        ],
        out_shape=[
            jax.ShapeDtypeStruct((nt, ROW_TILE, D_MODEL), jnp.float32),
            jax.ShapeDtypeStruct((nt, ROW_TILE, HALF), jnp.int32),
            jax.ShapeDtypeStruct((nt, SAMPLE_SEGS, WINDOW, kv_lanes), jnp.float32),
            jax.ShapeDtypeStruct((nt, SAMPLE_SEGS, WINDOW, kv_lanes), jnp.float32),
        ],
        compiler_params=pltpu.CompilerParams(
            dimension_semantics=("parallel",), vmem_limit_bytes=VMEM_LIMIT),
        name="swa_sample",
    )(x, ck, cv, wqkv, bqkv, cos, sin, sinks, wo, ln_g, ln_b)


def _router_kernel(x_ref, wr_ref, br_ref, idx_ref, gate_ref):
    logits = lax.dot_general(wr_ref[...].astype(jnp.bfloat16), x_ref[...].astype(jnp.bfloat16),
                             (((1,), (1,)), ((), ())), preferred_element_type=jnp.float32)
    m = jnp.max(logits, axis=0, keepdims=True)
    e = jnp.exp(logits - m)
    probs = e / jnp.sum(e, axis=0, keepdims=True)
    sel = probs + br_ref[...]
    p = [probs[i:i + 1] for i in range(N_EXPERTS)]
    s = [sel[i:i + 1] for i in range(N_EXPERTS)]

    def first_argmax(vals):
        best, arg = vals[0], jnp.zeros_like(vals[0], dtype=jnp.int32)
        for i in range(1, len(vals)):
            better = vals[i] > best
            best = jnp.where(better, vals[i], best)
            arg = jnp.where(better, i, arg)
        return arg

    group_scores = []
    for gidx in range(N_GROUPS):
        v = s[gidx * EXPERTS_PER_GROUP:(gidx + 1) * EXPERTS_PER_GROUP]
        best = v[0] + v[1]
        for i in range(EXPERTS_PER_GROUP):
            for j in range(i + 1, EXPERTS_PER_GROUP):
                if (i, j) != (0, 1):
                    best = jnp.maximum(best, v[i] + v[j])
        group_scores.append(best)
    best_group = first_argmax(group_scores)
    neg_inf = jnp.full_like(s[0], -jnp.inf)
    masked = [jnp.where(best_group == (i // EXPERTS_PER_GROUP), s[i], neg_inf) for i in range(N_EXPERTS)]
    idx1 = first_argmax(masked)
    masked2 = [jnp.where(idx1 == i, neg_inf, masked[i]) for i in range(N_EXPERTS)]
    idx2 = first_argmax(masked2)
    zero = jnp.zeros_like(p[0])
    w1, w2 = zero, zero
    for i in range(N_EXPERTS):
        w1 = w1 + jnp.where(idx1 == i, p[i], zero)
        w2 = w2 + jnp.where(idx2 == i, p[i], zero)
    tot = w1 + w2
    idx_ref[0:1, :] = idx1
    idx_ref[1:2, :] = idx2
    gate_ref[0:1, :] = w1 / tot
    gate_ref[1:2, :] = w2 / tot


def _router(x2d, wr_t, br):
    n = x2d.shape[0]
    rows = 2 * ROW_TILE
    return pl.pallas_call(
        _router_kernel,
        grid=(n // rows,),
        in_specs=[
            pl.BlockSpec((rows, D_MODEL), lambda i: (i, 0)),
            _const_spec((N_EXPERTS, D_MODEL)),
            _const_spec((N_EXPERTS, 1)),
        ],
        out_specs=[pl.BlockSpec((2, rows), lambda i: (0, i)), pl.BlockSpec((2, rows), lambda i: (0, i))],
        out_shape=[jax.ShapeDtypeStruct((2, n), jnp.int32), jax.ShapeDtypeStruct((2, n), jnp.float32)],
        compiler_params=pltpu.CompilerParams(
            dimension_semantics=("parallel",), vmem_limit_bytes=VMEM_LIMIT),
        name="router",
    )(x2d, wr_t, br)


def _dispatch_plan(idx, gate, n_chunks):
    e = idx.reshape(2, n_chunks, MOE_CHUNK).transpose(1, 0, 2).reshape(n_chunks, 2 * MOE_CHUNK)
    gt = gate.reshape(2, n_chunks, MOE_CHUNK).transpose(1, 0, 2).reshape(n_chunks, 2 * MOE_CHUNK)
    onehot = (e[:, :, None] == jnp.arange(N_EXPERTS, dtype=jnp.int32)).astype(jnp.int32)
    csum = jnp.cumsum(onehot, axis=1)
    rank = jnp.sum((csum - onehot) * onehot, axis=-1)
    counts = csum[:, -1, :]
    padded = ((counts + MOE_TM - 1) // MOE_TM) * MOE_TM
    ends = jnp.cumsum(padded, axis=-1)
    starts = ends - padded
    pos = jnp.take_along_axis(starts, e, axis=1) + rank
    pair = jnp.arange(2 * MOE_CHUNK, dtype=jnp.int32)[None, :]
    cidx = jnp.arange(n_chunks, dtype=jnp.int32)[:, None]
    row_pair = jnp.full((n_chunks, MOE_ROWS), 2 * MOE_CHUNK, jnp.int32).at[cidx, pos].set(
        jnp.broadcast_to(pair, pos.shape))
    row_gate = jnp.zeros((n_chunks, MOE_ROWS), jnp.float32).at[cidx, pos].set(gt)
    tile_start = jnp.arange(MOE_TILES, dtype=jnp.int32) * MOE_TM
    tile_expert = jnp.sum((ends[:, None, :] <= tile_start[None, :, None]).astype(jnp.int32), axis=-1)
    tile_expert = jnp.minimum(tile_expert, N_EXPERTS - 1)
    n_active = ends[:, -1] // MOE_TM
    return (tile_expert.reshape(-1).astype(jnp.int32), n_active.astype(jnp.int32),
            row_pair.reshape(n_chunks * MOE_TILES, 1, MOE_TM),
            row_gate.reshape(n_chunks * MOE_ROWS, 1))


def _moe_kernel(te_ref, nact_ref, xp_ref, pair_ref, rgate_ref, wgu_ref, wd_ref, x1_ref,
                g_ref, b_ref, out_ref, xs_scr, yo_scr, y_scr):
    c = pl.program_id(0)
    i = pl.program_id(1)

    @pl.when(jnp.logical_and(i < MOE_TILES, i < nact_ref[c]))
    def _():
        def gather(j, carry):
            tok = pair_ref[0, j] & (MOE_CHUNK - 1)
            xs_scr[pl.ds(j, 1), :] = xp_ref[pl.ds(tok, 1), :]
            return carry
        lax.fori_loop(0, MOE_TM, gather, 0, unroll=8)
        xa, xb = _unpack_rows(xs_scr[...])
        hgu = (jnp.dot(xa.astype(jnp.bfloat16), wgu_ref[0:HALF, :], preferred_element_type=jnp.float32)
               + jnp.dot(xb.astype(jnp.bfloat16), wgu_ref[HALF:, :], preferred_element_type=jnp.float32))
        act = (jax.nn.silu(hgu[:, :D_EXPERT]) * hgu[:, D_EXPERT:]).astype(jnp.bfloat16)
        yo_scr[...] = jnp.dot(act, wd_ref[...], preferred_element_type=jnp.float32) * rgate_ref[...]

        def scatter(j, carry):
            y_scr[pl.ds(pair_ref[0, j], 1), :] = yo_scr[pl.ds(j, 1), :]
            return carry
        lax.fori_loop(0, MOE_TM, scatter, 0, unroll=8)

    @pl.when(i >= MOE_TILES)
    def _():
        lo = pl.multiple_of((i - MOE_TILES) * ROW_TILE, ROW_TILE)
        moe = y_scr[pl.ds(lo, ROW_TILE), :] + y_scr[pl.ds(MOE_CHUNK + lo, ROW_TILE), :]
        out_ref[...] = _layer_norm(DEEPNORM_ALPHA * x1_ref[...] + moe, g_ref[...], b_ref[...])


def _moe(x1, x1p, plan, wgu, wd, ln_g, ln_b):
    n_chunks = x1.shape[0]
    tile_expert, n_active, row_pair, row_gate = plan
    nsub = MOE_CHUNK // ROW_TILE

    def tile_of(c, i):
        return c * MOE_TILES + jnp.minimum(i, MOE_TILES - 1)

    def sub_of(i):
        return jnp.maximum(i - MOE_TILES, 0)

    grid_spec = pltpu.PrefetchScalarGridSpec(
        num_scalar_prefetch=2,
        grid=(n_chunks, MOE_TILES + nsub),
        in_specs=[
            pl.BlockSpec((None, MOE_CHUNK, HALF), lambda c, i, te, na: (c, 0, 0)),
            pl.BlockSpec((None, 1, MOE_TM), lambda c, i, te, na: (tile_of(c, i), 0, 0),
                         memory_space=pltpu.SMEM),
            pl.BlockSpec((MOE_TM, 1), lambda c, i, te, na: (tile_of(c, i), 0)),
            pl.BlockSpec((None, D_MODEL, 2 * D_EXPERT), lambda c, i, te, na: (te[tile_of(c, i)], 0, 0)),
            pl.BlockSpec((None, D_EXPERT, D_MODEL), lambda c, i, te, na: (te[tile_of(c, i)], 0, 0)),
            pl.BlockSpec((None, ROW_TILE, D_MODEL), lambda c, i, te, na: (c, sub_of(i), 0)),
            pl.BlockSpec((1, D_MODEL), lambda c, i, te, na: (0, 0)),
            pl.BlockSpec((1, D_MODEL), lambda c, i, te, na: (0, 0)),
        ],
        out_specs=pl.BlockSpec((None, ROW_TILE, D_MODEL), lambda c, i, te, na: (c, sub_of(i), 0)),
        scratch_shapes=[
            pltpu.VMEM((MOE_TM, HALF), jnp.int32),
            pltpu.VMEM((MOE_TM, D_MODEL), jnp.float32),
            pltpu.VMEM((2 * MOE_CHUNK + SUBLANES, D_MODEL), jnp.float32),
        ],
    )
    return pl.pallas_call(
        _moe_kernel,
        grid_spec=grid_spec,
        out_shape=jax.ShapeDtypeStruct((n_chunks, MOE_CHUNK, D_MODEL), jnp.float32),
        compiler_params=pltpu.CompilerParams(
            dimension_semantics=("arbitrary", "arbitrary"), vmem_limit_bytes=VMEM_LIMIT),
        name="moe",
    )(tile_expert, n_active, x1p, row_pair, row_gate, wgu, wd, x1, ln_g, ln_b)


def _moe_layer(x1, x1p, wr_t, br, wgu, wd, ln_g, ln_b):
    n_chunks = x1.shape[0]
    idx, gate = _router(x1.reshape(n_chunks * MOE_CHUNK, D_MODEL), wr_t, br)
    plan = _dispatch_plan(idx, gate, n_chunks)
    return _moe(x1, x1p, plan, wgu, wd, ln_g, ln_b)


def _rope_tables(pos):
    half = HEAD_DIM // 2
    inv_freq = ROPE_THETA ** (-jnp.arange(half, dtype=jnp.float32) / half)
    ang = pos.astype(jnp.float32)[:, None] * inv_freq[None, :]
    cos = jnp.cos(ang)
    sin = jnp.sin(ang)
    reps = LANES // HEAD_DIM
    cos_t = jnp.tile(jnp.concatenate([cos, cos], axis=1), (1, reps))
    sin_t = jnp.tile(jnp.concatenate([-sin, sin], axis=1), (1, reps))
    return cos_t, sin_t


def _block_diag_tiles(w):
    per = MXU_EDGE // LRU_BLOCK_W
    w4 = w.reshape(LRU_BLOCKS // per, per, LRU_BLOCK_W, LRU_BLOCK_W)
    eye = jnp.eye(per, dtype=w.dtype)
    return jnp.einsum("qaij,ab->qaibj", w4, eye).reshape(LRU_BLOCKS // per, MXU_EDGE, MXU_EDGE)


def kernel(x_prompt, x_sample, state_conv_0, state_h_0, cache_k_1, cache_v_1, state_conv_2, state_h_2,
           cache_k_3, cache_v_3, w_router, b_router, ln_g, ln_b, lru_w_in, lru_conv_w, lru_conv_b,
           lru_w_gates, lru_b_gates, lru_lambda, lru_w_out, attn_w_qkv, attn_b_qkv, attn_sinks, attn_w_o,
           moe_w_gate, moe_w_up, moe_w_down):
    bf = jnp.bfloat16
    in_state = {0: (state_conv_0, state_h_0), 1: (cache_k_1, cache_v_1),
                2: (state_conv_2, state_h_2), 3: (cache_k_3, cache_v_3)}
    nt_s = DEC_BATCH // SAMPLE_SEGS
    kv_lanes = N_KV_HEADS * HEAD_DIM

    wr_t = w_router.T
    br = b_router.reshape(N_EXPERTS, 1)
    cos_p, sin_p = _rope_tables(jnp.arange(SEQ))
    cos_s, sin_s = _rope_tables(PAST_LEN + jnp.arange(DEC_SEQ))
    cos_s = jnp.tile(cos_s, (SAMPLE_SEGS, 1))
    sin_s = jnp.tile(sin_s, (SAMPLE_SEGS, 1))

    xp = x_prompt
    xs = x_sample.reshape(nt_s, ROW_TILE, D_MODEL)
    new_p, new_s = {}, {}
    for i in range(DEPTH):
        slot = i // 2
        g0 = ln_g[i, 0].reshape(1, D_MODEL)
        b0 = ln_b[i, 0].reshape(1, D_MODEL)
        g1 = ln_g[i, 1].reshape(1, D_MODEL)
        b1 = ln_b[i, 1].reshape(1, D_MODEL)
        st_a, st_b = in_state[i]
        if i % 2 == 0:
            gates = lru_w_gates[slot]
            w = (lru_w_in[slot].astype(bf), lru_conv_w[slot], lru_conv_b[slot].reshape(1, D_RNN),
                 _block_diag_tiles(gates[:, :, :LRU_BLOCK_W]).astype(bf),
                 _block_diag_tiles(gates[:, :, LRU_BLOCK_W:]).astype(bf),
                 lru_b_gates[slot][:, :LRU_BLOCK_W].reshape(1, D_RNN),
                 lru_b_gates[slot][:, LRU_BLOCK_W:].reshape(1, D_RNN),
                 lru_lambda[slot].reshape(1, D_RNN), lru_w_out[slot].astype(bf))
            xp1, xp1p, conv_p, h_p = _lru_prompt(xp, w, g0, b0)
            conv_pad = jnp.pad(st_a, ((0, 0), (SUBLANES - (CONV_WIDTH - 1), 0), (0, 0)))
            xs1, xs1p, conv_s, h_s = _lru_sample(
                xs, conv_pad.reshape(nt_s, SAMPLE_SEGS, SUBLANES, D_RNN),
                st_b.reshape(nt_s, SAMPLE_SEGS, D_RNN), w, g0, b0)
            new_p[i] = (conv_p, h_p.reshape(BATCH, D_RNN))
            new_s[i] = (conv_s.reshape(DEC_BATCH, CONV_WIDTH - 1, D_RNN), h_s.reshape(DEC_BATCH, D_RNN))
        else:
            w = (attn_w_qkv[slot].astype(bf), attn_b_qkv[slot].reshape(1, QKV_DIM), attn_sinks[slot],
                 attn_w_o[slot].astype(bf))
            xp1, xp1p, k_p, v_p = _swa_prompt(xp, w, cos_p, sin_p, g0, b0)
            xs1, xs1p, k_s, v_s = _swa_sample(
                xs, st_a.reshape(nt_s, SAMPLE_SEGS, WINDOW, kv_lanes),
                st_b.reshape(nt_s, SAMPLE_SEGS, WINDOW, kv_lanes), w, cos_s, sin_s, g0, b0)
            new_p[i] = (k_p.reshape(BATCH, WINDOW, N_KV_HEADS, HEAD_DIM),
                        v_p.reshape(BATCH, WINDOW, N_KV_HEADS, HEAD_DIM))
            new_s[i] = (k_s.reshape(DEC_BATCH, WINDOW, N_KV_HEADS, HEAD_DIM),
                        v_s.reshape(DEC_BATCH, WINDOW, N_KV_HEADS, HEAD_DIM))
        wgu = jnp.concatenate([moe_w_gate[i], moe_w_up[i]], axis=-1).astype(bf)
        wd = moe_w_down[i].astype(bf)
        xp = _moe_layer(xp1, xp1p, wr_t, br, wgu, wd, g1, b1)
        xs = _moe_layer(xs1.reshape(1, MOE_CHUNK, D_MODEL), xs1p.reshape(1, MOE_CHUNK, HALF),
                        wr_t, br, wgu, wd, g1, b1).reshape(nt_s, ROW_TILE, D_MODEL)
    return (xp, xs.reshape(DEC_BATCH, DEC_SEQ, D_MODEL),
            new_p[0][0], new_s[0][0], new_p[0][1], new_s[0][1],
            new_p[1][0], new_s[1][0], new_p[1][1], new_s[1][1],
            new_p[2][0], new_s[2][0], new_p[2][1], new_s[2][1],
            new_p[3][0], new_s[3][0], new_p[3][1], new_s[3][1])
```

```python
import jax
import jax.numpy as jnp
from jax import lax
from jax.experimental import pallas as pl
from jax.experimental.pallas import tpu as pltpu

D_MODEL = 1024
BATCH = 8
SEQ = 2048
DEPTH = 4
DEC_BATCH = 32
DEC_SEQ = 64
PAST_LEN = 4096
CHUNK = 64
D_RNN = D_MODEL
CONV_WIDTH = 4
LRU_BLOCKS = 16
LRU_BLOCK_W = D_RNN // LRU_BLOCKS
LRU_C = 8.0
N_HEADS = 16
N_KV_HEADS = 2
HEAD_DIM = 64
KV_GROUP = N_HEADS // N_KV_HEADS
WINDOW = 128
ROPE_THETA = 10000.0
QKV_DIM = (N_HEADS + 2 * N_KV_HEADS) * HEAD_DIM
N_EXPERTS = 16
N_GROUPS = 4
EXPERTS_PER_GROUP = N_EXPERTS // N_GROUPS
D_EXPERT = 512
DEEPNORM_ALPHA = (2.0 * DEPTH) ** 0.25
LN_EPS = 1e-5

MXU_EDGE = 256
LANES = 128
SUBLANES = 8

ROW_TILE = 256
MOE_CHUNK = SEQ
N_CHUNKS = BATCH + DEC_BATCH * DEC_SEQ // MOE_CHUNK
SAMPLE_CHUNK = BATCH
MOE_TM = 256
MOE_ROWS = 2 * MOE_CHUNK + N_EXPERTS * MOE_TM
MOE_TILES = MOE_ROWS // MOE_TM
COMBINE_STEPS = MOE_CHUNK // ROW_TILE
JUNK_PAIR = 2 * MOE_CHUNK
SAMPLE_SEGS = ROW_TILE // DEC_SEQ
NEG_BIG = -1e30
VMEM_LIMIT = 52 * 1024 * 1024

assert DEC_BATCH * DEC_SEQ == MOE_CHUNK and MOE_TILES <= LANES


def _layer_norm(x, g, b):
    mu = jnp.mean(x, -1, keepdims=True)
    xc = x - mu
    var = jnp.mean(xc * xc, -1, keepdims=True)
    return xc * lax.rsqrt(var + LN_EPS) * g + b


def _finish_mixer(x, y, g_ref, b_ref, x1_ref):
    x1_ref[...] = _layer_norm(DEEPNORM_ALPHA * x + y, g_ref[...], b_ref[...])


def _const_spec(shape):
    nd = len(shape)
    return pl.BlockSpec(shape, lambda *_: (0,) * nd)


_ACT_SHAPE = jax.ShapeDtypeStruct((N_CHUNKS, MOE_CHUNK, D_MODEL), jnp.float32)


def _lru_tile(x, ext_scr, h_in, n_seg, seg, win_ref, cw_ref, cb_ref, wr_ref, wi_ref,
              br_ref, bi_ref, lam_ref, wout_ref):
    rows = n_seg * seg
    gr = jnp.dot(x.astype(jnp.bfloat16), win_ref[...], preferred_element_type=jnp.float32)
    gate_in = gr[:, :D_RNN]
    rec = gr[:, D_RNN:]
    cw = cw_ref[...]
    xcs = []
    for s in range(n_seg):
        ext_scr[s, SUBLANES:SUBLANES + seg, :] = rec[s * seg:(s + 1) * seg]
        xc = cb_ref[...] + cw[3:4] * rec[s * seg:(s + 1) * seg]
        for k in range(CONV_WIDTH - 1):
            off = SUBLANES - (CONV_WIDTH - 1) + k
            xc = xc + cw[k:k + 1] * ext_scr[s, off:off + seg, :]
        xcs.append(xc)
    xc = xcs[0] if n_seg == 1 else jnp.concatenate(xcs, axis=0)

    xcb = xc.astype(jnp.bfloat16)
    r_parts, i_parts = [], []
    for q in range(D_RNN // MXU_EDGE):
        blk = xcb[:, q * MXU_EDGE:(q + 1) * MXU_EDGE]
        r_parts.append(jnp.dot(blk, wr_ref[q], preferred_element_type=jnp.float32))
        i_parts.append(jnp.dot(blk, wi_ref[q], preferred_element_type=jnp.float32))
    r = jax.nn.sigmoid(jnp.concatenate(r_parts, axis=1) + br_ref[...])
    ig = jax.nn.sigmoid(jnp.concatenate(i_parts, axis=1) + bi_ref[...])

    z = -lam_ref[...]
    softplus = jnp.maximum(z, 0.0) + jnp.log1p(jnp.exp(-jnp.abs(z)))
    log_a = (-LRU_C * r) * softplus
    a = jnp.exp(log_a)
    b = jnp.sqrt(-jnp.tanh(log_a) * (a * a + 1.0)) * (ig * xc)

    sub = lax.broadcasted_iota(jnp.int32, (rows, D_RNN), 0) & (SUBLANES - 1)
    step = 1
    while step < SUBLANES:
        keep = sub >= step
        a_prev = pltpu.roll(a, step, 0)
        b_prev = pltpu.roll(b, step, 0)
        b = jnp.where(keep, a * b_prev + b, b)
        a = jnp.where(keep, a * a_prev, a)
        step *= 2
    groups_per_seg = seg // SUBLANES
    h_blocks, h_last = [], []
    carry = None
    for gi in range(rows // SUBLANES):
        if gi % groups_per_seg == 0:
            carry = h_in[gi // groups_per_seg]
        lo = gi * SUBLANES
        hb = b[lo:lo + SUBLANES] + a[lo:lo + SUBLANES] * carry
        carry = hb[SUBLANES - 1:SUBLANES]
        h_blocks.append(hb)
        if gi % groups_per_seg == groups_per_seg - 1:
            h_last.append(carry)
    h = jnp.concatenate(h_blocks, axis=0)

    y = h * jax.nn.gelu(gate_in, approximate=True)
    out = jnp.dot(y.astype(jnp.bfloat16), wout_ref[...], preferred_element_type=jnp.float32)
    return out, rec, h_last


def _lru_kernel(xp_ref, xs_ref, conv_in_ref, h0_ref, win_ref, cw_ref, cb_ref, wr_ref, wi_ref, br_ref,
                bi_ref, lam_ref, wout_ref, g_ref, b_ref, x1_ref, conv_p_ref, h_p_ref, conv_s_ref, h_s_ref,
                ext_p_scr, h_scr, ext_s_scr):
    c = pl.program_id(0)
    t = pl.program_id(1)
    weights = (win_ref, cw_ref, cb_ref, wr_ref, wi_ref, br_ref, bi_ref, lam_ref, wout_ref)

    @pl.when(c < BATCH)
    def _():
        @pl.when(t == 0)
        def _():
            ext_p_scr[0, 0:SUBLANES, :] = jnp.zeros((SUBLANES, D_RNN), jnp.float32)
            h_scr[...] = jnp.zeros_like(h_scr)

        x = xp_ref[...]
        out, rec, h_last = _lru_tile(x, ext_p_scr, [h_scr[...]], 1, ROW_TILE, *weights)
        ext_p_scr[0, 0:SUBLANES, :] = rec[ROW_TILE - SUBLANES:]
        h_scr[...] = h_last[0]
        conv_p_ref[...] = rec[ROW_TILE - (CONV_WIDTH - 1):]
        h_p_ref[...] = h_last[0]
        _finish_mixer(x, out, g_ref, b_ref, x1_ref)

    @pl.when(c == SAMPLE_CHUNK)
    def _():
        for s in range(SAMPLE_SEGS):
            ext_s_scr[s, 0:SUBLANES, :] = conv_in_ref[s]
        h_in = [h0_ref[s:s + 1, :] for s in range(SAMPLE_SEGS)]
        x = xs_ref[...]
        out, rec, h_last = _lru_tile(x, ext_s_scr, h_in, SAMPLE_SEGS, DEC_SEQ, *weights)
        for s in range(SAMPLE_SEGS):
            end = (s + 1) * DEC_SEQ
            conv_s_ref[s] = rec[end - (CONV_WIDTH - 1):end]
            h_s_ref[s:s + 1, :] = h_last[s]
        _finish_mixer(x, out, g_ref, b_ref, x1_ref)


_TILES_PER_CHUNK = MOE_CHUNK // ROW_TILE


def _prompt_tile(c, t):
    return jnp.where(c < BATCH, t, _TILES_PER_CHUNK - 1)


def _sample_tile(c, t):
    return jnp.where(c == SAMPLE_CHUNK, t, 0)


def _mixer_act_specs(xs_chunk):
    return [
        pl.BlockSpec((None, ROW_TILE, D_MODEL), lambda c, t: (jnp.minimum(c, BATCH - 1), _prompt_tile(c, t), 0)),
        pl.BlockSpec((None, ROW_TILE, D_MODEL), lambda c, t: (xs_chunk, _sample_tile(c, t), 0)),
    ]


_MIXER_GRID = (N_CHUNKS, _TILES_PER_CHUNK)
_PER_PROMPT = lambda c, t: (jnp.minimum(c, BATCH - 1), 0, 0)
_PER_SAMPLE3 = lambda c, t: (_sample_tile(c, t), 0, 0)
_PER_SAMPLE4 = lambda c, t: (_sample_tile(c, t), 0, 0, 0)


def _lru(xp, xs, xs_chunk, conv_pad, h0, w, ln_g, ln_b):
    nt_s = DEC_BATCH // SAMPLE_SEGS
    return pl.pallas_call(
        _lru_kernel,
        grid=_MIXER_GRID,
        in_specs=_mixer_act_specs(xs_chunk) + [
            pl.BlockSpec((None, SAMPLE_SEGS, SUBLANES, D_RNN), _PER_SAMPLE4),
            pl.BlockSpec((None, SAMPLE_SEGS, D_RNN), _PER_SAMPLE3),
            _const_spec((D_MODEL, 2 * D_RNN)),
            _const_spec((CONV_WIDTH, D_RNN)),
            _const_spec((1, D_RNN)),
            _const_spec((D_RNN // MXU_EDGE, MXU_EDGE, MXU_EDGE)),
            _const_spec((D_RNN // MXU_EDGE, MXU_EDGE, MXU_EDGE)),
            _const_spec((1, D_RNN)),
            _const_spec((1, D_RNN)),
            _const_spec((1, D_RNN)),
            _const_spec((D_RNN, D_MODEL)),
            _const_spec((1, D_MODEL)),
            _const_spec((1, D_MODEL)),
        ],
        out_specs=[
            pl.BlockSpec((None, ROW_TILE, D_MODEL), lambda c, t: (c, t, 0)),
            pl.BlockSpec((None, CONV_WIDTH - 1, D_RNN), _PER_PROMPT),
            pl.BlockSpec((None, 1, D_RNN), _PER_PROMPT),
            pl.BlockSpec((None, SAMPLE_SEGS, CONV_WIDTH - 1, D_RNN), _PER_SAMPLE4),
            pl.BlockSpec((None, SAMPLE_SEGS, D_RNN), _PER_SAMPLE3),
        ],
        out_shape=[
            _ACT_SHAPE,
            jax.ShapeDtypeStruct((BATCH, CONV_WIDTH - 1, D_RNN), jnp.float32),
            jax.ShapeDtypeStruct((BATCH, 1, D_RNN), jnp.float32),
            jax.ShapeDtypeStruct((nt_s, SAMPLE_SEGS, CONV_WIDTH - 1, D_RNN), jnp.float32),
            jax.ShapeDtypeStruct((nt_s, SAMPLE_SEGS, D_RNN), jnp.float32),
        ],
        scratch_shapes=[
            pltpu.VMEM((1, SUBLANES + ROW_TILE, D_RNN), jnp.float32),
            pltpu.VMEM((1, D_RNN), jnp.float32),
            pltpu.VMEM((SAMPLE_SEGS, SUBLANES + DEC_SEQ, D_RNN), jnp.float32),
        ],
        compiler_params=pltpu.CompilerParams(
            dimension_semantics=("arbitrary", "arbitrary"), vmem_limit_bytes=VMEM_LIMIT),
        name="lru",
    )(xp, xs, conv_pad, h0, *w, ln_g, ln_b)


def _rope(x, cos, sin_signed):
    lane = lax.broadcasted_iota(jnp.int32, (x.shape[0], LANES), 1)
    first_half = (lane & (HEAD_DIM - 1)) < HEAD_DIM // 2
    cols = []
    for c in range(x.shape[1] // LANES):
        xc = x[:, c * LANES:(c + 1) * LANES]
        partner = jnp.where(first_half, pltpu.roll(xc, LANES - HEAD_DIM // 2, 1),
                            pltpu.roll(xc, HEAD_DIM // 2, 1))
        cols.append(xc * cos + partner * sin_signed)
    return cols[0] if len(cols) == 1 else jnp.concatenate(cols, axis=1)


def _dup_heads(kv):
    lane = lax.broadcasted_iota(jnp.int32, kv.shape, 1)
    low = lane < HEAD_DIM
    swapped = pltpu.roll(kv, HEAD_DIM, 1)
    return (jnp.where(low, kv, swapped).astype(jnp.bfloat16),
            jnp.where(low, swapped, kv).astype(jnp.bfloat16))


def _attend_chunk(q, k_wins, v_wins, valid, sinks_ref):
    lane = lax.broadcasted_iota(jnp.int32, (CHUNK, LANES), 1)
    low = lane < HEAD_DIM
    zero = jnp.zeros((CHUNK, LANES), jnp.bfloat16)
    cols_per_group = KV_GROUP * HEAD_DIM // LANES
    out_cols = []
    for g in range(N_KV_HEADS):
        stacked = []
        for c in range(g * cols_per_group, (g + 1) * cols_per_group):
            qc = q[:, c * LANES:(c + 1) * LANES]
            stacked.append(jnp.where(low, qc, zero))
            stacked.append(jnp.where(low, zero, qc))
        qst = jnp.concatenate(stacked, axis=0)
        s = lax.dot_general(qst, k_wins[g], (((1,), (1,)), ((), ())),
                            preferred_element_type=jnp.float32)
        if valid is not None:
            s = jnp.where(valid, s, NEG_BIG)
        probs = []
        for hh in range(KV_GROUP):
            sh = s[hh * CHUNK:(hh + 1) * CHUNK]
            sink = sinks_ref[g * KV_GROUP + hh]
            m = jnp.maximum(jnp.max(sh, -1, keepdims=True), sink)
            p = jnp.exp(sh - m)
            denom = jnp.sum(p, -1, keepdims=True) + jnp.exp(sink - m)
            probs.append((p * (1.0 / denom)).astype(jnp.bfloat16))
        pst = jnp.concatenate(probs, axis=0)
        o = jnp.dot(pst, v_wins[g], preferred_element_type=jnp.float32)
        for j in range(cols_per_group):
            oa = o[(2 * j) * CHUNK:(2 * j + 1) * CHUNK]
            ob = o[(2 * j + 1) * CHUNK:(2 * j + 2) * CHUNK]
            out_cols.append(jnp.where(low, oa, ob))
    return jnp.concatenate(out_cols, axis=1)


def _project_qkv(x, wqkv_ref, bqkv_ref, cos_ref, sin_ref):
    qkv = jnp.dot(x.astype(jnp.bfloat16), wqkv_ref[...], preferred_element_type=jnp.float32)
    qkv = qkv + bqkv_ref[...]
    nq = N_HEADS * HEAD_DIM
    nk = N_KV_HEADS * HEAD_DIM
    cos = cos_ref[...]
    sin = sin_ref[...]
    q = _rope(qkv[:, :nq], cos, sin)
    k = _rope(qkv[:, nq:nq + nk], cos, sin)
    v = qkv[:, nq + nk:]
    qs = (q * (HEAD_DIM ** -0.5)).astype(jnp.bfloat16)
    return qs, k, v


def _swa_kernel(xp_ref, xs_ref, ck_ref, cv_ref, wqkv_ref, bqkv_ref, cosp_ref, sinp_ref, coss_ref, sins_ref,
                sinks_ref, wo_ref, g_ref, b_ref, x1_ref, kp_ref, vp_ref, ks_ref, vs_ref,
                k0_scr, k1_scr, v0_scr, v1_scr):
    c = pl.program_id(0)
    t = pl.program_id(1)
    scrs = (k0_scr, k1_scr, v0_scr, v1_scr)

    @pl.when(c < BATCH)
    def _():
        @pl.when(t == 0)
        def _():
            for scr in scrs:
                scr[0:WINDOW, :] = jnp.zeros((WINDOW, LANES), jnp.bfloat16)

        x = xp_ref[...]
        qs, k, v = _project_qkv(x, wqkv_ref, bqkv_ref, cosp_ref, sinp_ref)
        kp_ref[...] = k[ROW_TILE - WINDOW:]
        vp_ref[...] = v[ROW_TILE - WINDOW:]
        k0, k1 = _dup_heads(k)
        v0, v1 = _dup_heads(v)
        for scr, val in zip(scrs, (k0, k1, v0, v1)):
            scr[WINDOW:WINDOW + ROW_TILE, :] = val

        span = WINDOW + CHUNK
        key_pos = lax.broadcasted_iota(jnp.int32, (1, span), 1)
        chunks = []
        for ci in range(ROW_TILE // CHUNK):
            lo = ci * CHUNK
            valid = (t * ROW_TILE + lo - WINDOW + key_pos) >= 0
            k_wins = (k0_scr[lo:lo + span, :], k1_scr[lo:lo + span, :])
            v_wins = (v0_scr[lo:lo + span, :], v1_scr[lo:lo + span, :])
            chunks.append(_attend_chunk(qs[lo:lo + CHUNK], k_wins, v_wins, valid, sinks_ref))
        o = jnp.concatenate(chunks, axis=0)
        for scr in scrs:
            scr[0:WINDOW, :] = scr[ROW_TILE:ROW_TILE + WINDOW, :]
        out = jnp.dot(o.astype(jnp.bfloat16), wo_ref[...], preferred_element_type=jnp.float32)
        _finish_mixer(x, out, g_ref, b_ref, x1_ref)

    @pl.when(c == SAMPLE_CHUNK)
    def _():
        x = xs_ref[...]
        qs, k, v = _project_qkv(x, wqkv_ref, bqkv_ref, coss_ref, sins_ref)
        knew = _dup_heads(k)
        vnew = _dup_heads(v)
        chunks = []
        for s in range(SAMPLE_SEGS):
            lo = s * DEC_SEQ
            ck = ck_ref[s]
            cv = cv_ref[s]
            kold = _dup_heads(ck)
            vold = _dup_heads(cv)
            k_wins = [jnp.concatenate([kold[g], knew[g][lo:lo + DEC_SEQ]], axis=0) for g in range(N_KV_HEADS)]
            v_wins = [jnp.concatenate([vold[g], vnew[g][lo:lo + DEC_SEQ]], axis=0) for g in range(N_KV_HEADS)]
            chunks.append(_attend_chunk(qs[lo:lo + DEC_SEQ], k_wins, v_wins, None, sinks_ref))
            ks_ref[s, 0:WINDOW - DEC_SEQ, :] = ck[DEC_SEQ:]
            ks_ref[s, WINDOW - DEC_SEQ:WINDOW, :] = k[lo:lo + DEC_SEQ]
            vs_ref[s, 0:WINDOW - DEC_SEQ, :] = cv[DEC_SEQ:]
            vs_ref[s, WINDOW - DEC_SEQ:WINDOW, :] = v[lo:lo + DEC_SEQ]
        o = jnp.concatenate(chunks, axis=0)
        out = jnp.dot(o.astype(jnp.bfloat16), wo_ref[...], preferred_element_type=jnp.float32)
        _finish_mixer(x, out, g_ref, b_ref, x1_ref)


def _swa(xp, xs, xs_chunk, ck, cv, w, cos_p, sin_p, cos_s, sin_s, ln_g, ln_b):
    wqkv, bqkv, sinks, wo = w
    nt_s = DEC_BATCH // SAMPLE_SEGS
    kv_lanes = N_KV_HEADS * HEAD_DIM
    cache_spec = pl.BlockSpec((None, SAMPLE_SEGS, WINDOW, kv_lanes), _PER_SAMPLE4)
    prompt_table = pl.BlockSpec((ROW_TILE, LANES), lambda c, t: (_prompt_tile(c, t), 0))
    kv_prompt = pl.BlockSpec((None, WINDOW, kv_lanes), _PER_PROMPT)
    return pl.pallas_call(
        _swa_kernel,
        grid=_MIXER_GRID,
        in_specs=_mixer_act_specs(xs_chunk) + [
            cache_spec, cache_spec,
            _const_spec((D_MODEL, QKV_DIM)),
            _const_spec((1, QKV_DIM)),
            prompt_table, prompt_table,
            _const_spec((ROW_TILE, LANES)), _const_spec((ROW_TILE, LANES)),
            pl.BlockSpec(memory_space=pltpu.SMEM),
            _const_spec((N_HEADS * HEAD_DIM, D_MODEL)),
            _const_spec((1, D_MODEL)),
            _const_spec((1, D_MODEL)),
        ],
        out_specs=[
            pl.BlockSpec((None, ROW_TILE, D_MODEL), lambda c, t: (c, t, 0)),
            kv_prompt, kv_prompt, cache_spec, cache_spec,
        ],
        out_shape=[
            _ACT_SHAPE,
            jax.ShapeDtypeStruct((BATCH, WINDOW, kv_lanes), jnp.float32),
            jax.ShapeDtypeStruct((BATCH, WINDOW, kv_lanes), jnp.float32),
            jax.ShapeDtypeStruct((nt_s, SAMPLE_SEGS, WINDOW, kv_lanes), jnp.float32),
            jax.ShapeDtypeStruct((nt_s, SAMPLE_SEGS, WINDOW, kv_lanes), jnp.float32),
        ],
        scratch_shapes=[pltpu.VMEM((WINDOW + ROW_TILE, LANES), jnp.bfloat16) for _ in range(4)],
        compiler_params=pltpu.CompilerParams(
            dimension_semantics=("arbitrary", "arbitrary"), vmem_limit_bytes=VMEM_LIMIT),
        name="swa",
    )(xp, xs, ck, cv, wqkv, bqkv, cos_p, sin_p, cos_s, sin_s, sinks, wo, ln_g, ln_b)


META_TILE_EXPERT, META_N_ACTIVE, META_FILL_LO, META_FILL_HI = 0, 1, 2, 3


def _router_kernel(x_ref, wr_ref, br_ref, pos_ref, gate_ref, meta_ref):
    f32 = jnp.float32
    logits = lax.dot_general(wr_ref[...].astype(jnp.bfloat16), x_ref[...].astype(jnp.bfloat16),
                             (((1,), (1,)), ((), ())), preferred_element_type=f32)
    m = jnp.max(logits, axis=0, keepdims=True)
    e = jnp.exp(logits - m)
    probs = e / jnp.sum(e, axis=0, keepdims=True)
    sel = probs + br_ref[...]
    p = [probs[i:i + 1] for i in range(N_EXPERTS)]
    s = [sel[i:i + 1] for i in range(N_EXPERTS)]

    def first_argmax(vals):
        best, arg = vals[0], jnp.zeros_like(vals[0], dtype=jnp.int32)
        for i in range(1, len(vals)):
            better = vals[i] > best
            best = jnp.where(better, vals[i], best)
            arg = jnp.where(better, i, arg)
        return arg

    group_scores = []
    for gidx in range(N_GROUPS):
        v = s[gidx * EXPERTS_PER_GROUP:(gidx + 1) * EXPERTS_PER_GROUP]
        best = v[0] + v[1]
        for i in range(EXPERTS_PER_GROUP):
            for j in range(i + 1, EXPERTS_PER_GROUP):
                if (i, j) != (0, 1):
                    best = jnp.maximum(best, v[i] + v[j])
        group_scores.append(best)
    best_group = first_argmax(group_scores)
    neg_inf = jnp.full_like(s[0], -jnp.inf)
    masked = [jnp.where(best_group == (i // EXPERTS_PER_GROUP), s[i], neg_inf) for i in range(N_EXPERTS)]
    idx1 = first_argmax(masked)
    masked2 = [jnp.where(idx1 == i, neg_inf, masked[i]) for i in range(N_EXPERTS)]
    idx2 = first_argmax(masked2)
    zero = jnp.zeros_like(p[0])
    w1, w2 = zero, zero
    for i in range(N_EXPERTS):
        w1 = w1 + jnp.where(idx1 == i, p[i], zero)
        w2 = w2 + jnp.where(idx2 == i, p[i], zero)
    tot = w1 + w2
    gate_ref[0:1, :] = w1 / tot
    gate_ref[1:2, :] = w2 / tot

    expert = lax.broadcasted_iota(jnp.int32, (N_EXPERTS, MOE_CHUNK), 0)
    strict_upper = (lax.broadcasted_iota(jnp.int32, (LANES, LANES), 0)
                    < lax.broadcasted_iota(jnp.int32, (LANES, LANES), 1)).astype(jnp.bfloat16)
    carry = jnp.zeros((N_EXPERTS, 1), f32)
    onehots, ranks = [], []
    for idx in (idx1, idx2):
        onehot = (expert == idx).astype(f32)
        before = []
        for blk in range(MOE_CHUNK // LANES):
            oh = onehot[:, blk * LANES:(blk + 1) * LANES]
            before.append(jnp.dot(oh.astype(jnp.bfloat16), strict_upper, preferred_element_type=f32) + carry)
            carry = carry + jnp.sum(oh, axis=1, keepdims=True)
        onehots.append(onehot)
        ranks.append(jnp.sum(onehot * jnp.concatenate(before, axis=1), axis=0, keepdims=True))
    counts = carry
    padded = jnp.floor((counts + (MOE_TM - 1)) * (1.0 / MOE_TM)) * MOE_TM
    strict_lower = (lax.broadcasted_iota(jnp.int32, (N_EXPERTS, N_EXPERTS), 0)
                    > lax.broadcasted_iota(jnp.int32, (N_EXPERTS, N_EXPERTS), 1)).astype(jnp.bfloat16)
    starts = jnp.dot(strict_lower, jnp.broadcast_to(padded, (N_EXPERTS, LANES)).astype(jnp.bfloat16),
                     preferred_element_type=f32)[:, 0:1]
    ends = starts + padded
    for k in range(2):
        pos = jnp.sum(onehots[k] * starts, axis=0, keepdims=True) + ranks[k]
        pos_ref[k:k + 1, :] = pos.astype(jnp.int32)

    lane = lax.broadcasted_iota(jnp.int32, (N_EXPERTS, LANES), 1)
    sub = lax.broadcasted_iota(jnp.int32, (N_EXPERTS, LANES), 0)
    tile_start = (lane * MOE_TM).astype(f32)
    tile_expert = jnp.sum((ends <= tile_start).astype(f32), axis=0, keepdims=True)
    tile_expert = jnp.minimum(tile_expert, N_EXPERTS - 1.0)
    on_diag = sub == lane

    def to_lanes(col):
        return jnp.sum(jnp.where(on_diag, col, 0.0), axis=0, keepdims=True)

    n_active = jnp.broadcast_to(ends[N_EXPERTS - 1:N_EXPERTS] * (1.0 / MOE_TM), (1, LANES))
    meta_ref[...] = jnp.zeros(meta_ref.shape, jnp.int32)
    meta_ref[META_TILE_EXPERT:META_TILE_EXPERT + 1, :] = tile_expert.astype(jnp.int32)
    meta_ref[META_N_ACTIVE:META_N_ACTIVE + 1, :] = n_active.astype(jnp.int32)
    meta_ref[META_FILL_LO:META_FILL_LO + 1, :] = to_lanes(starts + counts).astype(jnp.int32)
    meta_ref[META_FILL_HI:META_FILL_HI + 1, :] = to_lanes(ends).astype(jnp.int32)


def _router(x, wr_t, br):
    per_chunk = lambda c: (c, 0, 0)
    return pl.pallas_call(
        _router_kernel,
        grid=(N_CHUNKS,),
        in_specs=[
            pl.BlockSpec((None, MOE_CHUNK, D_MODEL), per_chunk),
            _const_spec((N_EXPERTS, D_MODEL)),
            _const_spec((N_EXPERTS, 1)),
        ],
        out_specs=[
            pl.BlockSpec((None, 2, MOE_CHUNK), per_chunk),
            pl.BlockSpec((None, 2, MOE_CHUNK), per_chunk),
            pl.BlockSpec((None, SUBLANES, LANES), per_chunk),
        ],
        out_shape=[
            jax.ShapeDtypeStruct((N_CHUNKS, 2, MOE_CHUNK), jnp.int32),
            jax.ShapeDtypeStruct((N_CHUNKS, 2, MOE_CHUNK), jnp.float32),
            jax.ShapeDtypeStruct((N_CHUNKS, SUBLANES, LANES), jnp.int32),
        ],
        compiler_params=pltpu.CompilerParams(
            dimension_semantics=("parallel",), vmem_limit_bytes=VMEM_LIMIT),
        name="router",
    )(x, wr_t, br)


def _moe_kernel(te_ref, nact_ref, flo_ref, fhi_ref, pos_ref, x_ref, gate_ref, wg_ref, wu_ref, wd_ref,
                g_ref, b_ref, out_ref, pair_scr, xs_scr, yo_scr, y_scr):
    c = pl.program_id(0)
    i = pl.program_id(1)

    @pl.when(i == 0)
    def _():
        def fill_expert(e, carry):
            def fill(r, inner):
                pair_scr[r] = JUNK_PAIR
                return inner
            lax.fori_loop(flo_ref[c * N_EXPERTS + e], fhi_ref[c * N_EXPERTS + e], fill, 0)
            return carry
        lax.fori_loop(0, N_EXPERTS, fill_expert, 0)

        def invert(t, carry):
            pair_scr[pos_ref[0, t]] = t
            pair_scr[pos_ref[1, t]] = MOE_CHUNK + t
            return carry
        lax.fori_loop(0, MOE_CHUNK, invert, 0, unroll=8)

    @pl.when(jnp.logical_and(i < MOE_TILES, i < nact_ref[c]))
    def _():
        row0 = i * MOE_TM

        def gather(gi, carry):
            for u in range(SUBLANES):
                tok = pair_scr[row0 + gi * SUBLANES + u] & (MOE_CHUNK - 1)
                xs_scr[gi, u:u + 1, :] = x_ref[pl.ds(tok, 1), :]
            return carry
        lax.fori_loop(0, MOE_TM // SUBLANES, gather, 0)

        xb = xs_scr[...].reshape(MOE_TM, D_MODEL).astype(jnp.bfloat16)
        hg = jnp.dot(xb, wg_ref[...], preferred_element_type=jnp.float32)
        hu = jnp.dot(xb, wu_ref[...], preferred_element_type=jnp.float32)
        act = (jax.nn.silu(hg) * hu).astype(jnp.bfloat16)
        yo = jnp.dot(act, wd_ref[...], preferred_element_type=jnp.float32)
        yo_scr[...] = yo.reshape(MOE_TM // SUBLANES, SUBLANES, D_MODEL)

        def scatter(gi, carry):
            for u in range(SUBLANES):
                pair = pair_scr[row0 + gi * SUBLANES + u]
                y_scr[pl.ds(pair, 1), :] = yo_scr[gi, u:u + 1, :]
            return carry
        lax.fori_loop(0, MOE_TM // SUBLANES, scatter, 0)

    @pl.when(i >= MOE_TILES)
    def _():
        lo = pl.multiple_of((i - MOE_TILES) * ROW_TILE, ROW_TILE)
        gates = gate_ref[...]
        moe = (gates[:, 0:1] * y_scr[pl.ds(lo, ROW_TILE), :]
               + gates[:, 1:2] * y_scr[pl.ds(MOE_CHUNK + lo, ROW_TILE), :])
        x = x_ref[pl.ds(lo, ROW_TILE), :]
        out_ref[...] = _layer_norm(DEEPNORM_ALPHA * x + moe, g_ref[...], b_ref[...])


def _moe(x, plan, gates_t, layer, wg, wu, wd, ln_g, ln_b):
    tile_expert, n_active, fill_lo, fill_hi, pos = plan

    def tile_of(c, i):
        return c * MOE_TILES + jnp.minimum(i, MOE_TILES - 1)

    def sub_of(i):
        return jnp.maximum(i - MOE_TILES, 0)

    def expert_of(c, i, te):
        return te[tile_of(c, i)]

    grid_spec = pltpu.PrefetchScalarGridSpec(
        num_scalar_prefetch=4,
        grid=(N_CHUNKS, MOE_TILES + COMBINE_STEPS),
        in_specs=[
            pl.BlockSpec((None, 2, MOE_CHUNK), lambda c, i, *_: (c, 0, 0), memory_space=pltpu.SMEM),
            pl.BlockSpec((None, MOE_CHUNK, D_MODEL), lambda c, i, *_: (c, 0, 0)),
            pl.BlockSpec((None, ROW_TILE, 2), lambda c, i, *_: (c, sub_of(i), 0)),
            pl.BlockSpec((None, None, D_MODEL, D_EXPERT),
                         lambda c, i, te, *_: (layer, expert_of(c, i, te), 0, 0)),
            pl.BlockSpec((None, None, D_MODEL, D_EXPERT),
                         lambda c, i, te, *_: (layer, expert_of(c, i, te), 0, 0)),
            pl.BlockSpec((None, None, D_EXPERT, D_MODEL),
                         lambda c, i, te, *_: (layer, expert_of(c, i, te), 0, 0)),
            pl.BlockSpec((1, D_MODEL), lambda c, i, *_: (0, 0)),
            pl.BlockSpec((1, D_MODEL), lambda c, i, *_: (0, 0)),
        ],
        out_specs=pl.BlockSpec((None, ROW_TILE, D_MODEL), lambda c, i, *_: (c, sub_of(i), 0)),
        scratch_shapes=[
            pltpu.SMEM((MOE_ROWS,), jnp.int32),
            pltpu.VMEM((MOE_TM // SUBLANES, SUBLANES, D_MODEL), jnp.float32),
            pltpu.VMEM((MOE_TM // SUBLANES, SUBLANES, D_MODEL), jnp.float32),
            pltpu.VMEM((2 * MOE_CHUNK + SUBLANES, D_MODEL), jnp.float32),
        ],
    )
    return pl.pallas_call(
        _moe_kernel,
        grid_spec=grid_spec,
        out_shape=_ACT_SHAPE,
        compiler_params=pltpu.CompilerParams(
            dimension_semantics=("arbitrary", "arbitrary"), vmem_limit_bytes=VMEM_LIMIT),
        name="moe",
    )(tile_expert, n_active, fill_lo, fill_hi, pos,
      x, gates_t, wg, wu, wd, ln_g, ln_b)


def _moe_layer(x, wr_t, br, layer, wg, wu, wd, ln_g, ln_b):
    pos, gates, meta = _router(x, wr_t, br)
    plan = (meta[:, META_TILE_EXPERT, :MOE_TILES].reshape(-1), meta[:, META_N_ACTIVE, 0],
            meta[:, META_FILL_LO, :N_EXPERTS].reshape(-1), meta[:, META_FILL_HI, :N_EXPERTS].reshape(-1), pos)
    return _moe(x, plan, jnp.swapaxes(gates, 1, 2), layer, wg, wu, wd, ln_g, ln_b)


def _rope_tables(pos):
    half = HEAD_DIM // 2
    inv_freq = ROPE_THETA ** (-jnp.arange(half, dtype=jnp.float32) / half)
    ang = pos.astype(jnp.float32)[:, None] * inv_freq[None, :]
    cos = jnp.cos(ang)
    sin = jnp.sin(ang)
    reps = LANES // HEAD_DIM
    cos_t = jnp.tile(jnp.concatenate([cos, cos], axis=1), (1, reps))
    sin_t = jnp.tile(jnp.concatenate([-sin, sin], axis=1), (1, reps))
    return cos_t, sin_t


def _block_diag_tiles(w):
    per = MXU_EDGE // LRU_BLOCK_W
    w4 = w.reshape(LRU_BLOCKS // per, per, LRU_BLOCK_W, LRU_BLOCK_W)
    eye = jnp.eye(per, dtype=w.dtype)
    return jnp.einsum("qaij,ab->qaibj", w4, eye).reshape(LRU_BLOCKS // per, MXU_EDGE, MXU_EDGE)


def kernel(x_prompt, x_sample, state_conv_0, state_h_0, cache_k_1, cache_v_1, state_conv_2, state_h_2,
           cache_k_3, cache_v_3, w_router, b_router, ln_g, ln_b, lru_w_in, lru_conv_w, lru_conv_b,
           lru_w_gates, lru_b_gates, lru_lambda, lru_w_out, attn_w_qkv, attn_b_qkv, attn_sinks, attn_w_o,
           moe_w_gate, moe_w_up, moe_w_down):
    bf = jnp.bfloat16
    in_state = {0: (state_conv_0, state_h_0), 1: (cache_k_1, cache_v_1),
                2: (state_conv_2, state_h_2), 3: (cache_k_3, cache_v_3)}
    nt_s = DEC_BATCH // SAMPLE_SEGS
    kv_lanes = N_KV_HEADS * HEAD_DIM

    wr_t = w_router.T
    br = b_router.reshape(N_EXPERTS, 1)
    cos_p, sin_p = _rope_tables(jnp.arange(SEQ))
    cos_s, sin_s = _rope_tables(PAST_LEN + jnp.arange(DEC_SEQ))
    cos_s = jnp.tile(cos_s, (SAMPLE_SEGS, 1))
    sin_s = jnp.tile(sin_s, (SAMPLE_SEGS, 1))
    wg_all = moe_w_gate.astype(bf)
    wu_all = moe_w_up.astype(bf)
    wd_all = moe_w_down.astype(bf)

    x = x_prompt
    x_s, x_s_chunk = x_sample.reshape(1, MOE_CHUNK, D_MODEL), 0
    new_p, new_s = {}, {}
    for i in range(DEPTH):
        slot = i // 2
        g0 = ln_g[i, 0].reshape(1, D_MODEL)
        b0 = ln_b[i, 0].reshape(1, D_MODEL)
        g1 = ln_g[i, 1].reshape(1, D_MODEL)
        b1 = ln_b[i, 1].reshape(1, D_MODEL)
        st_a, st_b = in_state[i]
        if i % 2 == 0:
            gates = lru_w_gates[slot]
            w = (lru_w_in[slot].astype(bf), lru_conv_w[slot], lru_conv_b[slot].reshape(1, D_RNN),
                 _block_diag_tiles(gates[:, :, :LRU_BLOCK_W]).astype(bf),
                 _block_diag_tiles(gates[:, :, LRU_BLOCK_W:]).astype(bf),
                 lru_b_gates[slot][:, :LRU_BLOCK_W].reshape(1, D_RNN),
                 lru_b_gates[slot][:, LRU_BLOCK_W:].reshape(1, D_RNN),
                 lru_lambda[slot].reshape(1, D_RNN), lru_w_out[slot].astype(bf))
            conv_pad = jnp.pad(st_a, ((0, 0), (SUBLANES - (CONV_WIDTH - 1), 0), (0, 0)))
            x1, conv_p, h_p, conv_s, h_s = _lru(
                x, x_s, x_s_chunk, conv_pad.reshape(nt_s, SAMPLE_SEGS, SUBLANES, D_RNN),
                st_b.reshape(nt_s, SAMPLE_SEGS, D_RNN), w, g0, b0)
            new_p[i] = (conv_p, h_p.reshape(BATCH, D_RNN))
            new_s[i] = (conv_s.reshape(DEC_BATCH, CONV_WIDTH - 1, D_RNN), h_s.reshape(DEC_BATCH, D_RNN))
        else:
            w = (attn_w_qkv[slot].astype(bf), attn_b_qkv[slot].reshape(1, QKV_DIM), attn_sinks[slot],
                 attn_w_o[slot].astype(bf))
            x1, k_p, v_p, k_s, v_s = _swa(
                x, x_s, x_s_chunk, st_a.reshape(nt_s, SAMPLE_SEGS, WINDOW, kv_lanes),
                st_b.reshape(nt_s, SAMPLE_SEGS, WINDOW, kv_lanes), w, cos_p, sin_p, cos_s, sin_s, g0, b0)
            new_p[i] = (k_p.reshape(BATCH, WINDOW, N_KV_HEADS, HEAD_DIM),
                        v_p.reshape(BATCH, WINDOW, N_KV_HEADS, HEAD_DIM))
            new_s[i] = (k_s.reshape(DEC_BATCH, WINDOW, N_KV_HEADS, HEAD_DIM),
                        v_s.reshape(DEC_BATCH, WINDOW, N_KV_HEADS, HEAD_DIM))
        x = _moe_layer(x1, wr_t, br, i, wg_all, wu_all, wd_all, g1, b1)
        x_s, x_s_chunk = x, SAMPLE_CHUNK
    return (x[:BATCH], x[SAMPLE_CHUNK].reshape(DEC_BATCH, DEC_SEQ, D_MODEL),
            new_p[0][0], new_s[0][0], new_p[0][1], new_s[0][1],
            new_p[1][0], new_s[1][0], new_p[1][1], new_s[1][1],
            new_p[2][0], new_s[2][0], new_p[2][1], new_s[2][1],
            new_p[3][0], new_s[3][0], new_p[3][1], new_s[3][1])
```

```python
import jax
import jax.numpy as jnp
from jax import lax
from jax.experimental import pallas as pl
from jax.experimental.pallas import tpu as pltpu

D_MODEL = 1024
BATCH = 8
SEQ = 2048
DEPTH = 4
DEC_BATCH = 32
DEC_SEQ = 64
PAST_LEN = 4096
CHUNK = 64
D_RNN = D_MODEL
CONV_WIDTH = 4
LRU_BLOCKS = 16
LRU_BLOCK_W = D_RNN // LRU_BLOCKS
LRU_C = 8.0
N_HEADS = 16
N_KV_HEADS = 2
HEAD_DIM = 64
KV_GROUP = N_HEADS // N_KV_HEADS
WINDOW = 128
ROPE_THETA = 10000.0
QKV_DIM = (N_HEADS + 2 * N_KV_HEADS) * HEAD_DIM
N_EXPERTS = 16
N_GROUPS = 4
EXPERTS_PER_GROUP = N_EXPERTS // N_GROUPS
D_EXPERT = 512
DEEPNORM_ALPHA = (2.0 * DEPTH) ** 0.25
LN_EPS = 1e-5

MXU_EDGE = 256
LANES = 128
SUBLANES = 8

ROW_TILE = 256
MOE_CHUNK = SEQ
N_CHUNKS = BATCH + DEC_BATCH * DEC_SEQ // MOE_CHUNK
SAMPLE_CHUNK = BATCH
MOE_TM = 256
MOE_ROWS = 2 * MOE_CHUNK + N_EXPERTS * MOE_TM
MOE_TILES = MOE_ROWS // MOE_TM
MOE_STEPS = MOE_TILES + 1
COMBINE_STEPS = MOE_CHUNK // ROW_TILE
JUNK_PAIR = 2 * MOE_CHUNK
SAMPLE_SEGS = ROW_TILE // DEC_SEQ
NEG_BIG = -1e30
VMEM_LIMIT = 52 * 1024 * 1024

assert DEC_BATCH * DEC_SEQ == MOE_CHUNK and MOE_TILES <= LANES


def _layer_norm(x, g, b):
    mu = jnp.mean(x, -1, keepdims=True)
    xc = x - mu
    var = jnp.mean(xc * xc, -1, keepdims=True)
    return xc * lax.rsqrt(var + LN_EPS) * g + b


def _finish_mixer(x, y, g_ref, b_ref, x1_ref):
    x1_ref[...] = _layer_norm(DEEPNORM_ALPHA * x + y, g_ref[...], b_ref[...])


def _const_spec(shape):
    nd = len(shape)
    return pl.BlockSpec(shape, lambda *_: (0,) * nd)


_ACT_SHAPE = jax.ShapeDtypeStruct((N_CHUNKS, MOE_CHUNK, D_MODEL), jnp.float32)


def _lru_tile(x, ext_scr, h_in, n_seg, seg, win_ref, cw_ref, cb_ref, wr_ref, wi_ref,
              br_ref, bi_ref, lam_ref, wout_ref):
    rows = n_seg * seg
    gr = jnp.dot(x.astype(jnp.bfloat16), win_ref[...], preferred_element_type=jnp.float32)
    gate_in = gr[:, :D_RNN]
    rec = gr[:, D_RNN:]
    cw = cw_ref[...]
    xcs = []
    for s in range(n_seg):
        ext_scr[s, SUBLANES:SUBLANES + seg, :] = rec[s * seg:(s + 1) * seg]
        xc = cb_ref[...] + cw[3:4] * rec[s * seg:(s + 1) * seg]
        for k in range(CONV_WIDTH - 1):
            off = SUBLANES - (CONV_WIDTH - 1) + k
            xc = xc + cw[k:k + 1] * ext_scr[s, off:off + seg, :]
        xcs.append(xc)
    xc = xcs[0] if n_seg == 1 else jnp.concatenate(xcs, axis=0)

    xcb = xc.astype(jnp.bfloat16)
    r_parts, i_parts = [], []
    for q in range(D_RNN // MXU_EDGE):
        blk = xcb[:, q * MXU_EDGE:(q + 1) * MXU_EDGE]
        r_parts.append(jnp.dot(blk, wr_ref[q], preferred_element_type=jnp.float32))
        i_parts.append(jnp.dot(blk, wi_ref[q], preferred_element_type=jnp.float32))
    r = jax.nn.sigmoid(jnp.concatenate(r_parts, axis=1) + br_ref[...])
    ig = jax.nn.sigmoid(jnp.concatenate(i_parts, axis=1) + bi_ref[...])

    z = -lam_ref[...]
    softplus = jnp.maximum(z, 0.0) + jnp.log1p(jnp.exp(-jnp.abs(z)))
    log_a = (-LRU_C * r) * softplus
    a = jnp.exp(log_a)
    b = jnp.sqrt(-jnp.tanh(log_a) * (a * a + 1.0)) * (ig * xc)

    sub = lax.broadcasted_iota(jnp.int32, (rows, D_RNN), 0) & (SUBLANES - 1)
    step = 1
    while step < SUBLANES:
        keep = sub >= step
        a_prev = pltpu.roll(a, step, 0)
        b_prev = pltpu.roll(b, step, 0)
        b = jnp.where(keep, a * b_prev + b, b)
        a = jnp.where(keep, a * a_prev, a)
        step *= 2
    groups_per_seg = seg // SUBLANES
    h_blocks, h_last = [], []
    carry = None
    for gi in range(rows // SUBLANES):
        if gi % groups_per_seg == 0:
            carry = h_in[gi // groups_per_seg]
        lo = gi * SUBLANES
        hb = b[lo:lo + SUBLANES] + a[lo:lo + SUBLANES] * carry
        carry = hb[SUBLANES - 1:SUBLANES]
        h_blocks.append(hb)
        if gi % groups_per_seg == groups_per_seg - 1:
            h_last.append(carry)
    h = jnp.concatenate(h_blocks, axis=0)

    y = h * jax.nn.gelu(gate_in, approximate=True)
    out = jnp.dot(y.astype(jnp.bfloat16), wout_ref[...], preferred_element_type=jnp.float32)
    return out, rec, h_last


def _lru_kernel(xp_ref, xs_ref, conv_in_ref, h0_ref, win_ref, cw_ref, cb_ref, wr_ref, wi_ref, br_ref,
                bi_ref, lam_ref, wout_ref, g_ref, b_ref, x1_ref, conv_p_ref, h_p_ref, conv_s_ref, h_s_ref,
                ext_p_scr, h_scr, ext_s_scr):
    c = pl.program_id(0)
    t = pl.program_id(1)
    weights = (win_ref, cw_ref, cb_ref, wr_ref, wi_ref, br_ref, bi_ref, lam_ref, wout_ref)

    @pl.when(c < BATCH)
    def _():
        @pl.when(t == 0)
        def _():
            ext_p_scr[0, 0:SUBLANES, :] = jnp.zeros((SUBLANES, D_RNN), jnp.float32)
            h_scr[...] = jnp.zeros_like(h_scr)

        x = xp_ref[...]
        out, rec, h_last = _lru_tile(x, ext_p_scr, [h_scr[...]], 1, ROW_TILE, *weights)
        ext_p_scr[0, 0:SUBLANES, :] = rec[ROW_TILE - SUBLANES:]
        h_scr[...] = h_last[0]
        conv_p_ref[...] = rec[ROW_TILE - (CONV_WIDTH - 1):]
        h_p_ref[...] = h_last[0]
        _finish_mixer(x, out, g_ref, b_ref, x1_ref)

    @pl.when(c == SAMPLE_CHUNK)
    def _():
        for s in range(SAMPLE_SEGS):
            ext_s_scr[s, 0:SUBLANES, :] = conv_in_ref[s]
        h_in = [h0_ref[s:s + 1, :] for s in range(SAMPLE_SEGS)]
        x = xs_ref[...]
        out, rec, h_last = _lru_tile(x, ext_s_scr, h_in, SAMPLE_SEGS, DEC_SEQ, *weights)
        for s in range(SAMPLE_SEGS):
            end = (s + 1) * DEC_SEQ
            conv_s_ref[s] = rec[end - (CONV_WIDTH - 1):end]
            h_s_ref[s:s + 1, :] = h_last[s]
        _finish_mixer(x, out, g_ref, b_ref, x1_ref)


_TILES_PER_CHUNK = MOE_CHUNK // ROW_TILE


def _prompt_tile(c, t):
    return jnp.where(c < BATCH, t, _TILES_PER_CHUNK - 1)


def _sample_tile(c, t):
    return jnp.where(c == SAMPLE_CHUNK, t, 0)


def _mixer_act_specs(xs_chunk):
    return [
        pl.BlockSpec((None, ROW_TILE, D_MODEL), lambda c, t: (jnp.minimum(c, BATCH - 1), _prompt_tile(c, t), 0)),
        pl.BlockSpec((None, ROW_TILE, D_MODEL), lambda c, t: (xs_chunk, _sample_tile(c, t), 0)),
    ]


_MIXER_GRID = (N_CHUNKS, _TILES_PER_CHUNK)
_PER_PROMPT = lambda c, t: (jnp.minimum(c, BATCH - 1), 0, 0)
_PER_SAMPLE3 = lambda c, t: (_sample_tile(c, t), 0, 0)
_PER_SAMPLE4 = lambda c, t: (_sample_tile(c, t), 0, 0, 0)


def _lru(xp, xs, xs_chunk, conv_pad, h0, w, ln_g, ln_b):
    nt_s = DEC_BATCH // SAMPLE_SEGS
    return pl.pallas_call(
        _lru_kernel,
        grid=_MIXER_GRID,
        in_specs=_mixer_act_specs(xs_chunk) + [
            pl.BlockSpec((None, SAMPLE_SEGS, SUBLANES, D_RNN), _PER_SAMPLE4),
            pl.BlockSpec((None, SAMPLE_SEGS, D_RNN), _PER_SAMPLE3),
            _const_spec((D_MODEL, 2 * D_RNN)),
            _const_spec((CONV_WIDTH, D_RNN)),
            _const_spec((1, D_RNN)),
            _const_spec((D_RNN // MXU_EDGE, MXU_EDGE, MXU_EDGE)),
            _const_spec((D_RNN // MXU_EDGE, MXU_EDGE, MXU_EDGE)),
            _const_spec((1, D_RNN)),
            _const_spec((1, D_RNN)),
            _const_spec((1, D_RNN)),
            _const_spec((D_RNN, D_MODEL)),
            _const_spec((1, D_MODEL)),
            _const_spec((1, D_MODEL)),
        ],
        out_specs=[
            pl.BlockSpec((None, ROW_TILE, D_MODEL), lambda c, t: (c, t, 0)),
            pl.BlockSpec((None, CONV_WIDTH - 1, D_RNN), _PER_PROMPT),
            pl.BlockSpec((None, 1, D_RNN), _PER_PROMPT),
            pl.BlockSpec((None, SAMPLE_SEGS, CONV_WIDTH - 1, D_RNN), _PER_SAMPLE4),
            pl.BlockSpec((None, SAMPLE_SEGS, D_RNN), _PER_SAMPLE3),
        ],
        out_shape=[
            _ACT_SHAPE,
            jax.ShapeDtypeStruct((BATCH, CONV_WIDTH - 1, D_RNN), jnp.float32),
            jax.ShapeDtypeStruct((BATCH, 1, D_RNN), jnp.float32),
            jax.ShapeDtypeStruct((nt_s, SAMPLE_SEGS, CONV_WIDTH - 1, D_RNN), jnp.float32),
            jax.ShapeDtypeStruct((nt_s, SAMPLE_SEGS, D_RNN), jnp.float32),
        ],
        scratch_shapes=[
            pltpu.VMEM((1, SUBLANES + ROW_TILE, D_RNN), jnp.float32),
            pltpu.VMEM((1, D_RNN), jnp.float32),
            pltpu.VMEM((SAMPLE_SEGS, SUBLANES + DEC_SEQ, D_RNN), jnp.float32),
        ],
        compiler_params=pltpu.CompilerParams(
            dimension_semantics=("arbitrary", "arbitrary"), vmem_limit_bytes=VMEM_LIMIT),
        name="lru",
    )(xp, xs, conv_pad, h0, *w, ln_g, ln_b)


def _rope(x, cos, sin_signed):
    lane = lax.broadcasted_iota(jnp.int32, (x.shape[0], LANES), 1)
    first_half = (lane & (HEAD_DIM - 1)) < HEAD_DIM // 2
    cols = []
    for c in range(x.shape[1] // LANES):
        xc = x[:, c * LANES:(c + 1) * LANES]
        partner = jnp.where(first_half, pltpu.roll(xc, LANES - HEAD_DIM // 2, 1),
                            pltpu.roll(xc, HEAD_DIM // 2, 1))
        cols.append(xc * cos + partner * sin_signed)
    return cols[0] if len(cols) == 1 else jnp.concatenate(cols, axis=1)


def _dup_heads(kv):
    lane = lax.broadcasted_iota(jnp.int32, kv.shape, 1)
    low = lane < HEAD_DIM
    swapped = pltpu.roll(kv, HEAD_DIM, 1)
    return (jnp.where(low, kv, swapped).astype(jnp.bfloat16),
            jnp.where(low, swapped, kv).astype(jnp.bfloat16))


def _attend_chunk(q, k_wins, v_wins, valid, sinks_ref):
    lane = lax.broadcasted_iota(jnp.int32, (CHUNK, LANES), 1)
    low = lane < HEAD_DIM
    zero = jnp.zeros((CHUNK, LANES), jnp.bfloat16)
    cols_per_group = KV_GROUP * HEAD_DIM // LANES
    out_cols = []
    for g in range(N_KV_HEADS):
        stacked = []
        for c in range(g * cols_per_group, (g + 1) * cols_per_group):
            qc = q[:, c * LANES:(c + 1) * LANES]
            stacked.append(jnp.where(low, qc, zero))
            stacked.append(jnp.where(low, zero, qc))
        qst = jnp.concatenate(stacked, axis=0)
        s = lax.dot_general(qst, k_wins[g], (((1,), (1,)), ((), ())),
                            preferred_element_type=jnp.float32)
        if valid is not None:
            s = jnp.where(valid, s, NEG_BIG)
        probs = []
        for hh in range(KV_GROUP):
            sh = s[hh * CHUNK:(hh + 1) * CHUNK]
            sink = sinks_ref[g * KV_GROUP + hh]
            m = jnp.maximum(jnp.max(sh, -1, keepdims=True), sink)
            p = jnp.exp(sh - m)
            denom = jnp.sum(p, -1, keepdims=True) + jnp.exp(sink - m)
            probs.append((p * (1.0 / denom)).astype(jnp.bfloat16))
        pst = jnp.concatenate(probs, axis=0)
        o = jnp.dot(pst, v_wins[g], preferred_element_type=jnp.float32)
        for j in range(cols_per_group):
            oa = o[(2 * j) * CHUNK:(2 * j + 1) * CHUNK]
            ob = o[(2 * j + 1) * CHUNK:(2 * j + 2) * CHUNK]
            out_cols.append(jnp.where(low, oa, ob))
    return jnp.concatenate(out_cols, axis=1)


def _project_qkv(x, wqkv_ref, bqkv_ref, cos_ref, sin_ref):
    qkv = jnp.dot(x.astype(jnp.bfloat16), wqkv_ref[...], preferred_element_type=jnp.float32)
    qkv = qkv + bqkv_ref[...]
    nq = N_HEADS * HEAD_DIM
    nk = N_KV_HEADS * HEAD_DIM
    cos = cos_ref[...]
    sin = sin_ref[...]
    q = _rope(qkv[:, :nq], cos, sin)
    k = _rope(qkv[:, nq:nq + nk], cos, sin)
    v = qkv[:, nq + nk:]
    qs = (q * (HEAD_DIM ** -0.5)).astype(jnp.bfloat16)
    return qs, k, v


def _swa_kernel(xp_ref, xs_ref, ck_ref, cv_ref, wqkv_ref, bqkv_ref, cosp_ref, sinp_ref, coss_ref, sins_ref,
                sinks_ref, wo_ref, g_ref, b_ref, x1_ref, kp_ref, vp_ref, ks_ref, vs_ref,
                k0_scr, k1_scr, v0_scr, v1_scr):
    c = pl.program_id(0)
    t = pl.program_id(1)
    scrs = (k0_scr, k1_scr, v0_scr, v1_scr)

    @pl.when(c < BATCH)
    def _():
        @pl.when(t == 0)
        def _():
            for scr in scrs:
                scr[0:WINDOW, :] = jnp.zeros((WINDOW, LANES), jnp.bfloat16)

        x = xp_ref[...]
        qs, k, v = _project_qkv(x, wqkv_ref, bqkv_ref, cosp_ref, sinp_ref)
        kp_ref[...] = k[ROW_TILE - WINDOW:]
        vp_ref[...] = v[ROW_TILE - WINDOW:]
        k0, k1 = _dup_heads(k)
        v0, v1 = _dup_heads(v)
        for scr, val in zip(scrs, (k0, k1, v0, v1)):
            scr[WINDOW:WINDOW + ROW_TILE, :] = val

        span = WINDOW + CHUNK
        key_pos = lax.broadcasted_iota(jnp.int32, (1, span), 1)
        chunks = []
        for ci in range(ROW_TILE // CHUNK):
            lo = ci * CHUNK
            valid = (t * ROW_TILE + lo - WINDOW + key_pos) >= 0
            k_wins = (k0_scr[lo:lo + span, :], k1_scr[lo:lo + span, :])
            v_wins = (v0_scr[lo:lo + span, :], v1_scr[lo:lo + span, :])
            chunks.append(_attend_chunk(qs[lo:lo + CHUNK], k_wins, v_wins, valid, sinks_ref))
        o = jnp.concatenate(chunks, axis=0)
        for scr in scrs:
            scr[0:WINDOW, :] = scr[ROW_TILE:ROW_TILE + WINDOW, :]
        out = jnp.dot(o.astype(jnp.bfloat16), wo_ref[...], preferred_element_type=jnp.float32)
        _finish_mixer(x, out, g_ref, b_ref, x1_ref)

    @pl.when(c == SAMPLE_CHUNK)
    def _():
        x = xs_ref[...]
        qs, k, v = _project_qkv(x, wqkv_ref, bqkv_ref, coss_ref, sins_ref)
        knew = _dup_heads(k)
        vnew = _dup_heads(v)
        chunks = []
        for s in range(SAMPLE_SEGS):
            lo = s * DEC_SEQ
            ck = ck_ref[s]
            cv = cv_ref[s]
            kold = _dup_heads(ck)
            vold = _dup_heads(cv)
            k_wins = [jnp.concatenate([kold[g], knew[g][lo:lo + DEC_SEQ]], axis=0) for g in range(N_KV_HEADS)]
            v_wins = [jnp.concatenate([vold[g], vnew[g][lo:lo + DEC_SEQ]], axis=0) for g in range(N_KV_HEADS)]
            chunks.append(_attend_chunk(qs[lo:lo + DEC_SEQ], k_wins, v_wins, None, sinks_ref))
            ks_ref[s, 0:WINDOW - DEC_SEQ, :] = ck[DEC_SEQ:]
            ks_ref[s, WINDOW - DEC_SEQ:WINDOW, :] = k[lo:lo + DEC_SEQ]
            vs_ref[s, 0:WINDOW - DEC_SEQ, :] = cv[DEC_SEQ:]
            vs_ref[s, WINDOW - DEC_SEQ:WINDOW, :] = v[lo:lo + DEC_SEQ]
        o = jnp.concatenate(chunks, axis=0)
        out = jnp.dot(o.astype(jnp.bfloat16), wo_ref[...], preferred_element_type=jnp.float32)
        _finish_mixer(x, out, g_ref, b_ref, x1_ref)


def _swa(xp, xs, xs_chunk, ck, cv, w, cos_p, sin_p, cos_s, sin_s, ln_g, ln_b):
    wqkv, bqkv, sinks, wo = w
    nt_s = DEC_BATCH // SAMPLE_SEGS
    kv_lanes = N_KV_HEADS * HEAD_DIM
    cache_spec = pl.BlockSpec((None, SAMPLE_SEGS, WINDOW, kv_lanes), _PER_SAMPLE4)
    prompt_table = pl.BlockSpec((ROW_TILE, LANES), lambda c, t: (_prompt_tile(c, t), 0))
    kv_prompt = pl.BlockSpec((None, WINDOW, kv_lanes), _PER_PROMPT)
    return pl.pallas_call(
        _swa_kernel,
        grid=_MIXER_GRID,
        in_specs=_mixer_act_specs(xs_chunk) + [
            cache_spec, cache_spec,
            _const_spec((D_MODEL, QKV_DIM)),
            _const_spec((1, QKV_DIM)),
            prompt_table, prompt_table,
            _const_spec((ROW_TILE, LANES)), _const_spec((ROW_TILE, LANES)),
            pl.BlockSpec(memory_space=pltpu.SMEM),
            _const_spec((N_HEADS * HEAD_DIM, D_MODEL)),
            _const_spec((1, D_MODEL)),
            _const_spec((1, D_MODEL)),
        ],
        out_specs=[
            pl.BlockSpec((None, ROW_TILE, D_MODEL), lambda c, t: (c, t, 0)),
            kv_prompt, kv_prompt, cache_spec, cache_spec,
        ],
        out_shape=[
            _ACT_SHAPE,
            jax.ShapeDtypeStruct((BATCH, WINDOW, kv_lanes), jnp.float32),
            jax.ShapeDtypeStruct((BATCH, WINDOW, kv_lanes), jnp.float32),
            jax.ShapeDtypeStruct((nt_s, SAMPLE_SEGS, WINDOW, kv_lanes), jnp.float32),
            jax.ShapeDtypeStruct((nt_s, SAMPLE_SEGS, WINDOW, kv_lanes), jnp.float32),
        ],
        scratch_shapes=[pltpu.VMEM((WINDOW + ROW_TILE, LANES), jnp.bfloat16) for _ in range(4)],
        compiler_params=pltpu.CompilerParams(
            dimension_semantics=("arbitrary", "arbitrary"), vmem_limit_bytes=VMEM_LIMIT),
        name="swa",
    )(xp, xs, ck, cv, wqkv, bqkv, cos_p, sin_p, cos_s, sin_s, sinks, wo, ln_g, ln_b)


META_TILE_EXPERT, META_N_ACTIVE, META_FILL_LO, META_FILL_HI = 0, 1, 2, 3


def _router_kernel(x_ref, wr_ref, br_ref, pos_ref, gate_ref, meta_ref):
    f32 = jnp.float32
    logits = lax.dot_general(wr_ref[...].astype(jnp.bfloat16), x_ref[...].astype(jnp.bfloat16),
                             (((1,), (1,)), ((), ())), preferred_element_type=f32)
    m = jnp.max(logits, axis=0, keepdims=True)
    e = jnp.exp(logits - m)
    probs = e / jnp.sum(e, axis=0, keepdims=True)
    sel = probs + br_ref[...]
    p = [probs[i:i + 1] for i in range(N_EXPERTS)]
    s = [sel[i:i + 1] for i in range(N_EXPERTS)]

    def first_argmax(vals):
        best, arg = vals[0], jnp.zeros_like(vals[0], dtype=jnp.int32)
        for i in range(1, len(vals)):
            better = vals[i] > best
            best = jnp.where(better, vals[i], best)
            arg = jnp.where(better, i, arg)
        return arg

    group_scores = []
    for gidx in range(N_GROUPS):
        v = s[gidx * EXPERTS_PER_GROUP:(gidx + 1) * EXPERTS_PER_GROUP]
        best = v[0] + v[1]
        for i in range(EXPERTS_PER_GROUP):
            for j in range(i + 1, EXPERTS_PER_GROUP):
                if (i, j) != (0, 1):
                    best = jnp.maximum(best, v[i] + v[j])
        group_scores.append(best)
    best_group = first_argmax(group_scores)
    neg_inf = jnp.full_like(s[0], -jnp.inf)
    masked = [jnp.where(best_group == (i // EXPERTS_PER_GROUP), s[i], neg_inf) for i in range(N_EXPERTS)]
    idx1 = first_argmax(masked)
    masked2 = [jnp.where(idx1 == i, neg_inf, masked[i]) for i in range(N_EXPERTS)]
    idx2 = first_argmax(masked2)
    zero = jnp.zeros_like(p[0])
    w1, w2 = zero, zero
    for i in range(N_EXPERTS):
        w1 = w1 + jnp.where(idx1 == i, p[i], zero)
        w2 = w2 + jnp.where(idx2 == i, p[i], zero)
    tot = w1 + w2
    gate_ref[0:1, :] = w1 / tot
    gate_ref[1:2, :] = w2 / tot

    expert = lax.broadcasted_iota(jnp.int32, (N_EXPERTS, MOE_CHUNK), 0)
    strict_upper = (lax.broadcasted_iota(jnp.int32, (LANES, LANES), 0)
                    < lax.broadcasted_iota(jnp.int32, (LANES, LANES), 1)).astype(jnp.bfloat16)
    carry = jnp.zeros((N_EXPERTS, 1), f32)
    onehots, ranks = [], []
    for idx in (idx1, idx2):
        onehot = (expert == idx).astype(f32)
        before = []
        for blk in range(MOE_CHUNK // LANES):
            oh = onehot[:, blk * LANES:(blk + 1) * LANES]
            before.append(jnp.dot(oh.astype(jnp.bfloat16), strict_upper, preferred_element_type=f32) + carry)
            carry = carry + jnp.sum(oh, axis=1, keepdims=True)
        onehots.append(onehot)
        ranks.append(jnp.sum(onehot * jnp.concatenate(before, axis=1), axis=0, keepdims=True))
    counts = carry
    padded = jnp.floor((counts + (MOE_TM - 1)) * (1.0 / MOE_TM)) * MOE_TM
    strict_lower = (lax.broadcasted_iota(jnp.int32, (N_EXPERTS, N_EXPERTS), 0)
                    > lax.broadcasted_iota(jnp.int32, (N_EXPERTS, N_EXPERTS), 1)).astype(jnp.bfloat16)
    starts = jnp.dot(strict_lower, jnp.broadcast_to(padded, (N_EXPERTS, LANES)).astype(jnp.bfloat16),
                     preferred_element_type=f32)[:, 0:1]
    ends = starts + padded
    for k in range(2):
        pos = jnp.sum(onehots[k] * starts, axis=0, keepdims=True) + ranks[k]
        pos_ref[k:k + 1, :] = pos.astype(jnp.int32)

    lane = lax.broadcasted_iota(jnp.int32, (N_EXPERTS, LANES), 1)
    sub = lax.broadcasted_iota(jnp.int32, (N_EXPERTS, LANES), 0)
    tile_start = (lane * MOE_TM).astype(f32)
    tile_expert = jnp.sum((ends <= tile_start).astype(f32), axis=0, keepdims=True)
    tile_expert = jnp.minimum(tile_expert, N_EXPERTS - 1.0)
    on_diag = sub == lane

    def to_lanes(col):
        return jnp.sum(jnp.where(on_diag, col, 0.0), axis=0, keepdims=True)

    n_active = jnp.broadcast_to(ends[N_EXPERTS - 1:N_EXPERTS] * (1.0 / MOE_TM), (1, LANES))
    meta_ref[...] = jnp.zeros(meta_ref.shape, jnp.int32)
    meta_ref[META_TILE_EXPERT:META_TILE_EXPERT + 1, :] = tile_expert.astype(jnp.int32)
    meta_ref[META_N_ACTIVE:META_N_ACTIVE + 1, :] = n_active.astype(jnp.int32)
    meta_ref[META_FILL_LO:META_FILL_LO + 1, :] = to_lanes(starts + counts).astype(jnp.int32)
    meta_ref[META_FILL_HI:META_FILL_HI + 1, :] = to_lanes(ends).astype(jnp.int32)


def _router(x, wr_t, br):
    per_chunk = lambda c: (c, 0, 0)
    return pl.pallas_call(
        _router_kernel,
        grid=(N_CHUNKS,),
        in_specs=[
            pl.BlockSpec((None, MOE_CHUNK, D_MODEL), per_chunk),
            _const_spec((N_EXPERTS, D_MODEL)),
            _const_spec((N_EXPERTS, 1)),
        ],
        out_specs=[
            pl.BlockSpec((None, 2, MOE_CHUNK), per_chunk),
            pl.BlockSpec((None, 2, MOE_CHUNK), per_chunk),
            pl.BlockSpec((None, SUBLANES, LANES), per_chunk),
        ],
        out_shape=[
            jax.ShapeDtypeStruct((N_CHUNKS, 2, MOE_CHUNK), jnp.int32),
            jax.ShapeDtypeStruct((N_CHUNKS, 2, MOE_CHUNK), jnp.float32),
            jax.ShapeDtypeStruct((N_CHUNKS, SUBLANES, LANES), jnp.int32),
        ],
        compiler_params=pltpu.CompilerParams(
            dimension_semantics=("parallel",), vmem_limit_bytes=VMEM_LIMIT),
        name="router",
    )(x, wr_t, br)


def _gather_rows(pair_scr, x_ref, xs_scr, row0, j):
    tok = pair_scr[row0 + j] & (MOE_CHUNK - 1)
    xs_scr[j // SUBLANES, (j % SUBLANES):(j % SUBLANES) + 1, :] = x_ref[pl.ds(tok, 1), :]


def _scatter_rows(pair_scr, y_scr, yo_scr, row0, j):
    y_scr[pl.ds(pair_scr[row0 + j], 1), :] = yo_scr[j // SUBLANES, (j % SUBLANES):(j % SUBLANES) + 1, :]


def _moe_kernel(te_ref, nact_ref, flo_ref, fhi_ref, pos_ref, x_ref, gate_ref, wg_ref, wu_ref, wd_ref,
                g_ref, b_ref, out_ref, pair_scr, xs_a, xs_b, yo_a, yo_b, y_scr):
    c = pl.program_id(0)
    i = pl.program_id(1)
    n_act = nact_ref[c]
    odd = i & 1

    @pl.when(i == 0)
    def _():
        def fill(r, carry):
            pair_scr[r] = JUNK_PAIR
            return carry

        def fill_expert(e, carry):
            lax.fori_loop(flo_ref[c * N_EXPERTS + e], fhi_ref[c * N_EXPERTS + e], fill, 0)
            return carry
        lax.fori_loop(0, N_EXPERTS, fill_expert, 0)
        end = fhi_ref[c * N_EXPERTS + N_EXPERTS - 1]
        lax.fori_loop(end, end + MOE_TM, fill, 0)

        def invert(t, carry):
            pair_scr[pos_ref[t]] = t
            pair_scr[pos_ref[MOE_CHUNK + t]] = MOE_CHUNK + t
            return carry
        lax.fori_loop(0, MOE_CHUNK, invert, 0, unroll=8)

        def gather(gi, carry):
            for u in range(SUBLANES):
                tok = pair_scr[gi * SUBLANES + u] & (MOE_CHUNK - 1)
                xs_a[gi, u:u + 1, :] = x_ref[pl.ds(tok, 1), :]
            return carry
        lax.fori_loop(0, MOE_TM // SUBLANES, gather, 0)

        @pl.when(c == 0)
        def _():
            yo_b[...] = jnp.zeros_like(yo_b)

    def expert_step(xs_cur, yo_cur, xs_nxt, yo_prv):
        row_nxt = (i + 1) * MOE_TM
        row_prv = jnp.maximum(i - 1, 0) * MOE_TM
        for j in range(MOE_TM):
            _gather_rows(pair_scr, x_ref, xs_nxt, row_nxt, j)
        xb = xs_cur[...].reshape(MOE_TM, D_MODEL).astype(jnp.bfloat16)
        hg = jnp.dot(xb, wg_ref[...], preferred_element_type=jnp.float32)
        hu = jnp.dot(xb, wu_ref[...], preferred_element_type=jnp.float32)
        act = (jax.nn.silu(hg) * hu).astype(jnp.bfloat16)
        yo = jnp.dot(act, wd_ref[...], preferred_element_type=jnp.float32)
        yo_cur[...] = yo.reshape(MOE_TM // SUBLANES, SUBLANES, D_MODEL)
        for j in range(MOE_TM):
            _scatter_rows(pair_scr, y_scr, yo_prv, row_prv, j)

    def drain_step(yo_prv):
        row_prv = (i - 1) * MOE_TM

        def scatter(gi, carry):
            for u in range(SUBLANES):
                y_scr[pl.ds(pair_scr[row_prv + gi * SUBLANES + u], 1), :] = yo_prv[gi, u:u + 1, :]
            return carry
        lax.fori_loop(0, MOE_TM // SUBLANES, scatter, 0)

    for parity, (xs_cur, yo_cur, xs_oth, yo_oth) in enumerate(((xs_a, yo_a, xs_b, yo_b),
                                                               (xs_b, yo_b, xs_a, yo_a))):
        @pl.when(jnp.logical_and(i < n_act, odd == parity))
        def _():
            expert_step(xs_cur, yo_cur, xs_oth, yo_oth)

        @pl.when(jnp.logical_and(i == n_act, odd == parity))
        def _():
            drain_step(yo_oth)

    @pl.when(i >= MOE_STEPS)
    def _():
        lo = pl.multiple_of((i - MOE_STEPS) * ROW_TILE, ROW_TILE)
        gates = gate_ref[...]
        moe = (gates[:, 0:1] * y_scr[pl.ds(lo, ROW_TILE), :]
               + gates[:, 1:2] * y_scr[pl.ds(MOE_CHUNK + lo, ROW_TILE), :])
        x = x_ref[pl.ds(lo, ROW_TILE), :]
        out_ref[...] = _layer_norm(DEEPNORM_ALPHA * x + moe, g_ref[...], b_ref[...])


def _moe(x, plan, gates_t, layer, wg, wu, wd, ln_g, ln_b):
    tile_expert, n_active, fill_lo, fill_hi, pos = plan

    def tile_of(c, i):
        return c * MOE_TILES + jnp.minimum(i, MOE_TILES - 1)

    def sub_of(i):
        return jnp.maximum(i - MOE_STEPS, 0)

    def expert_of(c, i, te):
        return te[tile_of(c, i)]

    stage = pltpu.VMEM((MOE_TM // SUBLANES, SUBLANES, D_MODEL), jnp.float32)
    grid_spec = pltpu.PrefetchScalarGridSpec(
        num_scalar_prefetch=4,
        grid=(N_CHUNKS, MOE_STEPS + COMBINE_STEPS),
        in_specs=[
            pl.BlockSpec((None, None, 2 * MOE_CHUNK), lambda c, i, *_: (c, 0, 0), memory_space=pltpu.SMEM),
            pl.BlockSpec((None, MOE_CHUNK, D_MODEL), lambda c, i, *_: (c, 0, 0)),
            pl.BlockSpec((None, ROW_TILE, 2), lambda c, i, *_: (c, sub_of(i), 0)),
            pl.BlockSpec((None, None, D_MODEL, D_EXPERT),
                         lambda c, i, te, *_: (layer, expert_of(c, i, te), 0, 0)),
            pl.BlockSpec((None, None, D_MODEL, D_EXPERT),
                         lambda c, i, te, *_: (layer, expert_of(c, i, te), 0, 0)),
            pl.BlockSpec((None, None, D_EXPERT, D_MODEL),
                         lambda c, i, te, *_: (layer, expert_of(c, i, te), 0, 0)),
            pl.BlockSpec((1, D_MODEL), lambda c, i, *_: (0, 0)),
            pl.BlockSpec((1, D_MODEL), lambda c, i, *_: (0, 0)),
        ],
        out_specs=pl.BlockSpec((None, ROW_TILE, D_MODEL), lambda c, i, *_: (c, sub_of(i), 0)),
        scratch_shapes=[
            pltpu.SMEM((MOE_ROWS + MOE_TM,), jnp.int32),
            stage, stage, stage, stage,
            pltpu.VMEM((2 * MOE_CHUNK + SUBLANES, D_MODEL), jnp.float32),
        ],
    )
    return pl.pallas_call(
        _moe_kernel,
        grid_spec=grid_spec,
        out_shape=_ACT_SHAPE,
        compiler_params=pltpu.CompilerParams(
            dimension_semantics=("arbitrary", "arbitrary"), vmem_limit_bytes=VMEM_LIMIT),
        name="moe",
    )(tile_expert, n_active, fill_lo, fill_hi, pos.reshape(N_CHUNKS, 1, 2 * MOE_CHUNK),
      x, gates_t, wg, wu, wd, ln_g, ln_b)


def _moe_layer(x, wr_t, br, layer, wg, wu, wd, ln_g, ln_b):
    pos, gates, meta = _router(x, wr_t, br)
    plan = (meta[:, META_TILE_EXPERT, :MOE_TILES].reshape(-1), meta[:, META_N_ACTIVE, 0],
            meta[:, META_FILL_LO, :N_EXPERTS].reshape(-1), meta[:, META_FILL_HI, :N_EXPERTS].reshape(-1), pos)
    return _moe(x, plan, jnp.swapaxes(gates, 1, 2), layer, wg, wu, wd, ln_g, ln_b)


def _rope_tables(pos):
    half = HEAD_DIM // 2
    inv_freq = ROPE_THETA ** (-jnp.arange(half, dtype=jnp.float32) / half)
    ang = pos.astype(jnp.float32)[:, None] * inv_freq[None, :]
    cos = jnp.cos(ang)
    sin = jnp.sin(ang)
    reps = LANES // HEAD_DIM
    cos_t = jnp.tile(jnp.concatenate([cos, cos], axis=1), (1, reps))
    sin_t = jnp.tile(jnp.concatenate([-sin, sin], axis=1), (1, reps))
    return cos_t, sin_t


def _block_diag_tiles(w):
    per = MXU_EDGE // LRU_BLOCK_W
    w4 = w.reshape(LRU_BLOCKS // per, per, LRU_BLOCK_W, LRU_BLOCK_W)
    eye = jnp.eye(per, dtype=w.dtype)
    return jnp.einsum("qaij,ab->qaibj", w4, eye).reshape(LRU_BLOCKS // per, MXU_EDGE, MXU_EDGE)


def kernel(x_prompt, x_sample, state_conv_0, state_h_0, cache_k_1, cache_v_1, state_conv_2, state_h_2,
           cache_k_3, cache_v_3, w_router, b_router, ln_g, ln_b, lru_w_in, lru_conv_w, lru_conv_b,
           lru_w_gates, lru_b_gates, lru_lambda, lru_w_out, attn_w_qkv, attn_b_qkv, attn_sinks, attn_w_o,
           moe_w_gate, moe_w_up, moe_w_down):
    bf = jnp.bfloat16
    in_state = {0: (state_conv_0, state_h_0), 1: (cache_k_1, cache_v_1),
                2: (state_conv_2, state_h_2), 3: (cache_k_3, cache_v_3)}
    nt_s = DEC_BATCH // SAMPLE_SEGS
    kv_lanes = N_KV_HEADS * HEAD_DIM

    wr_t = w_router.T
    br = b_router.reshape(N_EXPERTS, 1)
    cos_p, sin_p = _rope_tables(jnp.arange(SEQ))
    cos_s, sin_s = _rope_tables(PAST_LEN + jnp.arange(DEC_SEQ))
    cos_s = jnp.tile(cos_s, (SAMPLE_SEGS, 1))
    sin_s = jnp.tile(sin_s, (SAMPLE_SEGS, 1))
    wg_all = moe_w_gate.astype(bf)
    wu_all = moe_w_up.astype(bf)
    wd_all = moe_w_down.astype(bf)

    x = x_prompt
    x_s, x_s_chunk = x_sample.reshape(1, MOE_CHUNK, D_MODEL), 0
    new_p, new_s = {}, {}
    for i in range(DEPTH):
        slot = i // 2
        g0 = ln_g[i, 0].reshape(1, D_MODEL)
        b0 = ln_b[i, 0].reshape(1, D_MODEL)
        g1 = ln_g[i, 1].reshape(1, D_MODEL)
        b1 = ln_b[i, 1].reshape(1, D_MODEL)
        st_a, st_b = in_state[i]
        if i % 2 == 0:
            gates = lru_w_gates[slot]
            w = (lru_w_in[slot].astype(bf), lru_conv_w[slot], lru_conv_b[slot].reshape(1, D_RNN),
                 _block_diag_tiles(gates[:, :, :LRU_BLOCK_W]).astype(bf),
                 _block_diag_tiles(gates[:, :, LRU_BLOCK_W:]).astype(bf),
                 lru_b_gates[slot][:, :LRU_BLOCK_W].reshape(1, D_RNN),
                 lru_b_gates[slot][:, LRU_BLOCK_W:].reshape(1, D_RNN),
                 lru_lambda[slot].reshape(1, D_RNN), lru_w_out[slot].astype(bf))
            conv_pad = jnp.pad(st_a, ((0, 0), (SUBLANES - (CONV_WIDTH - 1), 0), (0, 0)))
            x1, conv_p, h_p, conv_s, h_s = _lru(
                x, x_s, x_s_chunk, conv_pad.reshape(nt_s, SAMPLE_SEGS, SUBLANES, D_RNN),
                st_b.reshape(nt_s, SAMPLE_SEGS, D_RNN), w, g0, b0)
            new_p[i] = (conv_p, h_p.reshape(BATCH, D_RNN))
            new_s[i] = (conv_s.reshape(DEC_BATCH, CONV_WIDTH - 1, D_RNN), h_s.reshape(DEC_BATCH, D_RNN))
        else:
            w = (attn_w_qkv[slot].astype(bf), attn_b_qkv[slot].reshape(1, QKV_DIM), attn_sinks[slot],
                 attn_w_o[slot].astype(bf))
            x1, k_p, v_p, k_s, v_s = _swa(
                x, x_s, x_s_chunk, st_a.reshape(nt_s, SAMPLE_SEGS, WINDOW, kv_lanes),
                st_b.reshape(nt_s, SAMPLE_SEGS, WINDOW, kv_lanes), w, cos_p, sin_p, cos_s, sin_s, g0, b0)
            new_p[i] = (k_p.reshape(BATCH, WINDOW, N_KV_HEADS, HEAD_DIM),
                        v_p.reshape(BATCH, WINDOW, N_KV_HEADS, HEAD_DIM))
            new_s[i] = (k_s.reshape(DEC_BATCH, WINDOW, N_KV_HEADS, HEAD_DIM),
                        v_s.reshape(DEC_BATCH, WINDOW, N_KV_HEADS, HEAD_DIM))
        x = _moe_layer(x1, wr_t, br, i, wg_all, wu_all, wd_all, g1, b1)
        x_s, x_s_chunk = x, SAMPLE_CHUNK
    return (x[:BATCH], x[SAMPLE_CHUNK].reshape(DEC_BATCH, DEC_SEQ, D_MODEL),
            new_p[0][0], new_s[0][0], new_p[0][1], new_s[0][1],
            new_p[1][0], new_s[1][0], new_p[1][1], new_s[1][1],
            new_p[2][0], new_s[2][0], new_p[2][1], new_s[2][1],
            new_p[3][0], new_s[3][0], new_p[3][1], new_s[3][1])
```

```python
import functools

import jax
import jax.numpy as jnp
from jax import lax
from jax.experimental import pallas as pl
from jax.experimental.pallas import tpu as pltpu

D_MODEL = 1024
BATCH = 8
SEQ = 2048
DEPTH = 4
DEC_BATCH = 32
DEC_SEQ = 64
PAST_LEN = 4096
CHUNK = 64
D_RNN = D_MODEL
CONV_WIDTH = 4
LRU_BLOCKS = 16
LRU_BLOCK_W = D_RNN // LRU_BLOCKS
LRU_C = 8.0
N_HEADS = 16
N_KV_HEADS = 2
HEAD_DIM = 64
KV_GROUP = N_HEADS // N_KV_HEADS
WINDOW = 128
ROPE_THETA = 10000.0
QKV_DIM = (N_HEADS + 2 * N_KV_HEADS) * HEAD_DIM
N_EXPERTS = 16
N_GROUPS = 4
EXPERTS_PER_GROUP = N_EXPERTS // N_GROUPS
D_EXPERT = 512
DEEPNORM_ALPHA = (2.0 * DEPTH) ** 0.25
LN_EPS = 1e-5

MXU_EDGE = 256
LANES = 128
SUBLANES = 8

ROW_TILE = 256
MOE_CHUNK = SEQ
N_CHUNKS = BATCH + DEC_BATCH * DEC_SEQ // MOE_CHUNK
SAMPLE_CHUNK = BATCH
MOE_TM = 256
MOE_ROWS = 2 * MOE_CHUNK + N_EXPERTS * MOE_TM
MOE_TILES = MOE_ROWS // MOE_TM
MOE_STEPS = MOE_TILES + 1
COMBINE_STEPS = MOE_CHUNK // ROW_TILE
JUNK_PAIR = 2 * MOE_CHUNK
SAMPLE_SEGS = ROW_TILE // DEC_SEQ
NEG_BIG = -1e30
VMEM_LIMIT = 52 * 1024 * 1024

assert DEC_BATCH * DEC_SEQ == MOE_CHUNK and MOE_TILES <= LANES


def _layer_norm(x, g, b):
    mu = jnp.mean(x, -1, keepdims=True)
    xc = x - mu
    var = jnp.mean(xc * xc, -1, keepdims=True)
    return xc * lax.rsqrt(var + LN_EPS) * g + b


def _finish_mixer(x, y, g_ref, b_ref, x1_ref):
    x1_ref[...] = _layer_norm(DEEPNORM_ALPHA * x + y, g_ref[...], b_ref[...])


def _const_spec(shape):
    nd = len(shape)
    return pl.BlockSpec(shape, lambda *_: (0,) * nd)


_ACT_SHAPE = jax.ShapeDtypeStruct((N_CHUNKS, MOE_CHUNK, D_MODEL), jnp.float32)


def _lru_tile(x, ext_scr, h_in, n_seg, seg, win_ref, cw_ref, cb_ref, wr_ref, wi_ref,
              br_ref, bi_ref, lam_ref, wout_ref):
    rows = n_seg * seg
    gr = jnp.dot(x.astype(jnp.bfloat16), win_ref[...], preferred_element_type=jnp.float32)
    gate_in = gr[:, :D_RNN]
    rec = gr[:, D_RNN:]
    cw = cw_ref[...]
    xcs = []
    for s in range(n_seg):
        ext_scr[s, SUBLANES:SUBLANES + seg, :] = rec[s * seg:(s + 1) * seg]
        xc = cb_ref[...] + cw[3:4] * rec[s * seg:(s + 1) * seg]
        for k in range(CONV_WIDTH - 1):
            off = SUBLANES - (CONV_WIDTH - 1) + k
            xc = xc + cw[k:k + 1] * ext_scr[s, off:off + seg, :]
        xcs.append(xc)
    xc = xcs[0] if n_seg == 1 else jnp.concatenate(xcs, axis=0)

    xcb = xc.astype(jnp.bfloat16)
    r_parts, i_parts = [], []
    for q in range(D_RNN // MXU_EDGE):
        blk = xcb[:, q * MXU_EDGE:(q + 1) * MXU_EDGE]
        r_parts.append(jnp.dot(blk, wr_ref[q], preferred_element_type=jnp.float32))
        i_parts.append(jnp.dot(blk, wi_ref[q], preferred_element_type=jnp.float32))
    r = jax.nn.sigmoid(jnp.concatenate(r_parts, axis=1) + br_ref[...])
    ig = jax.nn.sigmoid(jnp.concatenate(i_parts, axis=1) + bi_ref[...])

    z = -lam_ref[...]
    softplus = jnp.maximum(z, 0.0) + jnp.log1p(jnp.exp(-jnp.abs(z)))
    log_a = (-LRU_C * r) * softplus
    a = jnp.exp(log_a)
    b = jnp.sqrt(-jnp.tanh(log_a) * (a * a + 1.0)) * (ig * xc)

    sub = lax.broadcasted_iota(jnp.int32, (rows, D_RNN), 0) & (SUBLANES - 1)
    step = 1
    while step < SUBLANES:
        keep = sub >= step
        a_prev = pltpu.roll(a, step, 0)
        b_prev = pltpu.roll(b, step, 0)
        b = jnp.where(keep, a * b_prev + b, b)
        a = jnp.where(keep, a * a_prev, a)
        step *= 2
    groups_per_seg = seg // SUBLANES
    h_blocks, h_last = [], []
    carry = None
    for gi in range(rows // SUBLANES):
        if gi % groups_per_seg == 0:
            carry = h_in[gi // groups_per_seg]
        lo = gi * SUBLANES
        hb = b[lo:lo + SUBLANES] + a[lo:lo + SUBLANES] * carry
        carry = hb[SUBLANES - 1:SUBLANES]
        h_blocks.append(hb)
        if gi % groups_per_seg == groups_per_seg - 1:
            h_last.append(carry)
    h = jnp.concatenate(h_blocks, axis=0)

    y = h * jax.nn.gelu(gate_in, approximate=True)
    out = jnp.dot(y.astype(jnp.bfloat16), wout_ref[...], preferred_element_type=jnp.float32)
    return out, rec, h_last


def _lru_kernel(xp_ref, xs_ref, conv_in_ref, h0_ref, win_ref, cw_ref, cb_ref, wr_ref, wi_ref, br_ref,
                bi_ref, lam_ref, wout_ref, g_ref, b_ref, x1_ref, conv_p_ref, h_p_ref, conv_s_ref, h_s_ref,
                ext_p_scr, h_scr, ext_s_scr):
    c = pl.program_id(0)
    t = pl.program_id(1)
    weights = (win_ref, cw_ref, cb_ref, wr_ref, wi_ref, br_ref, bi_ref, lam_ref, wout_ref)

    @pl.when(c < BATCH)
    def _():
        @pl.when(t == 0)
        def _():
            ext_p_scr[0, 0:SUBLANES, :] = jnp.zeros((SUBLANES, D_RNN), jnp.float32)
            h_scr[...] = jnp.zeros_like(h_scr)

        x = xp_ref[...]
        out, rec, h_last = _lru_tile(x, ext_p_scr, [h_scr[...]], 1, ROW_TILE, *weights)
        ext_p_scr[0, 0:SUBLANES, :] = rec[ROW_TILE - SUBLANES:]
        h_scr[...] = h_last[0]
        conv_p_ref[...] = rec[ROW_TILE - (CONV_WIDTH - 1):]
        h_p_ref[...] = h_last[0]
        _finish_mixer(x, out, g_ref, b_ref, x1_ref)

    @pl.when(c == SAMPLE_CHUNK)
    def _():
        for s in range(SAMPLE_SEGS):
            ext_s_scr[s, 0:SUBLANES, :] = conv_in_ref[s]
        h_in = [h0_ref[s:s + 1, :] for s in range(SAMPLE_SEGS)]
        x = xs_ref[...]
        out, rec, h_last = _lru_tile(x, ext_s_scr, h_in, SAMPLE_SEGS, DEC_SEQ, *weights)
        for s in range(SAMPLE_SEGS):
            end = (s + 1) * DEC_SEQ
            conv_s_ref[s] = rec[end - (CONV_WIDTH - 1):end]
            h_s_ref[s:s + 1, :] = h_last[s]
        _finish_mixer(x, out, g_ref, b_ref, x1_ref)


_TILES_PER_CHUNK = MOE_CHUNK // ROW_TILE


def _prompt_tile(c, t):
    return jnp.where(c < BATCH, t, _TILES_PER_CHUNK - 1)


def _sample_tile(c, t):
    return jnp.where(c == SAMPLE_CHUNK, t, 0)


def _mixer_act_specs(xs_chunk):
    return [
        pl.BlockSpec((None, ROW_TILE, D_MODEL), lambda c, t: (jnp.minimum(c, BATCH - 1), _prompt_tile(c, t), 0)),
        pl.BlockSpec((None, ROW_TILE, D_MODEL), lambda c, t: (xs_chunk, _sample_tile(c, t), 0)),
    ]


_MIXER_GRID = (N_CHUNKS, _TILES_PER_CHUNK)
_PER_PROMPT = lambda c, t: (jnp.minimum(c, BATCH - 1), 0, 0)
_PER_SAMPLE3 = lambda c, t: (_sample_tile(c, t), 0, 0)
_PER_SAMPLE4 = lambda c, t: (_sample_tile(c, t), 0, 0, 0)


def _lru(xp, xs, xs_chunk, conv_pad, h0, w, ln_g, ln_b):
    nt_s = DEC_BATCH // SAMPLE_SEGS
    return pl.pallas_call(
        _lru_kernel,
        grid=_MIXER_GRID,
        in_specs=_mixer_act_specs(xs_chunk) + [
            pl.BlockSpec((None, SAMPLE_SEGS, SUBLANES, D_RNN), _PER_SAMPLE4),
            pl.BlockSpec((None, SAMPLE_SEGS, D_RNN), _PER_SAMPLE3),
            _const_spec((D_MODEL, 2 * D_RNN)),
            _const_spec((CONV_WIDTH, D_RNN)),
            _const_spec((1, D_RNN)),
            _const_spec((D_RNN // MXU_EDGE, MXU_EDGE, MXU_EDGE)),
            _const_spec((D_RNN // MXU_EDGE, MXU_EDGE, MXU_EDGE)),
            _const_spec((1, D_RNN)),
            _const_spec((1, D_RNN)),
            _const_spec((1, D_RNN)),
            _const_spec((D_RNN, D_MODEL)),
            _const_spec((1, D_MODEL)),
            _const_spec((1, D_MODEL)),
        ],
        out_specs=[
            pl.BlockSpec((None, ROW_TILE, D_MODEL), lambda c, t: (c, t, 0)),
            pl.BlockSpec((None, CONV_WIDTH - 1, D_RNN), _PER_PROMPT),
            pl.BlockSpec((None, 1, D_RNN), _PER_PROMPT),
            pl.BlockSpec((None, SAMPLE_SEGS, CONV_WIDTH - 1, D_RNN), _PER_SAMPLE4),
            pl.BlockSpec((None, SAMPLE_SEGS, D_RNN), _PER_SAMPLE3),
        ],
        out_shape=[
            _ACT_SHAPE,
            jax.ShapeDtypeStruct((BATCH, CONV_WIDTH - 1, D_RNN), jnp.float32),
            jax.ShapeDtypeStruct((BATCH, 1, D_RNN), jnp.float32),
            jax.ShapeDtypeStruct((nt_s, SAMPLE_SEGS, CONV_WIDTH - 1, D_RNN), jnp.float32),
            jax.ShapeDtypeStruct((nt_s, SAMPLE_SEGS, D_RNN), jnp.float32),
        ],
        scratch_shapes=[
            pltpu.VMEM((1, SUBLANES + ROW_TILE, D_RNN), jnp.float32),
            pltpu.VMEM((1, D_RNN), jnp.float32),
            pltpu.VMEM((SAMPLE_SEGS, SUBLANES + DEC_SEQ, D_RNN), jnp.float32),
        ],
        compiler_params=pltpu.CompilerParams(
            dimension_semantics=("arbitrary", "arbitrary"), vmem_limit_bytes=VMEM_LIMIT),
        name="lru",
    )(xp, xs, conv_pad, h0, *w, ln_g, ln_b)


def _rope(x, cos, sin_signed):
    lane = lax.broadcasted_iota(jnp.int32, (x.shape[0], LANES), 1)
    first_half = (lane & (HEAD_DIM - 1)) < HEAD_DIM // 2
    cols = []
    for c in range(x.shape[1] // LANES):
        xc = x[:, c * LANES:(c + 1) * LANES]
        partner = jnp.where(first_half, pltpu.roll(xc, LANES - HEAD_DIM // 2, 1),
                            pltpu.roll(xc, HEAD_DIM // 2, 1))
        cols.append(xc * cos + partner * sin_signed)
    return cols[0] if len(cols) == 1 else jnp.concatenate(cols, axis=1)


def _dup_heads(kv):
    lane = lax.broadcasted_iota(jnp.int32, kv.shape, 1)
    low = lane < HEAD_DIM
    swapped = pltpu.roll(kv, HEAD_DIM, 1)
    return (jnp.where(low, kv, swapped).astype(jnp.bfloat16),
            jnp.where(low, swapped, kv).astype(jnp.bfloat16))


def _attend_chunk(q, k_wins, v_wins, valid, sinks_ref):
    lane = lax.broadcasted_iota(jnp.int32, (CHUNK, LANES), 1)
    low = lane < HEAD_DIM
    zero = jnp.zeros((CHUNK, LANES), jnp.bfloat16)
    cols_per_group = KV_GROUP * HEAD_DIM // LANES
    out_cols = []
    for g in range(N_KV_HEADS):
        stacked = []
        for c in range(g * cols_per_group, (g + 1) * cols_per_group):
            qc = q[:, c * LANES:(c + 1) * LANES]
            stacked.append(jnp.where(low, qc, zero))
            stacked.append(jnp.where(low, zero, qc))
        qst = jnp.concatenate(stacked, axis=0)
        s = lax.dot_general(qst, k_wins[g], (((1,), (1,)), ((), ())),
                            preferred_element_type=jnp.float32)
        if valid is not None:
            s = jnp.where(valid, s, NEG_BIG)
        probs = []
        for hh in range(KV_GROUP):
            sh = s[hh * CHUNK:(hh + 1) * CHUNK]
            sink = sinks_ref[g * KV_GROUP + hh]
            m = jnp.maximum(jnp.max(sh, -1, keepdims=True), sink)
            p = jnp.exp(sh - m)
            denom = jnp.sum(p, -1, keepdims=True) + jnp.exp(sink - m)
            probs.append((p * (1.0 / denom)).astype(jnp.bfloat16))
        pst = jnp.concatenate(probs, axis=0)
        o = jnp.dot(pst, v_wins[g], preferred_element_type=jnp.float32)
        for j in range(cols_per_group):
            oa = o[(2 * j) * CHUNK:(2 * j + 1) * CHUNK]
            ob = o[(2 * j + 1) * CHUNK:(2 * j + 2) * CHUNK]
            out_cols.append(jnp.where(low, oa, ob))
    return jnp.concatenate(out_cols, axis=1)


def _project_qkv(x, wqkv_ref, bqkv_ref, cos_ref, sin_ref):
    qkv = jnp.dot(x.astype(jnp.bfloat16), wqkv_ref[...], preferred_element_type=jnp.float32)
    qkv = qkv + bqkv_ref[...]
    nq = N_HEADS * HEAD_DIM
    nk = N_KV_HEADS * HEAD_DIM
    cos = cos_ref[...]
    sin = sin_ref[...]
    q = _rope(qkv[:, :nq], cos, sin)
    k = _rope(qkv[:, nq:nq + nk], cos, sin)
    v = qkv[:, nq + nk:]
    qs = (q * (HEAD_DIM ** -0.5)).astype(jnp.bfloat16)
    return qs, k, v


def _swa_kernel(xp_ref, xs_ref, ck_ref, cv_ref, wqkv_ref, bqkv_ref, cosp_ref, sinp_ref, coss_ref, sins_ref,
                sinks_ref, wo_ref, g_ref, b_ref, x1_ref, kp_ref, vp_ref, ks_ref, vs_ref,
                k0_scr, k1_scr, v0_scr, v1_scr):
    c = pl.program_id(0)
    t = pl.program_id(1)
    scrs = (k0_scr, k1_scr, v0_scr, v1_scr)

    @pl.when(c < BATCH)
    def _():
        @pl.when(t == 0)
        def _():
            for scr in scrs:
                scr[0:WINDOW, :] = jnp.zeros((WINDOW, LANES), jnp.bfloat16)

        x = xp_ref[...]
        qs, k, v = _project_qkv(x, wqkv_ref, bqkv_ref, cosp_ref, sinp_ref)
        kp_ref[...] = k[ROW_TILE - WINDOW:]
        vp_ref[...] = v[ROW_TILE - WINDOW:]
        k0, k1 = _dup_heads(k)
        v0, v1 = _dup_heads(v)
        for scr, val in zip(scrs, (k0, k1, v0, v1)):
            scr[WINDOW:WINDOW + ROW_TILE, :] = val

        span = WINDOW + CHUNK
        key_pos = lax.broadcasted_iota(jnp.int32, (1, span), 1)
        chunks = []
        for ci in range(ROW_TILE // CHUNK):
            lo = ci * CHUNK
            valid = (t * ROW_TILE + lo - WINDOW + key_pos) >= 0
            k_wins = (k0_scr[lo:lo + span, :], k1_scr[lo:lo + span, :])
            v_wins = (v0_scr[lo:lo + span, :], v1_scr[lo:lo + span, :])
            chunks.append(_attend_chunk(qs[lo:lo + CHUNK], k_wins, v_wins, valid, sinks_ref))
        o = jnp.concatenate(chunks, axis=0)
        for scr in scrs:
            scr[0:WINDOW, :] = scr[ROW_TILE:ROW_TILE + WINDOW, :]
        out = jnp.dot(o.astype(jnp.bfloat16), wo_ref[...], preferred_element_type=jnp.float32)
        _finish_mixer(x, out, g_ref, b_ref, x1_ref)

    @pl.when(c == SAMPLE_CHUNK)
    def _():
        x = xs_ref[...]
        qs, k, v = _project_qkv(x, wqkv_ref, bqkv_ref, coss_ref, sins_ref)
        knew = _dup_heads(k)
        vnew = _dup_heads(v)
        chunks = []
        for s in range(SAMPLE_SEGS):
            lo = s * DEC_SEQ
            ck = ck_ref[s]
            cv = cv_ref[s]
            kold = _dup_heads(ck)
            vold = _dup_heads(cv)
            k_wins = [jnp.concatenate([kold[g], knew[g][lo:lo + DEC_SEQ]], axis=0) for g in range(N_KV_HEADS)]
            v_wins = [jnp.concatenate([vold[g], vnew[g][lo:lo + DEC_SEQ]], axis=0) for g in range(N_KV_HEADS)]
            chunks.append(_attend_chunk(qs[lo:lo + DEC_SEQ], k_wins, v_wins, None, sinks_ref))
            ks_ref[s, 0:WINDOW - DEC_SEQ, :] = ck[DEC_SEQ:]
            ks_ref[s, WINDOW - DEC_SEQ:WINDOW, :] = k[lo:lo + DEC_SEQ]
            vs_ref[s, 0:WINDOW - DEC_SEQ, :] = cv[DEC_SEQ:]
            vs_ref[s, WINDOW - DEC_SEQ:WINDOW, :] = v[lo:lo + DEC_SEQ]
        o = jnp.concatenate(chunks, axis=0)
        out = jnp.dot(o.astype(jnp.bfloat16), wo_ref[...], preferred_element_type=jnp.float32)
        _finish_mixer(x, out, g_ref, b_ref, x1_ref)


def _swa(xp, xs, xs_chunk, ck, cv, w, cos_p, sin_p, cos_s, sin_s, ln_g, ln_b):
    wqkv, bqkv, sinks, wo = w
    nt_s = DEC_BATCH // SAMPLE_SEGS
    kv_lanes = N_KV_HEADS * HEAD_DIM
    cache_spec = pl.BlockSpec((None, SAMPLE_SEGS, WINDOW, kv_lanes), _PER_SAMPLE4)
    prompt_table = pl.BlockSpec((ROW_TILE, LANES), lambda c, t: (_prompt_tile(c, t), 0))
    kv_prompt = pl.BlockSpec((None, WINDOW, kv_lanes), _PER_PROMPT)
    return pl.pallas_call(
        _swa_kernel,
        grid=_MIXER_GRID,
        in_specs=_mixer_act_specs(xs_chunk) + [
            cache_spec, cache_spec,
            _const_spec((D_MODEL, QKV_DIM)),
            _const_spec((1, QKV_DIM)),
            prompt_table, prompt_table,
            _const_spec((ROW_TILE, LANES)), _const_spec((ROW_TILE, LANES)),
            pl.BlockSpec(memory_space=pltpu.SMEM),
            _const_spec((N_HEADS * HEAD_DIM, D_MODEL)),
            _const_spec((1, D_MODEL)),
            _const_spec((1, D_MODEL)),
        ],
        out_specs=[
            pl.BlockSpec((None, ROW_TILE, D_MODEL), lambda c, t: (c, t, 0)),
            kv_prompt, kv_prompt, cache_spec, cache_spec,
        ],
        out_shape=[
            _ACT_SHAPE,
            jax.ShapeDtypeStruct((BATCH, WINDOW, kv_lanes), jnp.float32),
            jax.ShapeDtypeStruct((BATCH, WINDOW, kv_lanes), jnp.float32),
            jax.ShapeDtypeStruct((nt_s, SAMPLE_SEGS, WINDOW, kv_lanes), jnp.float32),
            jax.ShapeDtypeStruct((nt_s, SAMPLE_SEGS, WINDOW, kv_lanes), jnp.float32),
        ],
        scratch_shapes=[pltpu.VMEM((WINDOW + ROW_TILE, LANES), jnp.bfloat16) for _ in range(4)],
        compiler_params=pltpu.CompilerParams(
            dimension_semantics=("arbitrary", "arbitrary"), vmem_limit_bytes=VMEM_LIMIT),
        name="swa",
    )(xp, xs, ck, cv, wqkv, bqkv, cos_p, sin_p, cos_s, sin_s, sinks, wo, ln_g, ln_b)


META_TILE_EXPERT, META_N_ACTIVE, META_FILL_LO, META_FILL_HI, META_IS_NEW, META_WEIGHT_SLOT = range(6)


def _router_kernel(x_ref, wr_ref, br_ref, pos_ref, gate_ref, meta_ref):
    f32 = jnp.float32
    logits = lax.dot_general(wr_ref[...].astype(jnp.bfloat16), x_ref[...].astype(jnp.bfloat16),
                             (((1,), (1,)), ((), ())), preferred_element_type=f32)
    m = jnp.max(logits, axis=0, keepdims=True)
    e = jnp.exp(logits - m)
    probs = e / jnp.sum(e, axis=0, keepdims=True)
    sel = probs + br_ref[...]
    p = [probs[i:i + 1] for i in range(N_EXPERTS)]
    s = [sel[i:i + 1] for i in range(N_EXPERTS)]

    def first_argmax(vals):
        best, arg = vals[0], jnp.zeros_like(vals[0], dtype=jnp.int32)
        for i in range(1, len(vals)):
            better = vals[i] > best
            best = jnp.where(better, vals[i], best)
            arg = jnp.where(better, i, arg)
        return arg

    group_scores = []
    for gidx in range(N_GROUPS):
        v = s[gidx * EXPERTS_PER_GROUP:(gidx + 1) * EXPERTS_PER_GROUP]
        best = v[0] + v[1]
        for i in range(EXPERTS_PER_GROUP):
            for j in range(i + 1, EXPERTS_PER_GROUP):
                if (i, j) != (0, 1):
                    best = jnp.maximum(best, v[i] + v[j])
        group_scores.append(best)
    best_group = first_argmax(group_scores)
    neg_inf = jnp.full_like(s[0], -jnp.inf)
    masked = [jnp.where(best_group == (i // EXPERTS_PER_GROUP), s[i], neg_inf) for i in range(N_EXPERTS)]
    idx1 = first_argmax(masked)
    masked2 = [jnp.where(idx1 == i, neg_inf, masked[i]) for i in range(N_EXPERTS)]
    idx2 = first_argmax(masked2)
    zero = jnp.zeros_like(p[0])
    w1, w2 = zero, zero
    for i in range(N_EXPERTS):
        w1 = w1 + jnp.where(idx1 == i, p[i], zero)
        w2 = w2 + jnp.where(idx2 == i, p[i], zero)
    tot = w1 + w2
    gate_ref[0:1, :] = w1 / tot
    gate_ref[1:2, :] = w2 / tot

    expert = lax.broadcasted_iota(jnp.int32, (N_EXPERTS, MOE_CHUNK), 0)
    strict_upper = (lax.broadcasted_iota(jnp.int32, (LANES, LANES), 0)
                    < lax.broadcasted_iota(jnp.int32, (LANES, LANES), 1)).astype(jnp.bfloat16)
    carry = jnp.zeros((N_EXPERTS, 1), f32)
    onehots, ranks = [], []
    for idx in (idx1, idx2):
        onehot = (expert == idx).astype(f32)
        before = []
        for blk in range(MOE_CHUNK // LANES):
            oh = onehot[:, blk * LANES:(blk + 1) * LANES]
            before.append(jnp.dot(oh.astype(jnp.bfloat16), strict_upper, preferred_element_type=f32) + carry)
            carry = carry + jnp.sum(oh, axis=1, keepdims=True)
        onehots.append(onehot)
        ranks.append(jnp.sum(onehot * jnp.concatenate(before, axis=1), axis=0, keepdims=True))
    counts = carry
    padded = jnp.floor((counts + (MOE_TM - 1)) * (1.0 / MOE_TM)) * MOE_TM
    strict_lower = (lax.broadcasted_iota(jnp.int32, (N_EXPERTS, N_EXPERTS), 0)
                    > lax.broadcasted_iota(jnp.int32, (N_EXPERTS, N_EXPERTS), 1)).astype(jnp.bfloat16)
    starts = jnp.dot(strict_lower, jnp.broadcast_to(padded, (N_EXPERTS, LANES)).astype(jnp.bfloat16),
                     preferred_element_type=f32)[:, 0:1]
    ends = starts + padded
    for k in range(2):
        pos = jnp.sum(onehots[k] * starts, axis=0, keepdims=True) + ranks[k]
        pos_ref[k:k + 1, :] = pos.astype(jnp.int32)

    lane = lax.broadcasted_iota(jnp.int32, (N_EXPERTS, LANES), 1)
    sub = lax.broadcasted_iota(jnp.int32, (N_EXPERTS, LANES), 0)
    tile_start = (lane * MOE_TM).astype(f32)
    tile_expert = jnp.sum((ends <= tile_start).astype(f32), axis=0, keepdims=True)
    tile_expert = jnp.minimum(tile_expert, N_EXPERTS - 1.0)
    on_diag = sub == lane

    def to_lanes(col):
        return jnp.sum(jnp.where(on_diag, col, 0.0), axis=0, keepdims=True)

    lane1 = lax.broadcasted_iota(jnp.int32, (1, LANES), 1)
    is_new = jnp.where(jnp.logical_or(lane1 == 0, tile_expert != pltpu.roll(tile_expert, 1, 1)), 1.0, 0.0)
    upper_incl = (lax.broadcasted_iota(jnp.int32, (LANES, LANES), 0)
                  <= lax.broadcasted_iota(jnp.int32, (LANES, LANES), 1)).astype(jnp.bfloat16)
    runs = jnp.dot(jnp.broadcast_to(is_new, (SUBLANES, LANES)).astype(jnp.bfloat16), upper_incl,
                   preferred_element_type=f32)[0:1]
    weight_slot = (runs.astype(jnp.int32) - 1) & 1

    n_active = jnp.broadcast_to(ends[N_EXPERTS - 1:N_EXPERTS] * (1.0 / MOE_TM), (1, LANES))
    meta_ref[...] = jnp.zeros(meta_ref.shape, jnp.int32)
    meta_ref[META_IS_NEW:META_IS_NEW + 1, :] = is_new.astype(jnp.int32)
    meta_ref[META_WEIGHT_SLOT:META_WEIGHT_SLOT + 1, :] = weight_slot
    meta_ref[META_TILE_EXPERT:META_TILE_EXPERT + 1, :] = tile_expert.astype(jnp.int32)
    meta_ref[META_N_ACTIVE:META_N_ACTIVE + 1, :] = n_active.astype(jnp.int32)
    meta_ref[META_FILL_LO:META_FILL_LO + 1, :] = to_lanes(starts + counts).astype(jnp.int32)
    meta_ref[META_FILL_HI:META_FILL_HI + 1, :] = to_lanes(ends).astype(jnp.int32)


def _router(x, wr_t, br):
    per_chunk = lambda c: (c, 0, 0)
    return pl.pallas_call(
        _router_kernel,
        grid=(N_CHUNKS,),
        in_specs=[
            pl.BlockSpec((None, MOE_CHUNK, D_MODEL), per_chunk),
            _const_spec((N_EXPERTS, D_MODEL)),
            _const_spec((N_EXPERTS, 1)),
        ],
        out_specs=[
            pl.BlockSpec((None, 2, MOE_CHUNK), per_chunk),
            pl.BlockSpec((None, 2, MOE_CHUNK), per_chunk),
            pl.BlockSpec((None, SUBLANES, LANES), per_chunk),
        ],
        out_shape=[
            jax.ShapeDtypeStruct((N_CHUNKS, 2, MOE_CHUNK), jnp.int32),
            jax.ShapeDtypeStruct((N_CHUNKS, 2, MOE_CHUNK), jnp.float32),
            jax.ShapeDtypeStruct((N_CHUNKS, SUBLANES, LANES), jnp.int32),
        ],
        compiler_params=pltpu.CompilerParams(
            dimension_semantics=("parallel",), vmem_limit_bytes=VMEM_LIMIT),
        name="router",
    )(x, wr_t, br)


def _gather_rows(pair_scr, x_ref, xs_scr, row0, j):
    tok = pair_scr[row0 + j] & (MOE_CHUNK - 1)
    xs_scr[j // SUBLANES, (j % SUBLANES):(j % SUBLANES) + 1, :] = x_ref[pl.ds(tok, 1), :]


def _scatter_rows(pair_scr, y_scr, yo_scr, row0, j):
    y_scr[pl.ds(pair_scr[row0 + j], 1), :] = yo_scr[j // SUBLANES, (j % SUBLANES):(j % SUBLANES) + 1, :]


def _moe_kernel(layer, te_ref, new_ref, slot_ref, nact_ref, flo_ref, fhi_ref, pos_ref, x_ref, gate_ref,
                wg_hbm, wu_hbm, wd_hbm, g_ref, b_ref, out_hbm,
                pair_scr, xs_a, xs_b, yo_a, yo_b, y_scr, wg_buf, wu_buf, wd_buf, w_sem, out_sem):
    c = pl.program_id(0)
    n_act = nact_ref[c]
    t0 = c * MOE_STEPS

    def weight_copies(t):
        e, s = te_ref[t0 + t], slot_ref[t0 + t]
        return (pltpu.make_async_copy(wg_hbm.at[layer, e], wg_buf.at[s], w_sem.at[s, 0]),
                pltpu.make_async_copy(wu_hbm.at[layer, e], wu_buf.at[s], w_sem.at[s, 1]),
                pltpu.make_async_copy(wd_hbm.at[layer, e], wd_buf.at[s], w_sem.at[s, 2]))

    def out_copy(chunk):
        return pltpu.make_async_copy(y_scr.at[pl.ds(0, MOE_CHUNK)], out_hbm.at[chunk], out_sem.at[0])

    for cp in weight_copies(0):
        cp.start()

    def fill8(g, base):
        for u in range(SUBLANES):
            pair_scr[base + g * SUBLANES + u] = JUNK_PAIR
        return base

    def fill_expert(e, carry):
        lo, hi = flo_ref[c * N_EXPERTS + e], fhi_ref[c * N_EXPERTS + e]
        lax.fori_loop(0, (hi - lo + SUBLANES - 1) >> 3, fill8, lo)
        return carry
    lax.fori_loop(0, N_EXPERTS, fill_expert, 0)
    lax.fori_loop(0, MOE_TM // SUBLANES, fill8, fhi_ref[c * N_EXPERTS + N_EXPERTS - 1])

    def invert(t, carry):
        pair_scr[pos_ref[t]] = t
        pair_scr[pos_ref[MOE_CHUNK + t]] = MOE_CHUNK + t
        return carry
    lax.fori_loop(0, MOE_CHUNK, invert, 0, unroll=8)

    def gather0(gi, carry):
        for u in range(SUBLANES):
            tok = pair_scr[gi * SUBLANES + u] & (MOE_CHUNK - 1)
            xs_a[gi, u:u + 1, :] = x_ref[pl.ds(tok, 1), :]
        return carry
    lax.fori_loop(0, MOE_TM // SUBLANES, gather0, 0)

    @pl.when(c == 0)
    def _():
        yo_b[...] = jnp.zeros_like(yo_b)

    @pl.when(c > 0)
    def _():
        out_copy(c - 1).wait()

    def expert_step(t, s, xs_cur, yo_cur, xs_nxt, yo_prv):
        row_nxt = (t + 1) * MOE_TM
        row_prv = jnp.maximum(t - 1, 0) * MOE_TM
        for j in range(MOE_TM):
            _gather_rows(pair_scr, x_ref, xs_nxt, row_nxt, j)
        xb = xs_cur[...].reshape(MOE_TM, D_MODEL).astype(jnp.bfloat16)
        hg = jnp.dot(xb, wg_buf[s], preferred_element_type=jnp.float32)
        hu = jnp.dot(xb, wu_buf[s], preferred_element_type=jnp.float32)
        act = (jax.nn.silu(hg) * hu).astype(jnp.bfloat16)
        yo = jnp.dot(act, wd_buf[s], preferred_element_type=jnp.float32)
        yo_cur[...] = yo.reshape(MOE_TM // SUBLANES, SUBLANES, D_MODEL)
        for j in range(MOE_TM):
            _scatter_rows(pair_scr, y_scr, yo_prv, row_prv, j)

    def tile_iteration(t, carry):
        @pl.when(new_ref[t0 + t] == 1)
        def _():
            for cp in weight_copies(t):
                cp.wait()

        @pl.when(jnp.logical_and(t + 1 < n_act, new_ref[t0 + t + 1] == 1))
        def _():
            for cp in weight_copies(t + 1):
                cp.start()

        for s in range(2):
            in_slot = slot_ref[t0 + t] == s

            @pl.when(jnp.logical_and((t & 1) == 0, in_slot))
            def _():
                expert_step(t, s, xs_a, yo_a, xs_b, yo_b)

            @pl.when(jnp.logical_and((t & 1) == 1, in_slot))
            def _():
                expert_step(t, s, xs_b, yo_b, xs_a, yo_a)
        return carry
    lax.fori_loop(0, n_act, tile_iteration, 0)

    def drain(yo_last):
        row_last = (n_act - 1) * MOE_TM

        def scatter(gi, carry):
            for u in range(SUBLANES):
                y_scr[pl.ds(pair_scr[row_last + gi * SUBLANES + u], 1), :] = yo_last[gi, u:u + 1, :]
            return carry
        lax.fori_loop(0, MOE_TM // SUBLANES, scatter, 0)

    @pl.when((n_act & 1) == 1)
    def _():
        drain(yo_a)

    @pl.when((n_act & 1) == 0)
    def _():
        drain(yo_b)

    def combine(k, carry):
        lo = pl.multiple_of(k * ROW_TILE, ROW_TILE)
        gates = gate_ref[pl.ds(lo, ROW_TILE), :]
        moe = (gates[:, 0:1] * y_scr[pl.ds(lo, ROW_TILE), :]
               + gates[:, 1:2] * y_scr[pl.ds(MOE_CHUNK + lo, ROW_TILE), :])
        x = x_ref[pl.ds(lo, ROW_TILE), :]
        y_scr[pl.ds(lo, ROW_TILE), :] = _layer_norm(DEEPNORM_ALPHA * x + moe, g_ref[...], b_ref[...])
        return carry
    lax.fori_loop(0, COMBINE_STEPS, combine, 0)

    out_copy(c).start()

    @pl.when(c == N_CHUNKS - 1)
    def _():
        out_copy(c).wait()


def _moe(x, plan, gates_t, layer, wg, wu, wd, ln_g, ln_b):
    tile_expert, is_new, weight_slot, n_active, fill_lo, fill_hi, pos = plan
    stage = pltpu.VMEM((MOE_TM // SUBLANES, SUBLANES, D_MODEL), jnp.float32)
    per_chunk = lambda c, *_: (c, 0, 0)
    grid_spec = pltpu.PrefetchScalarGridSpec(
        num_scalar_prefetch=6,
        grid=(N_CHUNKS,),
        in_specs=[
            pl.BlockSpec((None, None, 2 * MOE_CHUNK), per_chunk, memory_space=pltpu.SMEM),
            pl.BlockSpec((None, MOE_CHUNK, D_MODEL), per_chunk),
            pl.BlockSpec((None, MOE_CHUNK, 2), per_chunk),
            pl.BlockSpec(memory_space=pl.ANY),
            pl.BlockSpec(memory_space=pl.ANY),
            pl.BlockSpec(memory_space=pl.ANY),
            pl.BlockSpec((1, D_MODEL), lambda c, *_: (0, 0)),
            pl.BlockSpec((1, D_MODEL), lambda c, *_: (0, 0)),
        ],
        out_specs=pl.BlockSpec(memory_space=pl.ANY),
        scratch_shapes=[
            pltpu.SMEM((MOE_ROWS + MOE_TM + SUBLANES,), jnp.int32),
            stage, stage, stage, stage,
            pltpu.VMEM((2 * MOE_CHUNK + SUBLANES, D_MODEL), jnp.float32),
            pltpu.VMEM((2, D_MODEL, D_EXPERT), jnp.bfloat16),
            pltpu.VMEM((2, D_MODEL, D_EXPERT), jnp.bfloat16),
            pltpu.VMEM((2, D_EXPERT, D_MODEL), jnp.bfloat16),
            pltpu.SemaphoreType.DMA((2, 3)),
            pltpu.SemaphoreType.DMA((1,)),
        ],
    )
    return pl.pallas_call(
        functools.partial(_moe_kernel, layer),
        grid_spec=grid_spec,
        out_shape=_ACT_SHAPE,
        compiler_params=pltpu.CompilerParams(
            dimension_semantics=("arbitrary",), vmem_limit_bytes=VMEM_LIMIT),
        name="moe",
    )(tile_expert, is_new, weight_slot, n_active, fill_lo, fill_hi,
      pos.reshape(N_CHUNKS, 1, 2 * MOE_CHUNK), x, gates_t, wg, wu, wd, ln_g, ln_b)


def _moe_layer(x, wr_t, br, layer, wg, wu, wd, ln_g, ln_b):
    pos, gates, meta = _router(x, wr_t, br)
    per_tile = lambda row: meta[:, row, :MOE_STEPS].reshape(-1)
    per_expert = lambda row: meta[:, row, :N_EXPERTS].reshape(-1)
    plan = (per_tile(META_TILE_EXPERT), per_tile(META_IS_NEW), per_tile(META_WEIGHT_SLOT),
            meta[:, META_N_ACTIVE, 0], per_expert(META_FILL_LO), per_expert(META_FILL_HI), pos)
    return _moe(x, plan, jnp.swapaxes(gates, 1, 2), layer, wg, wu, wd, ln_g, ln_b)


def _rope_tables(pos):
    half = HEAD_DIM // 2
    inv_freq = ROPE_THETA ** (-jnp.arange(half, dtype=jnp.float32) / half)
    ang = pos.astype(jnp.float32)[:, None] * inv_freq[None, :]
    cos = jnp.cos(ang)
    sin = jnp.sin(ang)
    reps = LANES // HEAD_DIM
    cos_t = jnp.tile(jnp.concatenate([cos, cos], axis=1), (1, reps))
    sin_t = jnp.tile(jnp.concatenate([-sin, sin], axis=1), (1, reps))
    return cos_t, sin_t


def _block_diag_tiles(w):
    per = MXU_EDGE // LRU_BLOCK_W
    w4 = w.reshape(LRU_BLOCKS // per, per, LRU_BLOCK_W, LRU_BLOCK_W)
    eye = jnp.eye(per, dtype=w.dtype)
    return jnp.einsum("qaij,ab->qaibj", w4, eye).reshape(LRU_BLOCKS // per, MXU_EDGE, MXU_EDGE)


def kernel(x_prompt, x_sample, state_conv_0, state_h_0, cache_k_1, cache_v_1, state_conv_2, state_h_2,
           cache_k_3, cache_v_3, w_router, b_router, ln_g, ln_b, lru_w_in, lru_conv_w, lru_conv_b,
           lru_w_gates, lru_b_gates, lru_lambda, lru_w_out, attn_w_qkv, attn_b_qkv, attn_sinks, attn_w_o,
           moe_w_gate, moe_w_up, moe_w_down):
    bf = jnp.bfloat16
    in_state = {0: (state_conv_0, state_h_0), 1: (cache_k_1, cache_v_1),
                2: (state_conv_2, state_h_2), 3: (cache_k_3, cache_v_3)}
    nt_s = DEC_BATCH // SAMPLE_SEGS
    kv_lanes = N_KV_HEADS * HEAD_DIM

    wr_t = w_router.T
    br = b_router.reshape(N_EXPERTS, 1)
    cos_p, sin_p = _rope_tables(jnp.arange(SEQ))
    cos_s, sin_s = _rope_tables(PAST_LEN + jnp.arange(DEC_SEQ))
    cos_s = jnp.tile(cos_s, (SAMPLE_SEGS, 1))
    sin_s = jnp.tile(sin_s, (SAMPLE_SEGS, 1))
    wg_all = moe_w_gate.astype(bf)
    wu_all = moe_w_up.astype(bf)
    wd_all = moe_w_down.astype(bf)

    x = x_prompt
    x_s, x_s_chunk = x_sample.reshape(1, MOE_CHUNK, D_MODEL), 0
    new_p, new_s = {}, {}
    for i in range(DEPTH):
        slot = i // 2
        g0 = ln_g[i, 0].reshape(1, D_MODEL)
        b0 = ln_b[i, 0].reshape(1, D_MODEL)
        g1 = ln_g[i, 1].reshape(1, D_MODEL)
        b1 = ln_b[i, 1].reshape(1, D_MODEL)
        st_a, st_b = in_state[i]
        if i % 2 == 0:
            gates = lru_w_gates[slot]
            w = (lru_w_in[slot].astype(bf), lru_conv_w[slot], lru_conv_b[slot].reshape(1, D_RNN),
                 _block_diag_tiles(gates[:, :, :LRU_BLOCK_W]).astype(bf),
                 _block_diag_tiles(gates[:, :, LRU_BLOCK_W:]).astype(bf),
                 lru_b_gates[slot][:, :LRU_BLOCK_W].reshape(1, D_RNN),
                 lru_b_gates[slot][:, LRU_BLOCK_W:].reshape(1, D_RNN),
                 lru_lambda[slot].reshape(1, D_RNN), lru_w_out[slot].astype(bf))
            conv_pad = jnp.pad(st_a, ((0, 0), (SUBLANES - (CONV_WIDTH - 1), 0), (0, 0)))
            x1, conv_p, h_p, conv_s, h_s = _lru(
                x, x_s, x_s_chunk, conv_pad.reshape(nt_s, SAMPLE_SEGS, SUBLANES, D_RNN),
                st_b.reshape(nt_s, SAMPLE_SEGS, D_RNN), w, g0, b0)
            new_p[i] = (conv_p, h_p.reshape(BATCH, D_RNN))
            new_s[i] = (conv_s.reshape(DEC_BATCH, CONV_WIDTH - 1, D_RNN), h_s.reshape(DEC_BATCH, D_RNN))
        else:
            w = (attn_w_qkv[slot].astype(bf), attn_b_qkv[slot].reshape(1, QKV_DIM), attn_sinks[slot],
                 attn_w_o[slot].astype(bf))
            x1, k_p, v_p, k_s, v_s = _swa(
                x, x_s, x_s_chunk, st_a.reshape(nt_s, SAMPLE_SEGS, WINDOW, kv_lanes),
                st_b.reshape(nt_s, SAMPLE_SEGS, WINDOW, kv_lanes), w, cos_p, sin_p, cos_s, sin_s, g0, b0)
            new_p[i] = (k_p.reshape(BATCH, WINDOW, N_KV_HEADS, HEAD_DIM),
                        v_p.reshape(BATCH, WINDOW, N_KV_HEADS, HEAD_DIM))
            new_s[i] = (k_s.reshape(DEC_BATCH, WINDOW, N_KV_HEADS, HEAD_DIM),
                        v_s.reshape(DEC_BATCH, WINDOW, N_KV_HEADS, HEAD_DIM))
        x = _moe_layer(x1, wr_t, br, i, wg_all, wu_all, wd_all, g1, b1)
        x_s, x_s_chunk = x, SAMPLE_CHUNK
    return (x[:BATCH], x[SAMPLE_CHUNK].reshape(DEC_BATCH, DEC_SEQ, D_MODEL),
            new_p[0][0], new_s[0][0], new_p[0][1], new_s[0][1],
            new_p[1][0], new_s[1][0], new_p[1][1], new_s[1][1],
            new_p[2][0], new_s[2][0], new_p[2][1], new_s[2][1],
            new_p[3][0], new_s[3][0], new_p[3][1], new_s[3][1])
```

```python
import functools

import jax
import jax.numpy as jnp
from jax import lax
from jax.experimental import pallas as pl
from jax.experimental.pallas import tpu as pltpu

D_MODEL = 1024
BATCH = 8
SEQ = 2048
DEPTH = 4
DEC_BATCH = 32
DEC_SEQ = 64
PAST_LEN = 4096
CHUNK = 64
D_RNN = D_MODEL
CONV_WIDTH = 4
LRU_BLOCKS = 16
LRU_BLOCK_W = D_RNN // LRU_BLOCKS
LRU_C = 8.0
N_HEADS = 16
N_KV_HEADS = 2
HEAD_DIM = 64
KV_GROUP = N_HEADS // N_KV_HEADS
WINDOW = 128
ROPE_THETA = 10000.0
QKV_DIM = (N_HEADS + 2 * N_KV_HEADS) * HEAD_DIM
N_EXPERTS = 16
N_GROUPS = 4
EXPERTS_PER_GROUP = N_EXPERTS // N_GROUPS
D_EXPERT = 512
DEEPNORM_ALPHA = (2.0 * DEPTH) ** 0.25
LN_EPS = 1e-5

MXU_EDGE = 256
LANES = 128
SUBLANES = 8

ROW_TILE = 256
MOE_CHUNK = SEQ
N_CHUNKS = BATCH + DEC_BATCH * DEC_SEQ // MOE_CHUNK
SAMPLE_CHUNK = BATCH
MOE_TM = 256
MOE_ROWS = 2 * MOE_CHUNK + N_EXPERTS * MOE_TM
MOE_TILES = MOE_ROWS // MOE_TM
MOE_STEPS = MOE_TILES + 1
COMBINE_STEPS = MOE_CHUNK // ROW_TILE
JUNK_PAIR = 2 * MOE_CHUNK
SAMPLE_SEGS = ROW_TILE // DEC_SEQ
NEG_BIG = -1e30
VMEM_LIMIT = 52 * 1024 * 1024

assert DEC_BATCH * DEC_SEQ == MOE_CHUNK and MOE_TILES <= LANES


def _layer_norm(x, g, b):
    mu = jnp.mean(x, -1, keepdims=True)
    xc = x - mu
    var = jnp.mean(xc * xc, -1, keepdims=True)
    return xc * lax.rsqrt(var + LN_EPS) * g + b


def _finish_mixer(x, y, g_ref, b_ref, x1_ref):
    x1_ref[...] = _layer_norm(DEEPNORM_ALPHA * x + y, g_ref[...], b_ref[...])


def _const_spec(shape):
    nd = len(shape)
    return pl.BlockSpec(shape, lambda *_: (0,) * nd)


_ACT_SHAPE = jax.ShapeDtypeStruct((N_CHUNKS, MOE_CHUNK, D_MODEL), jnp.float32)


def _lru_tile(x, ext_scr, h_in, n_seg, seg, win_ref, cw_ref, cb_ref, wr_ref, wi_ref,
              br_ref, bi_ref, lam_ref, wout_ref):
    rows = n_seg * seg
    gr = jnp.dot(x.astype(jnp.bfloat16), win_ref[...], preferred_element_type=jnp.float32)
    gate_in = gr[:, :D_RNN]
    rec = gr[:, D_RNN:]
    cw = cw_ref[...]
    xcs = []
    for s in range(n_seg):
        ext_scr[s, SUBLANES:SUBLANES + seg, :] = rec[s * seg:(s + 1) * seg]
        xc = cb_ref[...] + cw[3:4] * rec[s * seg:(s + 1) * seg]
        for k in range(CONV_WIDTH - 1):
            off = SUBLANES - (CONV_WIDTH - 1) + k
            xc = xc + cw[k:k + 1] * ext_scr[s, off:off + seg, :]
        xcs.append(xc)
    xc = xcs[0] if n_seg == 1 else jnp.concatenate(xcs, axis=0)

    xcb = xc.astype(jnp.bfloat16)
    r_parts, i_parts = [], []
    for q in range(D_RNN // MXU_EDGE):
        blk = xcb[:, q * MXU_EDGE:(q + 1) * MXU_EDGE]
        r_parts.append(jnp.dot(blk, wr_ref[q], preferred_element_type=jnp.float32))
        i_parts.append(jnp.dot(blk, wi_ref[q], preferred_element_type=jnp.float32))
    r = jax.nn.sigmoid(jnp.concatenate(r_parts, axis=1) + br_ref[...])
    ig = jax.nn.sigmoid(jnp.concatenate(i_parts, axis=1) + bi_ref[...])

    z = -lam_ref[...]
    softplus = jnp.maximum(z, 0.0) + jnp.log1p(jnp.exp(-jnp.abs(z)))
    log_a = (-LRU_C * r) * softplus
    a = jnp.exp(log_a)
    b = jnp.sqrt(-jnp.tanh(log_a) * (a * a + 1.0)) * (ig * xc)

    sub = lax.broadcasted_iota(jnp.int32, (rows, D_RNN), 0) & (SUBLANES - 1)
    step = 1
    while step < SUBLANES:
        keep = sub >= step
        a_prev = pltpu.roll(a, step, 0)
        b_prev = pltpu.roll(b, step, 0)
        b = jnp.where(keep, a * b_prev + b, b)
        a = jnp.where(keep, a * a_prev, a)
        step *= 2
    groups_per_seg = seg // SUBLANES
    h_blocks, h_last = [], []
    carry = None
    for gi in range(rows // SUBLANES):
        if gi % groups_per_seg == 0:
            carry = h_in[gi // groups_per_seg]
        lo = gi * SUBLANES
        hb = b[lo:lo + SUBLANES] + a[lo:lo + SUBLANES] * carry
        carry = hb[SUBLANES - 1:SUBLANES]
        h_blocks.append(hb)
        if gi % groups_per_seg == groups_per_seg - 1:
            h_last.append(carry)
    h = jnp.concatenate(h_blocks, axis=0)

    y = h * jax.nn.gelu(gate_in, approximate=True)
    out = jnp.dot(y.astype(jnp.bfloat16), wout_ref[...], preferred_element_type=jnp.float32)
    return out, rec, h_last


def _lru_kernel(xp_ref, xs_ref, conv_in_ref, h0_ref, win_ref, cw_ref, cb_ref, wr_ref, wi_ref, br_ref,
                bi_ref, lam_ref, wout_ref, g_ref, b_ref, x1_ref, conv_p_ref, h_p_ref, conv_s_ref, h_s_ref,
                ext_p_scr, h_scr, ext_s_scr):
    c = pl.program_id(0)
    t = pl.program_id(1)
    weights = (win_ref, cw_ref, cb_ref, wr_ref, wi_ref, br_ref, bi_ref, lam_ref, wout_ref)

    @pl.when(c < BATCH)
    def _():
        @pl.when(t == 0)
        def _():
            ext_p_scr[0, 0:SUBLANES, :] = jnp.zeros((SUBLANES, D_RNN), jnp.float32)
            h_scr[...] = jnp.zeros_like(h_scr)

        x = xp_ref[...]
        out, rec, h_last = _lru_tile(x, ext_p_scr, [h_scr[...]], 1, ROW_TILE, *weights)
        ext_p_scr[0, 0:SUBLANES, :] = rec[ROW_TILE - SUBLANES:]
        h_scr[...] = h_last[0]
        conv_p_ref[...] = rec[ROW_TILE - (CONV_WIDTH - 1):]
        h_p_ref[...] = h_last[0]
        _finish_mixer(x, out, g_ref, b_ref, x1_ref)

    @pl.when(c == SAMPLE_CHUNK)
    def _():
        for s in range(SAMPLE_SEGS):
            ext_s_scr[s, 0:SUBLANES, :] = conv_in_ref[s]
        h_in = [h0_ref[s:s + 1, :] for s in range(SAMPLE_SEGS)]
        x = xs_ref[...]
        out, rec, h_last = _lru_tile(x, ext_s_scr, h_in, SAMPLE_SEGS, DEC_SEQ, *weights)
        for s in range(SAMPLE_SEGS):
            end = (s + 1) * DEC_SEQ
            conv_s_ref[s] = rec[end - (CONV_WIDTH - 1):end]
            h_s_ref[s:s + 1, :] = h_last[s]
        _finish_mixer(x, out, g_ref, b_ref, x1_ref)


_TILES_PER_CHUNK = MOE_CHUNK // ROW_TILE


def _prompt_tile(c, t):
    return jnp.where(c < BATCH, t, _TILES_PER_CHUNK - 1)


def _sample_tile(c, t):
    return jnp.where(c == SAMPLE_CHUNK, t, 0)


def _mixer_act_specs():
    return [
        pl.BlockSpec((None, ROW_TILE, D_MODEL), lambda c, t: (jnp.minimum(c, BATCH - 1), _prompt_tile(c, t), 0)),
        pl.BlockSpec((None, ROW_TILE, D_MODEL), lambda c, t: (0, _sample_tile(c, t), 0)),
    ]


_MIXER_GRID = (N_CHUNKS, _TILES_PER_CHUNK)
_PER_PROMPT = lambda c, t: (jnp.minimum(c, BATCH - 1), 0, 0)
_PER_SAMPLE3 = lambda c, t: (_sample_tile(c, t), 0, 0)
_PER_SAMPLE4 = lambda c, t: (_sample_tile(c, t), 0, 0, 0)


def _lru(xp, xs, conv_pad, h0, w, ln_g, ln_b):
    nt_s = DEC_BATCH // SAMPLE_SEGS
    return pl.pallas_call(
        _lru_kernel,
        grid=_MIXER_GRID,
        in_specs=_mixer_act_specs() + [
            pl.BlockSpec((None, SAMPLE_SEGS, SUBLANES, D_RNN), _PER_SAMPLE4),
            pl.BlockSpec((None, SAMPLE_SEGS, D_RNN), _PER_SAMPLE3),
            _const_spec((D_MODEL, 2 * D_RNN)),
            _const_spec((CONV_WIDTH, D_RNN)),
            _const_spec((1, D_RNN)),
            _const_spec((D_RNN // MXU_EDGE, MXU_EDGE, MXU_EDGE)),
            _const_spec((D_RNN // MXU_EDGE, MXU_EDGE, MXU_EDGE)),
            _const_spec((1, D_RNN)),
            _const_spec((1, D_RNN)),
            _const_spec((1, D_RNN)),
            _const_spec((D_RNN, D_MODEL)),
            _const_spec((1, D_MODEL)),
            _const_spec((1, D_MODEL)),
        ],
        out_specs=[
            pl.BlockSpec((None, ROW_TILE, D_MODEL), lambda c, t: (c, t, 0)),
            pl.BlockSpec((None, CONV_WIDTH - 1, D_RNN), _PER_PROMPT),
            pl.BlockSpec((None, 1, D_RNN), _PER_PROMPT),
            pl.BlockSpec((None, SAMPLE_SEGS, CONV_WIDTH - 1, D_RNN), _PER_SAMPLE4),
            pl.BlockSpec((None, SAMPLE_SEGS, D_RNN), _PER_SAMPLE3),
        ],
        out_shape=[
            _ACT_SHAPE,
            jax.ShapeDtypeStruct((BATCH, CONV_WIDTH - 1, D_RNN), jnp.float32),
            jax.ShapeDtypeStruct((BATCH, 1, D_RNN), jnp.float32),
            jax.ShapeDtypeStruct((nt_s, SAMPLE_SEGS, CONV_WIDTH - 1, D_RNN), jnp.float32),
            jax.ShapeDtypeStruct((nt_s, SAMPLE_SEGS, D_RNN), jnp.float32),
        ],
        scratch_shapes=[
            pltpu.VMEM((1, SUBLANES + ROW_TILE, D_RNN), jnp.float32),
            pltpu.VMEM((1, D_RNN), jnp.float32),
            pltpu.VMEM((SAMPLE_SEGS, SUBLANES + DEC_SEQ, D_RNN), jnp.float32),
        ],
        compiler_params=pltpu.CompilerParams(
            dimension_semantics=("arbitrary", "arbitrary"), vmem_limit_bytes=VMEM_LIMIT),
        name="lru",
    )(xp, xs, conv_pad, h0, *w, ln_g, ln_b)


def _rope(x, cos, sin_signed):
    lane = lax.broadcasted_iota(jnp.int32, (x.shape[0], LANES), 1)
    first_half = (lane & (HEAD_DIM - 1)) < HEAD_DIM // 2
    cols = []
    for c in range(x.shape[1] // LANES):
        xc = x[:, c * LANES:(c + 1) * LANES]
        partner = jnp.where(first_half, pltpu.roll(xc, LANES - HEAD_DIM // 2, 1),
                            pltpu.roll(xc, HEAD_DIM // 2, 1))
        cols.append(xc * cos + partner * sin_signed)
    return cols[0] if len(cols) == 1 else jnp.concatenate(cols, axis=1)


def _dup_heads(kv):
    lane = lax.broadcasted_iota(jnp.int32, kv.shape, 1)
    low = lane < HEAD_DIM
    swapped = pltpu.roll(kv, HEAD_DIM, 1)
    return (jnp.where(low, kv, swapped).astype(jnp.bfloat16),
            jnp.where(low, swapped, kv).astype(jnp.bfloat16))


def _attend_chunk(q, k_wins, v_wins, valid, sinks_ref):
    lane = lax.broadcasted_iota(jnp.int32, (CHUNK, LANES), 1)
    low = lane < HEAD_DIM
    zero = jnp.zeros((CHUNK, LANES), jnp.bfloat16)
    cols_per_group = KV_GROUP * HEAD_DIM // LANES
    out_cols = []
    for g in range(N_KV_HEADS):
        stacked = []
        for c in range(g * cols_per_group, (g + 1) * cols_per_group):
            qc = q[:, c * LANES:(c + 1) * LANES]
            stacked.append(jnp.where(low, qc, zero))
            stacked.append(jnp.where(low, zero, qc))
        qst = jnp.concatenate(stacked, axis=0)
        s = lax.dot_general(qst, k_wins[g], (((1,), (1,)), ((), ())),
                            preferred_element_type=jnp.float32)
        if valid is not None:
            s = jnp.where(valid, s, NEG_BIG)
        probs = []
        for hh in range(KV_GROUP):
            sh = s[hh * CHUNK:(hh + 1) * CHUNK]
            sink = sinks_ref[g * KV_GROUP + hh]
            m = jnp.maximum(jnp.max(sh, -1, keepdims=True), sink)
            p = jnp.exp(sh - m)
            denom = jnp.sum(p, -1, keepdims=True) + jnp.exp(sink - m)
            probs.append((p * (1.0 / denom)).astype(jnp.bfloat16))
        pst = jnp.concatenate(probs, axis=0)
        o = jnp.dot(pst, v_wins[g], preferred_element_type=jnp.float32)
        for j in range(cols_per_group):
            oa = o[(2 * j) * CHUNK:(2 * j + 1) * CHUNK]
            ob = o[(2 * j + 1) * CHUNK:(2 * j + 2) * CHUNK]
            out_cols.append(jnp.where(low, oa, ob))
    return jnp.concatenate(out_cols, axis=1)


def _project_qkv(x, wqkv_ref, bqkv_ref, cos_ref, sin_ref):
    qkv = jnp.dot(x.astype(jnp.bfloat16), wqkv_ref[...], preferred_element_type=jnp.float32)
    qkv = qkv + bqkv_ref[...]
    nq = N_HEADS * HEAD_DIM
    nk = N_KV_HEADS * HEAD_DIM
    cos = cos_ref[...]
    sin = sin_ref[...]
    q = _rope(qkv[:, :nq], cos, sin)
    k = _rope(qkv[:, nq:nq + nk], cos, sin)
    v = qkv[:, nq + nk:]
    qs = (q * (HEAD_DIM ** -0.5)).astype(jnp.bfloat16)
    return qs, k, v


def _swa_kernel(xp_ref, xs_ref, ck_ref, cv_ref, wqkv_ref, bqkv_ref, cosp_ref, sinp_ref, coss_ref, sins_ref,
                sinks_ref, wo_ref, g_ref, b_ref, x1_ref, kp_ref, vp_ref, ks_ref, vs_ref,
                k0_scr, k1_scr, v0_scr, v1_scr):
    c = pl.program_id(0)
    t = pl.program_id(1)
    scrs = (k0_scr, k1_scr, v0_scr, v1_scr)

    @pl.when(c < BATCH)
    def _():
        @pl.when(t == 0)
        def _():
            for scr in scrs:
                scr[0:WINDOW, :] = jnp.zeros((WINDOW, LANES), jnp.bfloat16)

        x = xp_ref[...]
        qs, k, v = _project_qkv(x, wqkv_ref, bqkv_ref, cosp_ref, sinp_ref)
        kp_ref[...] = k[ROW_TILE - WINDOW:]
        vp_ref[...] = v[ROW_TILE - WINDOW:]
        k0, k1 = _dup_heads(k)
        v0, v1 = _dup_heads(v)
        for scr, val in zip(scrs, (k0, k1, v0, v1)):
            scr[WINDOW:WINDOW + ROW_TILE, :] = val

        span = WINDOW + CHUNK
        key_pos = lax.broadcasted_iota(jnp.int32, (1, span), 1)
        chunks = []
        for ci in range(ROW_TILE // CHUNK):
            lo = ci * CHUNK
            valid = (t * ROW_TILE + lo - WINDOW + key_pos) >= 0
            k_wins = (k0_scr[lo:lo + span, :], k1_scr[lo:lo + span, :])
            v_wins = (v0_scr[lo:lo + span, :], v1_scr[lo:lo + span, :])
            chunks.append(_attend_chunk(qs[lo:lo + CHUNK], k_wins, v_wins, valid, sinks_ref))
        o = jnp.concatenate(chunks, axis=0)
        for scr in scrs:
            scr[0:WINDOW, :] = scr[ROW_TILE:ROW_TILE + WINDOW, :]
        out = jnp.dot(o.astype(jnp.bfloat16), wo_ref[...], preferred_element_type=jnp.float32)
        _finish_mixer(x, out, g_ref, b_ref, x1_ref)

    @pl.when(c == SAMPLE_CHUNK)
    def _():
        x = xs_ref[...]
        qs, k, v = _project_qkv(x, wqkv_ref, bqkv_ref, coss_ref, sins_ref)
        knew = _dup_heads(k)
        vnew = _dup_heads(v)
        chunks = []
        for s in range(SAMPLE_SEGS):
            lo = s * DEC_SEQ
            ck = ck_ref[s]
            cv = cv_ref[s]
            kold = _dup_heads(ck)
            vold = _dup_heads(cv)
            k_wins = [jnp.concatenate([kold[g], knew[g][lo:lo + DEC_SEQ]], axis=0) for g in range(N_KV_HEADS)]
            v_wins = [jnp.concatenate([vold[g], vnew[g][lo:lo + DEC_SEQ]], axis=0) for g in range(N_KV_HEADS)]
            chunks.append(_attend_chunk(qs[lo:lo + DEC_SEQ], k_wins, v_wins, None, sinks_ref))
            ks_ref[s, 0:WINDOW - DEC_SEQ, :] = ck[DEC_SEQ:]
            ks_ref[s, WINDOW - DEC_SEQ:WINDOW, :] = k[lo:lo + DEC_SEQ]
            vs_ref[s, 0:WINDOW - DEC_SEQ, :] = cv[DEC_SEQ:]
            vs_ref[s, WINDOW - DEC_SEQ:WINDOW, :] = v[lo:lo + DEC_SEQ]
        o = jnp.concatenate(chunks, axis=0)
        out = jnp.dot(o.astype(jnp.bfloat16), wo_ref[...], preferred_element_type=jnp.float32)
        _finish_mixer(x, out, g_ref, b_ref, x1_ref)


def _swa(xp, xs, ck, cv, w, cos_p, sin_p, cos_s, sin_s, ln_g, ln_b):
    wqkv, bqkv, sinks, wo = w
    nt_s = DEC_BATCH // SAMPLE_SEGS
    kv_lanes = N_KV_HEADS * HEAD_DIM
    cache_spec = pl.BlockSpec((None, SAMPLE_SEGS, WINDOW, kv_lanes), _PER_SAMPLE4)
    prompt_table = pl.BlockSpec((ROW_TILE, LANES), lambda c, t: (_prompt_tile(c, t), 0))
    kv_prompt = pl.BlockSpec((None, WINDOW, kv_lanes), _PER_PROMPT)
    return pl.pallas_call(
        _swa_kernel,
        grid=_MIXER_GRID,
        in_specs=_mixer_act_specs() + [
            cache_spec, cache_spec,
            _const_spec((D_MODEL, QKV_DIM)),
            _const_spec((1, QKV_DIM)),
            prompt_table, prompt_table,
            _const_spec((ROW_TILE, LANES)), _const_spec((ROW_TILE, LANES)),
            pl.BlockSpec(memory_space=pltpu.SMEM),
            _const_spec((N_HEADS * HEAD_DIM, D_MODEL)),
            _const_spec((1, D_MODEL)),
            _const_spec((1, D_MODEL)),
        ],
        out_specs=[
            pl.BlockSpec((None, ROW_TILE, D_MODEL), lambda c, t: (c, t, 0)),
            kv_prompt, kv_prompt, cache_spec, cache_spec,
        ],
        out_shape=[
            _ACT_SHAPE,
            jax.ShapeDtypeStruct((BATCH, WINDOW, kv_lanes), jnp.float32),
            jax.ShapeDtypeStruct((BATCH, WINDOW, kv_lanes), jnp.float32),
            jax.ShapeDtypeStruct((nt_s, SAMPLE_SEGS, WINDOW, kv_lanes), jnp.float32),
            jax.ShapeDtypeStruct((nt_s, SAMPLE_SEGS, WINDOW, kv_lanes), jnp.float32),
        ],
        scratch_shapes=[pltpu.VMEM((WINDOW + ROW_TILE, LANES), jnp.bfloat16) for _ in range(4)],
        compiler_params=pltpu.CompilerParams(
            dimension_semantics=("arbitrary", "arbitrary"), vmem_limit_bytes=VMEM_LIMIT),
        name="swa",
    )(xp, xs, ck, cv, wqkv, bqkv, cos_p, sin_p, cos_s, sin_s, sinks, wo, ln_g, ln_b)


META_TILE_EXPERT, META_N_ACTIVE, META_FILL_LO, META_FILL_HI, META_IS_NEW, META_WEIGHT_SLOT = range(6)


def _router_kernel(x_ref, wr_ref, br_ref, pos_ref, gate_ref, meta_ref):
    f32 = jnp.float32
    logits = lax.dot_general(wr_ref[...].astype(jnp.bfloat16), x_ref[...].astype(jnp.bfloat16),
                             (((1,), (1,)), ((), ())), preferred_element_type=f32)
    m = jnp.max(logits, axis=0, keepdims=True)
    e = jnp.exp(logits - m)
    probs = e / jnp.sum(e, axis=0, keepdims=True)
    sel = probs + br_ref[...]
    p = [probs[i:i + 1] for i in range(N_EXPERTS)]
    s = [sel[i:i + 1] for i in range(N_EXPERTS)]

    def first_argmax(vals):
        best, arg = vals[0], jnp.zeros_like(vals[0], dtype=jnp.int32)
        for i in range(1, len(vals)):
            better = vals[i] > best
            best = jnp.where(better, vals[i], best)
            arg = jnp.where(better, i, arg)
        return arg

    group_scores = []
    for gidx in range(N_GROUPS):
        v = s[gidx * EXPERTS_PER_GROUP:(gidx + 1) * EXPERTS_PER_GROUP]
        best = v[0] + v[1]
        for i in range(EXPERTS_PER_GROUP):
            for j in range(i + 1, EXPERTS_PER_GROUP):
                if (i, j) != (0, 1):
                    best = jnp.maximum(best, v[i] + v[j])
        group_scores.append(best)
    best_group = first_argmax(group_scores)
    neg_inf = jnp.full_like(s[0], -jnp.inf)
    masked = [jnp.where(best_group == (i // EXPERTS_PER_GROUP), s[i], neg_inf) for i in range(N_EXPERTS)]
    idx1 = first_argmax(masked)
    masked2 = [jnp.where(idx1 == i, neg_inf, masked[i]) for i in range(N_EXPERTS)]
    idx2 = first_argmax(masked2)
    zero = jnp.zeros_like(p[0])
    w1, w2 = zero, zero
    for i in range(N_EXPERTS):
        w1 = w1 + jnp.where(idx1 == i, p[i], zero)
        w2 = w2 + jnp.where(idx2 == i, p[i], zero)
    tot = w1 + w2
    gate_ref[0:1, :] = w1 / tot
    gate_ref[1:2, :] = w2 / tot

    expert = lax.broadcasted_iota(jnp.int32, (N_EXPERTS, MOE_CHUNK), 0)
    strict_upper = (lax.broadcasted_iota(jnp.int32, (LANES, LANES), 0)
                    < lax.broadcasted_iota(jnp.int32, (LANES, LANES), 1)).astype(jnp.bfloat16)
    carry = jnp.zeros((N_EXPERTS, 1), f32)
    onehots, ranks = [], []
    for idx in (idx1, idx2):
        onehot = (expert == idx).astype(f32)
        before = []
        for blk in range(MOE_CHUNK // LANES):
            oh = onehot[:, blk * LANES:(blk + 1) * LANES]
            before.append(jnp.dot(oh.astype(jnp.bfloat16), strict_upper, preferred_element_type=f32) + carry)
            carry = carry + jnp.sum(oh, axis=1, keepdims=True)
        onehots.append(onehot)
        ranks.append(jnp.sum(onehot * jnp.concatenate(before, axis=1), axis=0, keepdims=True))
    counts = carry
    padded = jnp.floor((counts + (MOE_TM - 1)) * (1.0 / MOE_TM)) * MOE_TM
    strict_lower = (lax.broadcasted_iota(jnp.int32, (N_EXPERTS, N_EXPERTS), 0)
                    > lax.broadcasted_iota(jnp.int32, (N_EXPERTS, N_EXPERTS), 1)).astype(jnp.bfloat16)
    starts = jnp.dot(strict_lower, jnp.broadcast_to(padded, (N_EXPERTS, LANES)).astype(jnp.bfloat16),
                     preferred_element_type=f32)[:, 0:1]
    ends = starts + padded
    for k in range(2):
        pos = jnp.sum(onehots[k] * starts, axis=0, keepdims=True) + ranks[k]
        pos_ref[k:k + 1, :] = pos.astype(jnp.int32)

    lane = lax.broadcasted_iota(jnp.int32, (N_EXPERTS, LANES), 1)
    sub = lax.broadcasted_iota(jnp.int32, (N_EXPERTS, LANES), 0)
    tile_start = (lane * MOE_TM).astype(f32)
    tile_expert = jnp.sum((ends <= tile_start).astype(f32), axis=0, keepdims=True)
    tile_expert = jnp.minimum(tile_expert, N_EXPERTS - 1.0)
    on_diag = sub == lane

    def to_lanes(col):
        return jnp.sum(jnp.where(on_diag, col, 0.0), axis=0, keepdims=True)

    lane1 = lax.broadcasted_iota(jnp.int32, (1, LANES), 1)
    is_new = jnp.where(jnp.logical_or(lane1 == 0, tile_expert != pltpu.roll(tile_expert, 1, 1)), 1.0, 0.0)
    upper_incl = (lax.broadcasted_iota(jnp.int32, (LANES, LANES), 0)
                  <= lax.broadcasted_iota(jnp.int32, (LANES, LANES), 1)).astype(jnp.bfloat16)
    runs = jnp.dot(jnp.broadcast_to(is_new, (SUBLANES, LANES)).astype(jnp.bfloat16), upper_incl,
                   preferred_element_type=f32)[0:1]
    weight_slot = (runs.astype(jnp.int32) - 1) & 1

    n_active = jnp.broadcast_to(ends[N_EXPERTS - 1:N_EXPERTS] * (1.0 / MOE_TM), (1, LANES))
    meta_ref[...] = jnp.zeros(meta_ref.shape, jnp.int32)
    meta_ref[META_IS_NEW:META_IS_NEW + 1, :] = is_new.astype(jnp.int32)
    meta_ref[META_WEIGHT_SLOT:META_WEIGHT_SLOT + 1, :] = weight_slot
    meta_ref[META_TILE_EXPERT:META_TILE_EXPERT + 1, :] = tile_expert.astype(jnp.int32)
    meta_ref[META_N_ACTIVE:META_N_ACTIVE + 1, :] = n_active.astype(jnp.int32)
    meta_ref[META_FILL_LO:META_FILL_LO + 1, :] = to_lanes(starts + counts).astype(jnp.int32)
    meta_ref[META_FILL_HI:META_FILL_HI + 1, :] = to_lanes(ends).astype(jnp.int32)


def _router(x, wr_t, br):
    per_chunk = lambda c: (c, 0, 0)
    return pl.pallas_call(
        _router_kernel,
        grid=(N_CHUNKS,),
        in_specs=[
            pl.BlockSpec((None, MOE_CHUNK, D_MODEL), per_chunk),
            _const_spec((N_EXPERTS, D_MODEL)),
            _const_spec((N_EXPERTS, 1)),
        ],
        out_specs=[
            pl.BlockSpec((None, 2, MOE_CHUNK), per_chunk),
            pl.BlockSpec((None, 2, MOE_CHUNK), per_chunk),
            pl.BlockSpec((None, SUBLANES, LANES), per_chunk),
        ],
        out_shape=[
            jax.ShapeDtypeStruct((N_CHUNKS, 2, MOE_CHUNK), jnp.int32),
            jax.ShapeDtypeStruct((N_CHUNKS, 2, MOE_CHUNK), jnp.float32),
            jax.ShapeDtypeStruct((N_CHUNKS, SUBLANES, LANES), jnp.int32),
        ],
        compiler_params=pltpu.CompilerParams(
            dimension_semantics=("parallel",), vmem_limit_bytes=VMEM_LIMIT),
        name="router",
    )(x, wr_t, br)


def _gather_rows(pair_scr, x_ref, xs_scr, row0, j):
    tok = pair_scr[row0 + j] & (MOE_CHUNK - 1)
    xs_scr[j // SUBLANES, (j % SUBLANES):(j % SUBLANES) + 1, :] = x_ref[pl.ds(tok, 1), :]


def _scatter_rows(pair_scr, y_scr, yo_scr, row0, j):
    y_scr[pl.ds(pair_scr[row0 + j], 1), :] = yo_scr[j // SUBLANES, (j % SUBLANES):(j % SUBLANES) + 1, :]


def _moe_kernel(layer, te_ref, new_ref, slot_ref, nact_ref, flo_ref, fhi_ref, pos_ref, x_ref, gate_ref,
                wg_hbm, wu_hbm, wd_hbm, g_ref, b_ref, outp_hbm, outs_hbm,
                pair_scr, xs_a, xs_b, yo_a, yo_b, y_scr, wg_buf, wu_buf, wd_buf, w_sem, out_sem):
    c = pl.program_id(0)
    n_act = nact_ref[c]
    t0 = c * MOE_STEPS

    def weight_copies(t):
        e, s = te_ref[t0 + t], slot_ref[t0 + t]
        return (pltpu.make_async_copy(wg_hbm.at[layer, e], wg_buf.at[s], w_sem.at[s, 0]),
                pltpu.make_async_copy(wu_hbm.at[layer, e], wu_buf.at[s], w_sem.at[s, 1]),
                pltpu.make_async_copy(wd_hbm.at[layer, e], wd_buf.at[s], w_sem.at[s, 2]))

    def prompt_out_copy(chunk):
        return pltpu.make_async_copy(y_scr.at[pl.ds(0, MOE_CHUNK)], outp_hbm.at[chunk], out_sem.at[0])

    def sample_out_copy():
        return pltpu.make_async_copy(y_scr.at[pl.ds(0, MOE_CHUNK)], outs_hbm.at[0], out_sem.at[0])

    for cp in weight_copies(0):
        cp.start()

    def fill8(g, base):
        for u in range(SUBLANES):
            pair_scr[base + g * SUBLANES + u] = JUNK_PAIR
        return base

    def fill_expert(e, carry):
        lo, hi = flo_ref[c * N_EXPERTS + e], fhi_ref[c * N_EXPERTS + e]
        lax.fori_loop(0, (hi - lo + SUBLANES - 1) >> 3, fill8, lo)
        return carry
    lax.fori_loop(0, N_EXPERTS, fill_expert, 0)
    lax.fori_loop(0, MOE_TM // SUBLANES, fill8, fhi_ref[c * N_EXPERTS + N_EXPERTS - 1])

    def invert(t, carry):
        pair_scr[pos_ref[t]] = t
        pair_scr[pos_ref[MOE_CHUNK + t]] = MOE_CHUNK + t
        return carry
    lax.fori_loop(0, MOE_CHUNK, invert, 0, unroll=8)

    def gather0(gi, carry):
        for u in range(SUBLANES):
            tok = pair_scr[gi * SUBLANES + u] & (MOE_CHUNK - 1)
            xs_a[gi, u:u + 1, :] = x_ref[pl.ds(tok, 1), :]
        return carry
    lax.fori_loop(0, MOE_TM // SUBLANES, gather0, 0)

    @pl.when(c == 0)
    def _():
        yo_b[...] = jnp.zeros_like(yo_b)

    @pl.when(c > 0)
    def _():
        prompt_out_copy(c - 1).wait()

    def expert_step(t, s, xs_cur, yo_cur, xs_nxt, yo_prv):
        row_nxt = (t + 1) * MOE_TM
        row_prv = jnp.maximum(t - 1, 0) * MOE_TM
        for j in range(MOE_TM):
            _gather_rows(pair_scr, x_ref, xs_nxt, row_nxt, j)
        xb = xs_cur[...].reshape(MOE_TM, D_MODEL).astype(jnp.bfloat16)
        hg = jnp.dot(xb, wg_buf[s], preferred_element_type=jnp.float32)
        hu = jnp.dot(xb, wu_buf[s], preferred_element_type=jnp.float32)
        act = (jax.nn.silu(hg) * hu).astype(jnp.bfloat16)
        yo = jnp.dot(act, wd_buf[s], preferred_element_type=jnp.float32)
        yo_cur[...] = yo.reshape(MOE_TM // SUBLANES, SUBLANES, D_MODEL)
        for j in range(MOE_TM):
            _scatter_rows(pair_scr, y_scr, yo_prv, row_prv, j)

    def tile_iteration(t, carry):
        @pl.when(new_ref[t0 + t] == 1)
        def _():
            for cp in weight_copies(t):
                cp.wait()

        @pl.when(jnp.logical_and(t + 1 < n_act, new_ref[t0 + t + 1] == 1))
        def _():
            for cp in weight_copies(t + 1):
                cp.start()

        for s in range(2):
            in_slot = slot_ref[t0 + t] == s

            @pl.when(jnp.logical_and((t & 1) == 0, in_slot))
            def _():
                expert_step(t, s, xs_a, yo_a, xs_b, yo_b)

            @pl.when(jnp.logical_and((t & 1) == 1, in_slot))
            def _():
                expert_step(t, s, xs_b, yo_b, xs_a, yo_a)
        return carry
    lax.fori_loop(0, n_act, tile_iteration, 0)

    def drain(yo_last):
        row_last = (n_act - 1) * MOE_TM

        def scatter(gi, carry):
            for u in range(SUBLANES):
                y_scr[pl.ds(pair_scr[row_last + gi * SUBLANES + u], 1), :] = yo_last[gi, u:u + 1, :]
            return carry
        lax.fori_loop(0, MOE_TM // SUBLANES, scatter, 0)

    @pl.when((n_act & 1) == 1)
    def _():
        drain(yo_a)

    @pl.when((n_act & 1) == 0)
    def _():
        drain(yo_b)

    def combine(k, carry):
        lo = pl.multiple_of(k * ROW_TILE, ROW_TILE)
        gates = gate_ref[pl.ds(lo, ROW_TILE), :]
        moe = (gates[:, 0:1] * y_scr[pl.ds(lo, ROW_TILE), :]
               + gates[:, 1:2] * y_scr[pl.ds(MOE_CHUNK + lo, ROW_TILE), :])
        x = x_ref[pl.ds(lo, ROW_TILE), :]
        y_scr[pl.ds(lo, ROW_TILE), :] = _layer_norm(DEEPNORM_ALPHA * x + moe, g_ref[...], b_ref[...])
        return carry
    lax.fori_loop(0, COMBINE_STEPS, combine, 0)

    @pl.when(c < SAMPLE_CHUNK)
    def _():
        prompt_out_copy(c).start()

    @pl.when(c == SAMPLE_CHUNK)
    def _():
        sample_out_copy().start()
        sample_out_copy().wait()


def _moe(x, plan, gates_t, layer, wg, wu, wd, ln_g, ln_b):
    tile_expert, is_new, weight_slot, n_active, fill_lo, fill_hi, pos = plan
    stage = pltpu.VMEM((MOE_TM // SUBLANES, SUBLANES, D_MODEL), jnp.float32)
    per_chunk = lambda c, *_: (c, 0, 0)
    grid_spec = pltpu.PrefetchScalarGridSpec(
        num_scalar_prefetch=6,
        grid=(N_CHUNKS,),
        in_specs=[
            pl.BlockSpec((None, None, 2 * MOE_CHUNK), per_chunk, memory_space=pltpu.SMEM),
            pl.BlockSpec((None, MOE_CHUNK, D_MODEL), per_chunk),
            pl.BlockSpec((None, MOE_CHUNK, 2), per_chunk),
            pl.BlockSpec(memory_space=pl.ANY),
            pl.BlockSpec(memory_space=pl.ANY),
            pl.BlockSpec(memory_space=pl.ANY),
            pl.BlockSpec((1, D_MODEL), lambda c, *_: (0, 0)),
            pl.BlockSpec((1, D_MODEL), lambda c, *_: (0, 0)),
        ],
        out_specs=[pl.BlockSpec(memory_space=pl.ANY), pl.BlockSpec(memory_space=pl.ANY)],
        scratch_shapes=[
            pltpu.SMEM((MOE_ROWS + MOE_TM + SUBLANES,), jnp.int32),
            stage, stage, stage, stage,
            pltpu.VMEM((2 * MOE_CHUNK + SUBLANES, D_MODEL), jnp.float32),
            pltpu.VMEM((2, D_MODEL, D_EXPERT), jnp.bfloat16),
            pltpu.VMEM((2, D_MODEL, D_EXPERT), jnp.bfloat16),
            pltpu.VMEM((2, D_EXPERT, D_MODEL), jnp.bfloat16),
            pltpu.SemaphoreType.DMA((2, 3)),
            pltpu.SemaphoreType.DMA((1,)),
        ],
    )
    return pl.pallas_call(
        functools.partial(_moe_kernel, layer),
        grid_spec=grid_spec,
        out_shape=[jax.ShapeDtypeStruct((BATCH, MOE_CHUNK, D_MODEL), jnp.float32),
                   jax.ShapeDtypeStruct((1, MOE_CHUNK, D_MODEL), jnp.float32)],
        compiler_params=pltpu.CompilerParams(
            dimension_semantics=("arbitrary",), vmem_limit_bytes=VMEM_LIMIT),
        name="moe",
    )(tile_expert, is_new, weight_slot, n_active, fill_lo, fill_hi,
      pos.reshape(N_CHUNKS, 1, 2 * MOE_CHUNK), x, gates_t, wg, wu, wd, ln_g, ln_b)


def _moe_layer(x, wr_t, br, layer, wg, wu, wd, ln_g, ln_b):
    pos, gates, meta = _router(x, wr_t, br)
    per_tile = lambda row: meta[:, row, :MOE_STEPS].reshape(-1)
    per_expert = lambda row: meta[:, row, :N_EXPERTS].reshape(-1)
    plan = (per_tile(META_TILE_EXPERT), per_tile(META_IS_NEW), per_tile(META_WEIGHT_SLOT),
            meta[:, META_N_ACTIVE, 0], per_expert(META_FILL_LO), per_expert(META_FILL_HI), pos)
    return _moe(x, plan, jnp.swapaxes(gates, 1, 2), layer, wg, wu, wd, ln_g, ln_b)


def _rope_tables(pos):
    half = HEAD_DIM // 2
    inv_freq = ROPE_THETA ** (-jnp.arange(half, dtype=jnp.float32) / half)
    ang = pos.astype(jnp.float32)[:, None] * inv_freq[None, :]
    cos = jnp.cos(ang)
    sin = jnp.sin(ang)
    reps = LANES // HEAD_DIM
    cos_t = jnp.tile(jnp.concatenate([cos, cos], axis=1), (1, reps))
    sin_t = jnp.tile(jnp.concatenate([-sin, sin], axis=1), (1, reps))
    return cos_t, sin_t


def _block_diag_tiles(w):
    per = MXU_EDGE // LRU_BLOCK_W
    w4 = w.reshape(LRU_BLOCKS // per, per, LRU_BLOCK_W, LRU_BLOCK_W)
    eye = jnp.eye(per, dtype=w.dtype)
    return jnp.einsum("qaij,ab->qaibj", w4, eye).reshape(LRU_BLOCKS // per, MXU_EDGE, MXU_EDGE)


def kernel(x_prompt, x_sample, state_conv_0, state_h_0, cache_k_1, cache_v_1, state_conv_2, state_h_2,
           cache_k_3, cache_v_3, w_router, b_router, ln_g, ln_b, lru_w_in, lru_conv_w, lru_conv_b,
           lru_w_gates, lru_b_gates, lru_lambda, lru_w_out, attn_w_qkv, attn_b_qkv, attn_sinks, attn_w_o,
           moe_w_gate, moe_w_up, moe_w_down):
    bf = jnp.bfloat16
    in_state = {0: (state_conv_0, state_h_0), 1: (cache_k_1, cache_v_1),
                2: (state_conv_2, state_h_2), 3: (cache_k_3, cache_v_3)}
    nt_s = DEC_BATCH // SAMPLE_SEGS
    kv_lanes = N_KV_HEADS * HEAD_DIM

    wr_t = w_router.T
    br = b_router.reshape(N_EXPERTS, 1)
    cos_p, sin_p = _rope_tables(jnp.arange(SEQ))
    cos_s, sin_s = _rope_tables(PAST_LEN + jnp.arange(DEC_SEQ))
    cos_s = jnp.tile(cos_s, (SAMPLE_SEGS, 1))
    sin_s = jnp.tile(sin_s, (SAMPLE_SEGS, 1))
    wg_all = moe_w_gate.astype(bf)
    wu_all = moe_w_up.astype(bf)
    wd_all = moe_w_down.astype(bf)

    x = x_prompt
    x_s = x_sample.reshape(1, MOE_CHUNK, D_MODEL)
    new_p, new_s = {}, {}
    for i in range(DEPTH):
        slot = i // 2
        g0 = ln_g[i, 0].reshape(1, D_MODEL)
        b0 = ln_b[i, 0].reshape(1, D_MODEL)
        g1 = ln_g[i, 1].reshape(1, D_MODEL)
        b1 = ln_b[i, 1].reshape(1, D_MODEL)
        st_a, st_b = in_state[i]
        if i % 2 == 0:
            gates = lru_w_gates[slot]
            w = (lru_w_in[slot].astype(bf), lru_conv_w[slot], lru_conv_b[slot].reshape(1, D_RNN),
                 _block_diag_tiles(gates[:, :, :LRU_BLOCK_W]).astype(bf),
                 _block_diag_tiles(gates[:, :, LRU_BLOCK_W:]).astype(bf),
                 lru_b_gates[slot][:, :LRU_BLOCK_W].reshape(1, D_RNN),
                 lru_b_gates[slot][:, LRU_BLOCK_W:].reshape(1, D_RNN),
                 lru_lambda[slot].reshape(1, D_RNN), lru_w_out[slot].astype(bf))
            conv_pad = jnp.pad(st_a, ((0, 0), (SUBLANES - (CONV_WIDTH - 1), 0), (0, 0)))
            x1, conv_p, h_p, conv_s, h_s = _lru(
                x, x_s, conv_pad.reshape(nt_s, SAMPLE_SEGS, SUBLANES, D_RNN),
                st_b.reshape(nt_s, SAMPLE_SEGS, D_RNN), w, g0, b0)
            new_p[i] = (conv_p, h_p.reshape(BATCH, D_RNN))
            new_s[i] = (conv_s.reshape(DEC_BATCH, CONV_WIDTH - 1, D_RNN), h_s.reshape(DEC_BATCH, D_RNN))
        else:
            w = (attn_w_qkv[slot].astype(bf), attn_b_qkv[slot].reshape(1, QKV_DIM), attn_sinks[slot],
                 attn_w_o[slot].astype(bf))
            x1, k_p, v_p, k_s, v_s = _swa(
                x, x_s, st_a.reshape(nt_s, SAMPLE_SEGS, WINDOW, kv_lanes),
                st_b.reshape(nt_s, SAMPLE_SEGS, WINDOW, kv_lanes), w, cos_p, sin_p, cos_s, sin_s, g0, b0)
            new_p[i] = (k_p.reshape(BATCH, WINDOW, N_KV_HEADS, HEAD_DIM),
                        v_p.reshape(BATCH, WINDOW, N_KV_HEADS, HEAD_DIM))
            new_s[i] = (k_s.reshape(DEC_BATCH, WINDOW, N_KV_HEADS, HEAD_DIM),
                        v_s.reshape(DEC_BATCH, WINDOW, N_KV_HEADS, HEAD_DIM))
        x, x_s = _moe_layer(x1, wr_t, br, i, wg_all, wu_all, wd_all, g1, b1)
    return (x, x_s.reshape(DEC_BATCH, DEC_SEQ, D_MODEL),
            new_p[0][0], new_s[0][0], new_p[0][1], new_s[0][1],
            new_p[1][0], new_s[1][0], new_p[1][1], new_s[1][1],
            new_p[2][0], new_s[2][0], new_p[2][1], new_s[2][1],
            new_p[3][0], new_s[3][0], new_p[3][1], new_s[3][1])
```

```python
import functools

import jax
import jax.numpy as jnp
from jax import lax
from jax.experimental import pallas as pl
from jax.experimental.pallas import tpu as pltpu

D_MODEL = 1024
BATCH = 8
SEQ = 2048
DEPTH = 4
DEC_BATCH = 32
DEC_SEQ = 64
PAST_LEN = 4096
CHUNK = 64
D_RNN = D_MODEL
CONV_WIDTH = 4
LRU_BLOCKS = 16
LRU_BLOCK_W = D_RNN // LRU_BLOCKS
LRU_C = 8.0
N_HEADS = 16
N_KV_HEADS = 2
HEAD_DIM = 64
KV_GROUP = N_HEADS // N_KV_HEADS
WINDOW = 128
ROPE_THETA = 10000.0
QKV_DIM = (N_HEADS + 2 * N_KV_HEADS) * HEAD_DIM
N_EXPERTS = 16
N_GROUPS = 4
EXPERTS_PER_GROUP = N_EXPERTS // N_GROUPS
D_EXPERT = 512
DEEPNORM_ALPHA = (2.0 * DEPTH) ** 0.25
LN_EPS = 1e-5

MXU_EDGE = 256
LANES = 128
SUBLANES = 8

ROW_TILE = 256
MOE_CHUNK = SEQ
N_CHUNKS = BATCH + DEC_BATCH * DEC_SEQ // MOE_CHUNK
SAMPLE_CHUNK = BATCH
MOE_TM = 256
MOE_ROWS = 2 * MOE_CHUNK + N_EXPERTS * MOE_TM
MOE_TILES = MOE_ROWS // MOE_TM
MOE_STEPS = MOE_TILES + 1
COMBINE_STEPS = MOE_CHUNK // ROW_TILE
JUNK_PAIR = 2 * MOE_CHUNK
SAMPLE_SEGS = ROW_TILE // DEC_SEQ
NEG_BIG = -1e30
VMEM_LIMIT = 52 * 1024 * 1024

assert DEC_BATCH * DEC_SEQ == MOE_CHUNK and MOE_TILES <= LANES


def _layer_norm(x, g, b):
    mu = jnp.mean(x, -1, keepdims=True)
    xc = x - mu
    var = jnp.mean(xc * xc, -1, keepdims=True)
    return xc * lax.rsqrt(var + LN_EPS) * g + b


def _finish_mixer(x, y, g_ref, b_ref, x1_ref):
    x1_ref[...] = _layer_norm(DEEPNORM_ALPHA * x + y, g_ref[...], b_ref[...])


def _const_spec(shape):
    nd = len(shape)
    return pl.BlockSpec(shape, lambda *_: (0,) * nd)


_ACT_SHAPE = jax.ShapeDtypeStruct((N_CHUNKS, MOE_CHUNK, D_MODEL), jnp.float32)


def _lru_tile(x, ext_scr, h_in, n_seg, seg, win_ref, cw_ref, cb_ref, wr_ref, wi_ref,
              br_ref, bi_ref, lam_ref, wout_ref):
    rows = n_seg * seg
    gr = jnp.dot(x.astype(jnp.bfloat16), win_ref[...], preferred_element_type=jnp.float32)
    gate_in = gr[:, :D_RNN]
    rec = gr[:, D_RNN:]
    cw = cw_ref[...]
    xcs = []
    for s in range(n_seg):
        ext_scr[s, SUBLANES:SUBLANES + seg, :] = rec[s * seg:(s + 1) * seg]
        xc = cb_ref[...] + cw[3:4] * rec[s * seg:(s + 1) * seg]
        for k in range(CONV_WIDTH - 1):
            off = SUBLANES - (CONV_WIDTH - 1) + k
            xc = xc + cw[k:k + 1] * ext_scr[s, off:off + seg, :]
        xcs.append(xc)
    xc = xcs[0] if n_seg == 1 else jnp.concatenate(xcs, axis=0)

    xcb = xc.astype(jnp.bfloat16)
    r_parts, i_parts = [], []
    for q in range(D_RNN // MXU_EDGE):
        blk = xcb[:, q * MXU_EDGE:(q + 1) * MXU_EDGE]
        r_parts.append(jnp.dot(blk, wr_ref[q], preferred_element_type=jnp.float32))
        i_parts.append(jnp.dot(blk, wi_ref[q], preferred_element_type=jnp.float32))
    r = jax.nn.sigmoid(jnp.concatenate(r_parts, axis=1) + br_ref[...])
    ig = jax.nn.sigmoid(jnp.concatenate(i_parts, axis=1) + bi_ref[...])

    z = -lam_ref[...]
    softplus = jnp.maximum(z, 0.0) + jnp.log1p(jnp.exp(-jnp.abs(z)))
    log_a = (-LRU_C * r) * softplus
    a = jnp.exp(log_a)
    b = jnp.sqrt(-jnp.tanh(log_a) * (a * a + 1.0)) * (ig * xc)

    sub = lax.broadcasted_iota(jnp.int32, (rows, D_RNN), 0) & (SUBLANES - 1)
    step = 1
    while step < SUBLANES:
        keep = sub >= step
        a_prev = pltpu.roll(a, step, 0)
        b_prev = pltpu.roll(b, step, 0)
        b = jnp.where(keep, a * b_prev + b, b)
        a = jnp.where(keep, a * a_prev, a)
        step *= 2
    groups_per_seg = seg // SUBLANES
    h_blocks, h_last = [], []
    carry = None
    for gi in range(rows // SUBLANES):
        if gi % groups_per_seg == 0:
            carry = h_in[gi // groups_per_seg]
        lo = gi * SUBLANES
        hb = b[lo:lo + SUBLANES] + a[lo:lo + SUBLANES] * carry
        carry = hb[SUBLANES - 1:SUBLANES]
        h_blocks.append(hb)
        if gi % groups_per_seg == groups_per_seg - 1:
            h_last.append(carry)
    h = jnp.concatenate(h_blocks, axis=0)

    y = h * jax.nn.gelu(gate_in, approximate=True)
    out = jnp.dot(y.astype(jnp.bfloat16), wout_ref[...], preferred_element_type=jnp.float32)
    return out, rec, h_last


def _lru_kernel(xp_ref, xs_ref, conv_in_ref, h0_ref, win_ref, cw_ref, cb_ref, wr_ref, wi_ref, br_ref,
                bi_ref, lam_ref, wout_ref, g_ref, b_ref, x1_ref, conv_p_ref, h_p_ref, conv_s_ref, h_s_ref,
                ext_p_scr, h_scr, ext_s_scr):
    c = pl.program_id(0)
    t = pl.program_id(1)
    weights = (win_ref, cw_ref, cb_ref, wr_ref, wi_ref, br_ref, bi_ref, lam_ref, wout_ref)

    @pl.when(c < BATCH)
    def _():
        @pl.when(t == 0)
        def _():
            ext_p_scr[0, 0:SUBLANES, :] = jnp.zeros((SUBLANES, D_RNN), jnp.float32)
            h_scr[...] = jnp.zeros_like(h_scr)

        x = xp_ref[...]
        out, rec, h_last = _lru_tile(x, ext_p_scr, [h_scr[...]], 1, ROW_TILE, *weights)
        ext_p_scr[0, 0:SUBLANES, :] = rec[ROW_TILE - SUBLANES:]
        h_scr[...] = h_last[0]
        conv_p_ref[...] = rec[ROW_TILE - (CONV_WIDTH - 1):]
        h_p_ref[...] = h_last[0]
        _finish_mixer(x, out, g_ref, b_ref, x1_ref)

    @pl.when(c == SAMPLE_CHUNK)
    def _():
        for s in range(SAMPLE_SEGS):
            ext_s_scr[s, 0:SUBLANES, :] = conv_in_ref[s]
        h_in = [h0_ref[s:s + 1, :] for s in range(SAMPLE_SEGS)]
        x = xs_ref[...]
        out, rec, h_last = _lru_tile(x, ext_s_scr, h_in, SAMPLE_SEGS, DEC_SEQ, *weights)
        for s in range(SAMPLE_SEGS):
            end = (s + 1) * DEC_SEQ
            conv_s_ref[s] = rec[end - (CONV_WIDTH - 1):end]
            h_s_ref[s:s + 1, :] = h_last[s]
        _finish_mixer(x, out, g_ref, b_ref, x1_ref)


_TILES_PER_CHUNK = MOE_CHUNK // ROW_TILE


def _prompt_tile(c, t):
    return jnp.where(c < BATCH, t, _TILES_PER_CHUNK - 1)


def _sample_tile(c, t):
    return jnp.where(c == SAMPLE_CHUNK, t, 0)


def _mixer_act_specs():
    return [
        pl.BlockSpec((None, ROW_TILE, D_MODEL), lambda c, t: (jnp.minimum(c, BATCH - 1), _prompt_tile(c, t), 0)),
        pl.BlockSpec((None, ROW_TILE, D_MODEL), lambda c, t: (0, _sample_tile(c, t), 0)),
    ]


_MIXER_GRID = (N_CHUNKS, _TILES_PER_CHUNK)
_PER_PROMPT = lambda c, t: (jnp.minimum(c, BATCH - 1), 0, 0)
_PER_SAMPLE3 = lambda c, t: (_sample_tile(c, t), 0, 0)
_PER_SAMPLE4 = lambda c, t: (_sample_tile(c, t), 0, 0, 0)


def _lru(xp, xs, conv_pad, h0, w, ln_g, ln_b):
    nt_s = DEC_BATCH // SAMPLE_SEGS
    return pl.pallas_call(
        _lru_kernel,
        grid=_MIXER_GRID,
        in_specs=_mixer_act_specs() + [
            pl.BlockSpec((None, SAMPLE_SEGS, SUBLANES, D_RNN), _PER_SAMPLE4),
            pl.BlockSpec((None, SAMPLE_SEGS, D_RNN), _PER_SAMPLE3),
            _const_spec((D_MODEL, 2 * D_RNN)),
            _const_spec((CONV_WIDTH, D_RNN)),
            _const_spec((1, D_RNN)),
            _const_spec((D_RNN // MXU_EDGE, MXU_EDGE, MXU_EDGE)),
            _const_spec((D_RNN // MXU_EDGE, MXU_EDGE, MXU_EDGE)),
            _const_spec((1, D_RNN)),
            _const_spec((1, D_RNN)),
            _const_spec((1, D_RNN)),
            _const_spec((D_RNN, D_MODEL)),
            _const_spec((1, D_MODEL)),
            _const_spec((1, D_MODEL)),
        ],
        out_specs=[
            pl.BlockSpec((None, ROW_TILE, D_MODEL), lambda c, t: (c, t, 0)),
            pl.BlockSpec((None, CONV_WIDTH - 1, D_RNN), _PER_PROMPT),
            pl.BlockSpec((None, 1, D_RNN), _PER_PROMPT),
            pl.BlockSpec((None, SAMPLE_SEGS, CONV_WIDTH - 1, D_RNN), _PER_SAMPLE4),
            pl.BlockSpec((None, SAMPLE_SEGS, D_RNN), _PER_SAMPLE3),
        ],
        out_shape=[
            _ACT_SHAPE,
            jax.ShapeDtypeStruct((BATCH, CONV_WIDTH - 1, D_RNN), jnp.float32),
            jax.ShapeDtypeStruct((BATCH, 1, D_RNN), jnp.float32),
            jax.ShapeDtypeStruct((nt_s, SAMPLE_SEGS, CONV_WIDTH - 1, D_RNN), jnp.float32),
            jax.ShapeDtypeStruct((nt_s, SAMPLE_SEGS, D_RNN), jnp.float32),
        ],
        scratch_shapes=[
            pltpu.VMEM((1, SUBLANES + ROW_TILE, D_RNN), jnp.float32),
            pltpu.VMEM((1, D_RNN), jnp.float32),
            pltpu.VMEM((SAMPLE_SEGS, SUBLANES + DEC_SEQ, D_RNN), jnp.float32),
        ],
        compiler_params=pltpu.CompilerParams(
            dimension_semantics=("arbitrary", "arbitrary"), vmem_limit_bytes=VMEM_LIMIT),
        name="lru",
    )(xp, xs, conv_pad, h0, *w, ln_g, ln_b)


def _rope(x, cos, sin_signed):
    lane = lax.broadcasted_iota(jnp.int32, (x.shape[0], LANES), 1)
    first_half = (lane & (HEAD_DIM - 1)) < HEAD_DIM // 2
    cols = []
    for c in range(x.shape[1] // LANES):
        xc = x[:, c * LANES:(c + 1) * LANES]
        partner = jnp.where(first_half, pltpu.roll(xc, LANES - HEAD_DIM // 2, 1),
                            pltpu.roll(xc, HEAD_DIM // 2, 1))
        cols.append(xc * cos + partner * sin_signed)
    return cols[0] if len(cols) == 1 else jnp.concatenate(cols, axis=1)


def _dup_heads(kv):
    lane = lax.broadcasted_iota(jnp.int32, kv.shape, 1)
    low = lane < HEAD_DIM
    swapped = pltpu.roll(kv, HEAD_DIM, 1)
    return (jnp.where(low, kv, swapped).astype(jnp.bfloat16),
            jnp.where(low, swapped, kv).astype(jnp.bfloat16))


def _attend(chunks, sinks_ref):
    lane = lax.broadcasted_iota(jnp.int32, (CHUNK, LANES), 1)
    low = lane < HEAD_DIM
    zero = jnp.zeros((CHUNK, LANES), jnp.bfloat16)
    cols_per_group = KV_GROUP * HEAD_DIM // LANES
    units = [(ci, g) for ci in range(len(chunks)) for g in range(N_KV_HEADS)]

    scores = {}
    for ci, g in units:
        q, k_wins, _, valid = chunks[ci]
        stacked = []
        for c in range(g * cols_per_group, (g + 1) * cols_per_group):
            qc = q[:, c * LANES:(c + 1) * LANES]
            stacked.append(jnp.where(low, qc, zero))
            stacked.append(jnp.where(low, zero, qc))
        qst = jnp.concatenate(stacked, axis=0)
        s = lax.dot_general(qst, k_wins[g], (((1,), (1,)), ((), ())),
                            preferred_element_type=jnp.float32)
        scores[ci, g] = s if valid is None else jnp.where(valid, s, NEG_BIG)

    probs = {}
    for ci, g in units:
        s = scores[ci, g]
        heads = []
        for hh in range(KV_GROUP):
            sh = s[hh * CHUNK:(hh + 1) * CHUNK]
            sink = sinks_ref[g * KV_GROUP + hh]
            m = jnp.maximum(jnp.max(sh, -1, keepdims=True), sink)
            p = jnp.exp(sh - m)
            denom = jnp.sum(p, -1, keepdims=True) + jnp.exp(sink - m)
            heads.append((p * (1.0 / denom)).astype(jnp.bfloat16))
        probs[ci, g] = jnp.concatenate(heads, axis=0)

    out_cols = {ci: [] for ci in range(len(chunks))}
    for ci, g in units:
        o = jnp.dot(probs[ci, g], chunks[ci][2][g], preferred_element_type=jnp.float32)
        for j in range(cols_per_group):
            oa = o[(2 * j) * CHUNK:(2 * j + 1) * CHUNK]
            ob = o[(2 * j + 1) * CHUNK:(2 * j + 2) * CHUNK]
            out_cols[ci].append(jnp.where(low, oa, ob))
    return [jnp.concatenate(out_cols[ci], axis=1) for ci in range(len(chunks))]


def _project_qkv(x, wqkv_ref, bqkv_ref, cos_ref, sin_ref):
    qkv = jnp.dot(x.astype(jnp.bfloat16), wqkv_ref[...], preferred_element_type=jnp.float32)
    qkv = qkv + bqkv_ref[...]
    nq = N_HEADS * HEAD_DIM
    nk = N_KV_HEADS * HEAD_DIM
    cos = cos_ref[...]
    sin = sin_ref[...]
    q = _rope(qkv[:, :nq], cos, sin)
    k = _rope(qkv[:, nq:nq + nk], cos, sin)
    v = qkv[:, nq + nk:]
    qs = (q * (HEAD_DIM ** -0.5)).astype(jnp.bfloat16)
    return qs, k, v


def _swa_kernel(xp_ref, xs_ref, ck_ref, cv_ref, wqkv_ref, bqkv_ref, cosp_ref, sinp_ref, coss_ref, sins_ref,
                sinks_ref, wo_ref, g_ref, b_ref, x1_ref, kp_ref, vp_ref, ks_ref, vs_ref,
                k0_scr, k1_scr, v0_scr, v1_scr):
    c = pl.program_id(0)
    t = pl.program_id(1)
    scrs = (k0_scr, k1_scr, v0_scr, v1_scr)

    @pl.when(c < BATCH)
    def _():
        @pl.when(t == 0)
        def _():
            for scr in scrs:
                scr[0:WINDOW, :] = jnp.zeros((WINDOW, LANES), jnp.bfloat16)

        x = xp_ref[...]
        qs, k, v = _project_qkv(x, wqkv_ref, bqkv_ref, cosp_ref, sinp_ref)
        kp_ref[...] = k[ROW_TILE - WINDOW:]
        vp_ref[...] = v[ROW_TILE - WINDOW:]
        k0, k1 = _dup_heads(k)
        v0, v1 = _dup_heads(v)
        for scr, val in zip(scrs, (k0, k1, v0, v1)):
            scr[WINDOW:WINDOW + ROW_TILE, :] = val

        span = WINDOW + CHUNK
        key_pos = lax.broadcasted_iota(jnp.int32, (1, span), 1)
        chunks = []
        for ci in range(ROW_TILE // CHUNK):
            lo = ci * CHUNK
            valid = (t * ROW_TILE + lo - WINDOW + key_pos) >= 0
            k_wins = (k0_scr[lo:lo + span, :], k1_scr[lo:lo + span, :])
            v_wins = (v0_scr[lo:lo + span, :], v1_scr[lo:lo + span, :])
            chunks.append((qs[lo:lo + CHUNK], k_wins, v_wins, valid))
        o = jnp.concatenate(_attend(chunks, sinks_ref), axis=0)
        for scr in scrs:
            scr[0:WINDOW, :] = scr[ROW_TILE:ROW_TILE + WINDOW, :]
        out = jnp.dot(o.astype(jnp.bfloat16), wo_ref[...], preferred_element_type=jnp.float32)
        _finish_mixer(x, out, g_ref, b_ref, x1_ref)

    @pl.when(c == SAMPLE_CHUNK)
    def _():
        x = xs_ref[...]
        qs, k, v = _project_qkv(x, wqkv_ref, bqkv_ref, coss_ref, sins_ref)
        knew = _dup_heads(k)
        vnew = _dup_heads(v)
        chunks = []
        for s in range(SAMPLE_SEGS):
            lo = s * DEC_SEQ
            ck = ck_ref[s]
            cv = cv_ref[s]
            kold = _dup_heads(ck)
            vold = _dup_heads(cv)
            k_wins = [jnp.concatenate([kold[g], knew[g][lo:lo + DEC_SEQ]], axis=0) for g in range(N_KV_HEADS)]
            v_wins = [jnp.concatenate([vold[g], vnew[g][lo:lo + DEC_SEQ]], axis=0) for g in range(N_KV_HEADS)]
            chunks.append((qs[lo:lo + DEC_SEQ], k_wins, v_wins, None))
            ks_ref[s, 0:WINDOW - DEC_SEQ, :] = ck[DEC_SEQ:]
            ks_ref[s, WINDOW - DEC_SEQ:WINDOW, :] = k[lo:lo + DEC_SEQ]
            vs_ref[s, 0:WINDOW - DEC_SEQ, :] = cv[DEC_SEQ:]
            vs_ref[s, WINDOW - DEC_SEQ:WINDOW, :] = v[lo:lo + DEC_SEQ]
        o = jnp.concatenate(_attend(chunks, sinks_ref), axis=0)
        out = jnp.dot(o.astype(jnp.bfloat16), wo_ref[...], preferred_element_type=jnp.float32)
        _finish_mixer(x, out, g_ref, b_ref, x1_ref)


def _swa(xp, xs, ck, cv, w, cos_p, sin_p, cos_s, sin_s, ln_g, ln_b):
    wqkv, bqkv, sinks, wo = w
    nt_s = DEC_BATCH // SAMPLE_SEGS
    kv_lanes = N_KV_HEADS * HEAD_DIM
    cache_spec = pl.BlockSpec((None, SAMPLE_SEGS, WINDOW, kv_lanes), _PER_SAMPLE4)
    prompt_table = pl.BlockSpec((ROW_TILE, LANES), lambda c, t: (_prompt_tile(c, t), 0))
    kv_prompt = pl.BlockSpec((None, WINDOW, kv_lanes), _PER_PROMPT)
    return pl.pallas_call(
        _swa_kernel,
        grid=_MIXER_GRID,
        in_specs=_mixer_act_specs() + [
            cache_spec, cache_spec,
            _const_spec((D_MODEL, QKV_DIM)),
            _const_spec((1, QKV_DIM)),
            prompt_table, prompt_table,
            _const_spec((ROW_TILE, LANES)), _const_spec((ROW_TILE, LANES)),
            pl.BlockSpec(memory_space=pltpu.SMEM),
            _const_spec((N_HEADS * HEAD_DIM, D_MODEL)),
            _const_spec((1, D_MODEL)),
            _const_spec((1, D_MODEL)),
        ],
        out_specs=[
            pl.BlockSpec((None, ROW_TILE, D_MODEL), lambda c, t: (c, t, 0)),
            kv_prompt, kv_prompt, cache_spec, cache_spec,
        ],
        out_shape=[
            _ACT_SHAPE,
            jax.ShapeDtypeStruct((BATCH, WINDOW, kv_lanes), jnp.float32),
            jax.ShapeDtypeStruct((BATCH, WINDOW, kv_lanes), jnp.float32),
            jax.ShapeDtypeStruct((nt_s, SAMPLE_SEGS, WINDOW, kv_lanes), jnp.float32),
            jax.ShapeDtypeStruct((nt_s, SAMPLE_SEGS, WINDOW, kv_lanes), jnp.float32),
        ],
        scratch_shapes=[pltpu.VMEM((WINDOW + ROW_TILE, LANES), jnp.bfloat16) for _ in range(4)],
        compiler_params=pltpu.CompilerParams(
            dimension_semantics=("arbitrary", "arbitrary"), vmem_limit_bytes=VMEM_LIMIT),
        name="swa",
    )(xp, xs, ck, cv, wqkv, bqkv, cos_p, sin_p, cos_s, sin_s, sinks, wo, ln_g, ln_b)


META_TILE_EXPERT, META_N_ACTIVE, META_FILL_LO, META_FILL_HI, META_IS_NEW, META_WEIGHT_SLOT = range(6)


def _router_kernel(x_ref, wr_ref, br_ref, pos_ref, gate_ref, meta_ref):
    f32 = jnp.float32
    logits = lax.dot_general(wr_ref[...].astype(jnp.bfloat16), x_ref[...].astype(jnp.bfloat16),
                             (((1,), (1,)), ((), ())), preferred_element_type=f32)
    m = jnp.max(logits, axis=0, keepdims=True)
    e = jnp.exp(logits - m)
    probs = e / jnp.sum(e, axis=0, keepdims=True)
    sel = probs + br_ref[...]
    p = [probs[i:i + 1] for i in range(N_EXPERTS)]
    s = [sel[i:i + 1] for i in range(N_EXPERTS)]

    def first_argmax(vals):
        best, arg = vals[0], jnp.zeros_like(vals[0], dtype=jnp.int32)
        for i in range(1, len(vals)):
            better = vals[i] > best
            best = jnp.where(better, vals[i], best)
            arg = jnp.where(better, i, arg)
        return arg

    group_scores = []
    for gidx in range(N_GROUPS):
        v = s[gidx * EXPERTS_PER_GROUP:(gidx + 1) * EXPERTS_PER_GROUP]
        best = v[0] + v[1]
        for i in range(EXPERTS_PER_GROUP):
            for j in range(i + 1, EXPERTS_PER_GROUP):
                if (i, j) != (0, 1):
                    best = jnp.maximum(best, v[i] + v[j])
        group_scores.append(best)
    best_group = first_argmax(group_scores)
    neg_inf = jnp.full_like(s[0], -jnp.inf)
    masked = [jnp.where(best_group == (i // EXPERTS_PER_GROUP), s[i], neg_inf) for i in range(N_EXPERTS)]
    idx1 = first_argmax(masked)
    masked2 = [jnp.where(idx1 == i, neg_inf, masked[i]) for i in range(N_EXPERTS)]
    idx2 = first_argmax(masked2)
    zero = jnp.zeros_like(p[0])
    w1, w2 = zero, zero
    for i in range(N_EXPERTS):
        w1 = w1 + jnp.where(idx1 == i, p[i], zero)
        w2 = w2 + jnp.where(idx2 == i, p[i], zero)
    tot = w1 + w2
    gate_ref[0:1, :] = w1 / tot
    gate_ref[1:2, :] = w2 / tot

    expert = lax.broadcasted_iota(jnp.int32, (N_EXPERTS, MOE_CHUNK), 0)
    strict_upper = (lax.broadcasted_iota(jnp.int32, (LANES, LANES), 0)
                    < lax.broadcasted_iota(jnp.int32, (LANES, LANES), 1)).astype(jnp.bfloat16)
    carry = jnp.zeros((N_EXPERTS, 1), f32)
    onehots, ranks = [], []
    for idx in (idx1, idx2):
        onehot = (expert == idx).astype(f32)
        before = []
        for blk in range(MOE_CHUNK // LANES):
            oh = onehot[:, blk * LANES:(blk + 1) * LANES]
            before.append(jnp.dot(oh.astype(jnp.bfloat16), strict_upper, preferred_element_type=f32) + carry)
            carry = carry + jnp.sum(oh, axis=1, keepdims=True)
        onehots.append(onehot)
        ranks.append(jnp.sum(onehot * jnp.concatenate(before, axis=1), axis=0, keepdims=True))
    counts = carry
    padded = jnp.floor((counts + (MOE_TM - 1)) * (1.0 / MOE_TM)) * MOE_TM
    strict_lower = (lax.broadcasted_iota(jnp.int32, (N_EXPERTS, N_EXPERTS), 0)
                    > lax.broadcasted_iota(jnp.int32, (N_EXPERTS, N_EXPERTS), 1)).astype(jnp.bfloat16)
    starts = jnp.dot(strict_lower, jnp.broadcast_to(padded, (N_EXPERTS, LANES)).astype(jnp.bfloat16),
                     preferred_element_type=f32)[:, 0:1]
    ends = starts + padded
    for k in range(2):
        pos = jnp.sum(onehots[k] * starts, axis=0, keepdims=True) + ranks[k]
        pos_ref[k:k + 1, :] = pos.astype(jnp.int32)

    lane = lax.broadcasted_iota(jnp.int32, (N_EXPERTS, LANES), 1)
    sub = lax.broadcasted_iota(jnp.int32, (N_EXPERTS, LANES), 0)
    tile_start = (lane * MOE_TM).astype(f32)
    tile_expert = jnp.sum((ends <= tile_start).astype(f32), axis=0, keepdims=True)
    tile_expert = jnp.minimum(tile_expert, N_EXPERTS - 1.0)
    on_diag = sub == lane

    def to_lanes(col):
        return jnp.sum(jnp.where(on_diag, col, 0.0), axis=0, keepdims=True)

    lane1 = lax.broadcasted_iota(jnp.int32, (1, LANES), 1)
    is_new = jnp.where(jnp.logical_or(lane1 == 0, tile_expert != pltpu.roll(tile_expert, 1, 1)), 1.0, 0.0)
    upper_incl = (lax.broadcasted_iota(jnp.int32, (LANES, LANES), 0)
                  <= lax.broadcasted_iota(jnp.int32, (LANES, LANES), 1)).astype(jnp.bfloat16)
    runs = jnp.dot(jnp.broadcast_to(is_new, (SUBLANES, LANES)).astype(jnp.bfloat16), upper_incl,
                   preferred_element_type=f32)[0:1]
    weight_slot = (runs.astype(jnp.int32) - 1) & 1

    n_active = jnp.broadcast_to(ends[N_EXPERTS - 1:N_EXPERTS] * (1.0 / MOE_TM), (1, LANES))
    meta_ref[...] = jnp.zeros(meta_ref.shape, jnp.int32)
    meta_ref[META_IS_NEW:META_IS_NEW + 1, :] = is_new.astype(jnp.int32)
    meta_ref[META_WEIGHT_SLOT:META_WEIGHT_SLOT + 1, :] = weight_slot
    meta_ref[META_TILE_EXPERT:META_TILE_EXPERT + 1, :] = tile_expert.astype(jnp.int32)
    meta_ref[META_N_ACTIVE:META_N_ACTIVE + 1, :] = n_active.astype(jnp.int32)
    meta_ref[META_FILL_LO:META_FILL_LO + 1, :] = to_lanes(starts + counts).astype(jnp.int32)
    meta_ref[META_FILL_HI:META_FILL_HI + 1, :] = to_lanes(ends).astype(jnp.int32)


def _router(x, wr_t, br):
    per_chunk = lambda c: (c, 0, 0)
    return pl.pallas_call(
        _router_kernel,
        grid=(N_CHUNKS,),
        in_specs=[
            pl.BlockSpec((None, MOE_CHUNK, D_MODEL), per_chunk),
            _const_spec((N_EXPERTS, D_MODEL)),
            _const_spec((N_EXPERTS, 1)),
        ],
        out_specs=[
            pl.BlockSpec((None, 2, MOE_CHUNK), per_chunk),
            pl.BlockSpec((None, 2, MOE_CHUNK), per_chunk),
            pl.BlockSpec((None, SUBLANES, LANES), per_chunk),
        ],
        out_shape=[
            jax.ShapeDtypeStruct((N_CHUNKS, 2, MOE_CHUNK), jnp.int32),
            jax.ShapeDtypeStruct((N_CHUNKS, 2, MOE_CHUNK), jnp.float32),
            jax.ShapeDtypeStruct((N_CHUNKS, SUBLANES, LANES), jnp.int32),
        ],
        compiler_params=pltpu.CompilerParams(
            dimension_semantics=("parallel",), vmem_limit_bytes=VMEM_LIMIT),
        name="router",
    )(x, wr_t, br)


def _gather_rows(pair_scr, x_ref, xs_scr, row0, j):
    tok = pair_scr[row0 + j] & (MOE_CHUNK - 1)
    xs_scr[j // SUBLANES, (j % SUBLANES):(j % SUBLANES) + 1, :] = x_ref[pl.ds(tok, 1), :]


def _scatter_rows(pair_scr, y_scr, yo_scr, row0, j):
    y_scr[pl.ds(pair_scr[row0 + j], 1), :] = yo_scr[j // SUBLANES, (j % SUBLANES):(j % SUBLANES) + 1, :]


def _moe_kernel(layer, te_ref, new_ref, slot_ref, nact_ref, flo_ref, fhi_ref, pos_ref, x_ref, gate_ref,
                wg_hbm, wu_hbm, wd_hbm, g_ref, b_ref, outp_hbm, outs_hbm,
                pair_scr, xs_a, xs_b, yo_a, yo_b, y_scr, wg_buf, wu_buf, wd_buf, w_sem, out_sem):
    c = pl.program_id(0)
    n_act = nact_ref[c]
    t0 = c * MOE_STEPS

    def weight_copies(t):
        e, s = te_ref[t0 + t], slot_ref[t0 + t]
        return (pltpu.make_async_copy(wg_hbm.at[layer, e], wg_buf.at[s], w_sem.at[s, 0]),
                pltpu.make_async_copy(wu_hbm.at[layer, e], wu_buf.at[s], w_sem.at[s, 1]),
                pltpu.make_async_copy(wd_hbm.at[layer, e], wd_buf.at[s], w_sem.at[s, 2]))

    def prompt_out_copy(chunk):
        return pltpu.make_async_copy(y_scr.at[pl.ds(0, MOE_CHUNK)], outp_hbm.at[chunk], out_sem.at[0])

    def sample_out_copy():
        return pltpu.make_async_copy(y_scr.at[pl.ds(0, MOE_CHUNK)], outs_hbm.at[0], out_sem.at[0])

    for cp in weight_copies(0):
        cp.start()

    def fill8(g, base):
        for u in range(SUBLANES):
            pair_scr[base + g * SUBLANES + u] = JUNK_PAIR
        return base

    def fill_expert(e, carry):
        lo, hi = flo_ref[c * N_EXPERTS + e], fhi_ref[c * N_EXPERTS + e]
        lax.fori_loop(0, (hi - lo + SUBLANES - 1) >> 3, fill8, lo)
        return carry
    lax.fori_loop(0, N_EXPERTS, fill_expert, 0)
    lax.fori_loop(0, MOE_TM // SUBLANES, fill8, fhi_ref[c * N_EXPERTS + N_EXPERTS - 1])

    def invert(t, carry):
        pair_scr[pos_ref[t]] = t
        pair_scr[pos_ref[MOE_CHUNK + t]] = MOE_CHUNK + t
        return carry
    lax.fori_loop(0, MOE_CHUNK, invert, 0, unroll=8)

    def gather0(gi, carry):
        for u in range(SUBLANES):
            tok = pair_scr[gi * SUBLANES + u] & (MOE_CHUNK - 1)
            xs_a[gi, u:u + 1, :] = x_ref[pl.ds(tok, 1), :]
        return carry
    lax.fori_loop(0, MOE_TM // SUBLANES, gather0, 0)

    @pl.when(c == 0)
    def _():
        yo_b[...] = jnp.zeros_like(yo_b)

    @pl.when(c > 0)
    def _():
        prompt_out_copy(c - 1).wait()

    def expert_step(t, s, xs_cur, yo_cur, xs_nxt, yo_prv):
        row_nxt = (t + 1) * MOE_TM
        row_prv = jnp.maximum(t - 1, 0) * MOE_TM
        for j in range(MOE_TM):
            _gather_rows(pair_scr, x_ref, xs_nxt, row_nxt, j)
        xb = xs_cur[...].reshape(MOE_TM, D_MODEL).astype(jnp.bfloat16)
        hg = jnp.dot(xb, wg_buf[s], preferred_element_type=jnp.float32)
        hu = jnp.dot(xb, wu_buf[s], preferred_element_type=jnp.float32)
        act = (jax.nn.silu(hg) * hu).astype(jnp.bfloat16)
        yo = jnp.dot(act, wd_buf[s], preferred_element_type=jnp.float32)
        yo_cur[...] = yo.reshape(MOE_TM // SUBLANES, SUBLANES, D_MODEL)
        for j in range(MOE_TM):
            _scatter_rows(pair_scr, y_scr, yo_prv, row_prv, j)

    def tile_iteration(t, carry):
        @pl.when(new_ref[t0 + t] == 1)
        def _():
            for cp in weight_copies(t):
                cp.wait()

        @pl.when(jnp.logical_and(t + 1 < n_act, new_ref[t0 + t + 1] == 1))
        def _():
            for cp in weight_copies(t + 1):
                cp.start()

        for s in range(2):
            in_slot = slot_ref[t0 + t] == s

            @pl.when(jnp.logical_and((t & 1) == 0, in_slot))
            def _():
                expert_step(t, s, xs_a, yo_a, xs_b, yo_b)

            @pl.when(jnp.logical_and((t & 1) == 1, in_slot))
            def _():
                expert_step(t, s, xs_b, yo_b, xs_a, yo_a)
        return carry
    lax.fori_loop(0, n_act, tile_iteration, 0)

    def drain(yo_last):
        row_last = (n_act - 1) * MOE_TM

        def scatter(gi, carry):
            for u in range(SUBLANES):
                y_scr[pl.ds(pair_scr[row_last + gi * SUBLANES + u], 1), :] = yo_last[gi, u:u + 1, :]
            return carry
        lax.fori_loop(0, MOE_TM // SUBLANES, scatter, 0)

    @pl.when((n_act & 1) == 1)
    def _():
        drain(yo_a)

    @pl.when((n_act & 1) == 0)
    def _():
        drain(yo_b)

    def combine(k, carry):
        lo = pl.multiple_of(k * ROW_TILE, ROW_TILE)
        gates = gate_ref[pl.ds(lo, ROW_TILE), :]
        moe = (gates[:, 0:1] * y_scr[pl.ds(lo, ROW_TILE), :]
               + gates[:, 1:2] * y_scr[pl.ds(MOE_CHUNK + lo, ROW_TILE), :])
        x = x_ref[pl.ds(lo, ROW_TILE), :]
        y_scr[pl.ds(lo, ROW_TILE), :] = _layer_norm(DEEPNORM_ALPHA * x + moe, g_ref[...], b_ref[...])
        return carry
    lax.fori_loop(0, COMBINE_STEPS, combine, 0)

    @pl.when(c < SAMPLE_CHUNK)
    def _():
        prompt_out_copy(c).start()

    @pl.when(c == SAMPLE_CHUNK)
    def _():
        sample_out_copy().start()
        sample_out_copy().wait()


def _moe(x, plan, gates_t, layer, wg, wu, wd, ln_g, ln_b):
    tile_expert, is_new, weight_slot, n_active, fill_lo, fill_hi, pos = plan
    stage = pltpu.VMEM((MOE_TM // SUBLANES, SUBLANES, D_MODEL), jnp.float32)
    per_chunk = lambda c, *_: (c, 0, 0)
    grid_spec = pltpu.PrefetchScalarGridSpec(
        num_scalar_prefetch=6,
        grid=(N_CHUNKS,),
        in_specs=[
            pl.BlockSpec((None, None, 2 * MOE_CHUNK), per_chunk, memory_space=pltpu.SMEM),
            pl.BlockSpec((None, MOE_CHUNK, D_MODEL), per_chunk),
            pl.BlockSpec((None, MOE_CHUNK, 2), per_chunk),
            pl.BlockSpec(memory_space=pl.ANY),
            pl.BlockSpec(memory_space=pl.ANY),
            pl.BlockSpec(memory_space=pl.ANY),
            pl.BlockSpec((1, D_MODEL), lambda c, *_: (0, 0)),
            pl.BlockSpec((1, D_MODEL), lambda c, *_: (0, 0)),
        ],
        out_specs=[pl.BlockSpec(memory_space=pl.ANY), pl.BlockSpec(memory_space=pl.ANY)],
        scratch_shapes=[
            pltpu.SMEM((MOE_ROWS + MOE_TM + SUBLANES,), jnp.int32),
            stage, stage, stage, stage,
            pltpu.VMEM((2 * MOE_CHUNK + SUBLANES, D_MODEL), jnp.float32),
            pltpu.VMEM((2, D_MODEL, D_EXPERT), jnp.bfloat16),
            pltpu.VMEM((2, D_MODEL, D_EXPERT), jnp.bfloat16),
            pltpu.VMEM((2, D_EXPERT, D_MODEL), jnp.bfloat16),
            pltpu.SemaphoreType.DMA((2, 3)),
            pltpu.SemaphoreType.DMA((1,)),
        ],
    )
    return pl.pallas_call(
        functools.partial(_moe_kernel, layer),
        grid_spec=grid_spec,
        out_shape=[jax.ShapeDtypeStruct((BATCH, MOE_CHUNK, D_MODEL), jnp.float32),
                   jax.ShapeDtypeStruct((1, MOE_CHUNK, D_MODEL), jnp.float32)],
        compiler_params=pltpu.CompilerParams(
            dimension_semantics=("arbitrary",), vmem_limit_bytes=VMEM_LIMIT),
        name="moe",
    )(tile_expert, is_new, weight_slot, n_active, fill_lo, fill_hi,
      pos.reshape(N_CHUNKS, 1, 2 * MOE_CHUNK), x, gates_t, wg, wu, wd, ln_g, ln_b)


def _moe_layer(x, wr_t, br, layer, wg, wu, wd, ln_g, ln_b):
    pos, gates, meta = _router(x, wr_t, br)
    per_tile = lambda row: meta[:, row, :MOE_STEPS].reshape(-1)
    per_expert = lambda row: meta[:, row, :N_EXPERTS].reshape(-1)
    plan = (per_tile(META_TILE_EXPERT), per_tile(META_IS_NEW), per_tile(META_WEIGHT_SLOT),
            meta[:, META_N_ACTIVE, 0], per_expert(META_FILL_LO), per_expert(META_FILL_HI), pos)
    return _moe(x, plan, jnp.swapaxes(gates, 1, 2), layer, wg, wu, wd, ln_g, ln_b)


def _rope_tables(pos):
    half = HEAD_DIM // 2
    inv_freq = ROPE_THETA ** (-jnp.arange(half, dtype=jnp.float32) / half)
    ang = pos.astype(jnp.float32)[:, None] * inv_freq[None, :]
    cos = jnp.cos(ang)
    sin = jnp.sin(ang)
    reps = LANES // HEAD_DIM
    cos_t = jnp.tile(jnp.concatenate([cos, cos], axis=1), (1, reps))
    sin_t = jnp.tile(jnp.concatenate([-sin, sin], axis=1), (1, reps))
    return cos_t, sin_t


def _block_diag_tiles(w):
    per = MXU_EDGE // LRU_BLOCK_W
    w4 = w.reshape(LRU_BLOCKS // per, per, LRU_BLOCK_W, LRU_BLOCK_W)
    eye = jnp.eye(per, dtype=w.dtype)
    return jnp.einsum("qaij,ab->qaibj", w4, eye).reshape(LRU_BLOCKS // per, MXU_EDGE, MXU_EDGE)


def kernel(x_prompt, x_sample, state_conv_0, state_h_0, cache_k_1, cache_v_1, state_conv_2, state_h_2,
           cache_k_3, cache_v_3, w_router, b_router, ln_g, ln_b, lru_w_in, lru_conv_w, lru_conv_b,
           lru_w_gates, lru_b_gates, lru_lambda, lru_w_out, attn_w_qkv, attn_b_qkv, attn_sinks, attn_w_o,
           moe_w_gate, moe_w_up, moe_w_down):
    bf = jnp.bfloat16
    in_state = {0: (state_conv_0, state_h_0), 1: (cache_k_1, cache_v_1),
                2: (state_conv_2, state_h_2), 3: (cache_k_3, cache_v_3)}
    nt_s = DEC_BATCH // SAMPLE_SEGS
    kv_lanes = N_KV_HEADS * HEAD_DIM

    wr_t = w_router.T
    br = b_router.reshape(N_EXPERTS, 1)
    cos_p, sin_p = _rope_tables(jnp.arange(SEQ))
    cos_s, sin_s = _rope_tables(PAST_LEN + jnp.arange(DEC_SEQ))
    cos_s = jnp.tile(cos_s, (SAMPLE_SEGS, 1))
    sin_s = jnp.tile(sin_s, (SAMPLE_SEGS, 1))
    wg_all = moe_w_gate.astype(bf)
    wu_all = moe_w_up.astype(bf)
    wd_all = moe_w_down.astype(bf)

    x = x_prompt
    x_s = x_sample.reshape(1, MOE_CHUNK, D_MODEL)
    new_p, new_s = {}, {}
    for i in range(DEPTH):
        slot = i // 2
        g0 = ln_g[i, 0].reshape(1, D_MODEL)
        b0 = ln_b[i, 0].reshape(1, D_MODEL)
        g1 = ln_g[i, 1].reshape(1, D_MODEL)
        b1 = ln_b[i, 1].reshape(1, D_MODEL)
        st_a, st_b = in_state[i]
        if i % 2 == 0:
            gates = lru_w_gates[slot]
            w = (lru_w_in[slot].astype(bf), lru_conv_w[slot], lru_conv_b[slot].reshape(1, D_RNN),
                 _block_diag_tiles(gates[:, :, :LRU_BLOCK_W]).astype(bf),
                 _block_diag_tiles(gates[:, :, LRU_BLOCK_W:]).astype(bf),
                 lru_b_gates[slot][:, :LRU_BLOCK_W].reshape(1, D_RNN),
                 lru_b_gates[slot][:, LRU_BLOCK_W:].reshape(1, D_RNN),
                 lru_lambda[slot].reshape(1, D_RNN), lru_w_out[slot].astype(bf))
            conv_pad = jnp.pad(st_a, ((0, 0), (SUBLANES - (CONV_WIDTH - 1), 0), (0, 0)))
            x1, conv_p, h_p, conv_s, h_s = _lru(
                x, x_s, conv_pad.reshape(nt_s, SAMPLE_SEGS, SUBLANES, D_RNN),
                st_b.reshape(nt_s, SAMPLE_SEGS, D_RNN), w, g0, b0)
            new_p[i] = (conv_p, h_p.reshape(BATCH, D_RNN))
            new_s[i] = (conv_s.reshape(DEC_BATCH, CONV_WIDTH - 1, D_RNN), h_s.reshape(DEC_BATCH, D_RNN))
        else:
            w = (attn_w_qkv[slot].astype(bf), attn_b_qkv[slot].reshape(1, QKV_DIM), attn_sinks[slot],
                 attn_w_o[slot].astype(bf))
            x1, k_p, v_p, k_s, v_s = _swa(
                x, x_s, st_a.reshape(nt_s, SAMPLE_SEGS, WINDOW, kv_lanes),
                st_b.reshape(nt_s, SAMPLE_SEGS, WINDOW, kv_lanes), w, cos_p, sin_p, cos_s, sin_s, g0, b0)
            new_p[i] = (k_p.reshape(BATCH, WINDOW, N_KV_HEADS, HEAD_DIM),
                        v_p.reshape(BATCH, WINDOW, N_KV_HEADS, HEAD_DIM))
            new_s[i] = (k_s.reshape(DEC_BATCH, WINDOW, N_KV_HEADS, HEAD_DIM),
                        v_s.reshape(DEC_BATCH, WINDOW, N_KV_HEADS, HEAD_DIM))
        x, x_s = _moe_layer(x1, wr_t, br, i, wg_all, wu_all, wd_all, g1, b1)
    return (x, x_s.reshape(DEC_BATCH, DEC_SEQ, D_MODEL),
            new_p[0][0], new_s[0][0], new_p[0][1], new_s[0][1],
            new_p[1][0], new_s[1][0], new_p[1][1], new_s[1][1],
            new_p[2][0], new_s[2][0], new_p[2][1], new_s[2][1],
            new_p[3][0], new_s[3][0], new_p[3][1], new_s[3][1])
```

```python
import functools

import jax
import jax.numpy as jnp
from jax import lax
from jax.experimental import pallas as pl
from jax.experimental.pallas import tpu as pltpu

D_MODEL = 1024
BATCH = 8
SEQ = 2048
DEPTH = 4
DEC_BATCH = 32
DEC_SEQ = 64
PAST_LEN = 4096
CHUNK = 64
D_RNN = D_MODEL
CONV_WIDTH = 4
LRU_BLOCKS = 16
LRU_BLOCK_W = D_RNN // LRU_BLOCKS
LRU_C = 8.0
N_HEADS = 16
N_KV_HEADS = 2
HEAD_DIM = 64
KV_GROUP = N_HEADS // N_KV_HEADS
WINDOW = 128
ROPE_THETA = 10000.0
QKV_DIM = (N_HEADS + 2 * N_KV_HEADS) * HEAD_DIM
N_EXPERTS = 16
N_GROUPS = 4
EXPERTS_PER_GROUP = N_EXPERTS // N_GROUPS
D_EXPERT = 512
DEEPNORM_ALPHA = (2.0 * DEPTH) ** 0.25
LN_EPS = 1e-5

MXU_EDGE = 256
LANES = 128
SUBLANES = 8

ROW_TILE = 512
MOE_CHUNK = SEQ
N_CHUNKS = BATCH + DEC_BATCH * DEC_SEQ // MOE_CHUNK
SAMPLE_CHUNK = BATCH
MOE_TM = 256
MOE_ROWS = 2 * MOE_CHUNK + N_EXPERTS * MOE_TM
MOE_TILES = MOE_ROWS // MOE_TM
COMBINE_STEPS = MOE_CHUNK // ROW_TILE
JUNK_PAIR = 2 * MOE_CHUNK
SAMPLE_SEGS = ROW_TILE // DEC_SEQ
NEG_BIG = -1e30
VMEM_LIMIT = 52 * 1024 * 1024

assert DEC_BATCH * DEC_SEQ == MOE_CHUNK and MOE_TILES <= LANES


def _layer_norm(x, g, b):
    mu = jnp.mean(x, -1, keepdims=True)
    xc = x - mu
    var = jnp.mean(xc * xc, -1, keepdims=True)
    return xc * lax.rsqrt(var + LN_EPS) * g + b


def _finish_mixer(x, y, g_ref, b_ref, x1_ref):
    x1_ref[...] = _layer_norm(DEEPNORM_ALPHA * x + y, g_ref[...], b_ref[...])


def _const_spec(shape):
    nd = len(shape)
    return pl.BlockSpec(shape, lambda *_: (0,) * nd)


_ACT_SHAPE = jax.ShapeDtypeStruct((N_CHUNKS, MOE_CHUNK, D_MODEL), jnp.float32)


def _lru_tile(x, ext_scr, h_in, n_seg, seg, win_ref, cw_ref, cb_ref, wr_ref, wi_ref,
              br_ref, bi_ref, lam_ref, wout_ref):
    rows = n_seg * seg
    xb = x.astype(jnp.bfloat16)
    rec = jnp.dot(xb, win_ref[:, D_RNN:], preferred_element_type=jnp.float32)
    gate_in = jnp.dot(xb, win_ref[:, :D_RNN], preferred_element_type=jnp.float32)
    cw = cw_ref[...]
    xcs = []
    for s in range(n_seg):
        ext_scr[s, SUBLANES:SUBLANES + seg, :] = rec[s * seg:(s + 1) * seg]
        xc = cb_ref[...] + cw[3:4] * rec[s * seg:(s + 1) * seg]
        for k in range(CONV_WIDTH - 1):
            off = SUBLANES - (CONV_WIDTH - 1) + k
            xc = xc + cw[k:k + 1] * ext_scr[s, off:off + seg, :]
        xcs.append(xc)
    xc = xcs[0] if n_seg == 1 else jnp.concatenate(xcs, axis=0)

    xcb = xc.astype(jnp.bfloat16)
    r_parts, i_parts = [], []
    for q in range(D_RNN // MXU_EDGE):
        blk = xcb[:, q * MXU_EDGE:(q + 1) * MXU_EDGE]
        r_parts.append(jnp.dot(blk, wr_ref[q], preferred_element_type=jnp.float32))
        i_parts.append(jnp.dot(blk, wi_ref[q], preferred_element_type=jnp.float32))
    r = jax.nn.sigmoid(jnp.concatenate(r_parts, axis=1) + br_ref[...])
    ig = jax.nn.sigmoid(jnp.concatenate(i_parts, axis=1) + bi_ref[...])

    z = -lam_ref[...]
    softplus = jnp.maximum(z, 0.0) + jnp.log1p(jnp.exp(-jnp.abs(z)))
    log_a = (-LRU_C * r) * softplus
    a = jnp.exp(log_a)
    b = jnp.sqrt(-jnp.tanh(log_a) * (a * a + 1.0)) * (ig * xc)

    sub = lax.broadcasted_iota(jnp.int32, (rows, D_RNN), 0) & (SUBLANES - 1)
    step = 1
    while step < SUBLANES:
        keep = sub >= step
        a_prev = pltpu.roll(a, step, 0)
        b_prev = pltpu.roll(b, step, 0)
        b = jnp.where(keep, a * b_prev + b, b)
        a = jnp.where(keep, a * a_prev, a)
        step *= 2
    groups_per_seg = seg // SUBLANES
    h_blocks, h_last = [], []
    carry = None
    for gi in range(rows // SUBLANES):
        if gi % groups_per_seg == 0:
            carry = h_in[gi // groups_per_seg]
        lo = gi * SUBLANES
        hb = b[lo:lo + SUBLANES] + a[lo:lo + SUBLANES] * carry
        carry = hb[SUBLANES - 1:SUBLANES]
        h_blocks.append(hb)
        if gi % groups_per_seg == groups_per_seg - 1:
            h_last.append(carry)
    h = jnp.concatenate(h_blocks, axis=0)

    y = h * jax.nn.gelu(gate_in, approximate=True)
    out = jnp.dot(y.astype(jnp.bfloat16), wout_ref[...], preferred_element_type=jnp.float32)
    return out, rec, h_last


def _lru_kernel(xp_ref, xs_ref, conv_in_ref, h0_ref, win_ref, cw_ref, cb_ref, wr_ref, wi_ref, br_ref,
                bi_ref, lam_ref, wout_ref, g_ref, b_ref, x1_ref, conv_p_ref, h_p_ref, conv_s_ref, h_s_ref,
                ext_p_scr, h_scr, ext_s_scr):
    c = pl.program_id(0)
    t = pl.program_id(1)
    weights = (win_ref, cw_ref, cb_ref, wr_ref, wi_ref, br_ref, bi_ref, lam_ref, wout_ref)

    @pl.when(c < BATCH)
    def _():
        @pl.when(t == 0)
        def _():
            ext_p_scr[0, 0:SUBLANES, :] = jnp.zeros((SUBLANES, D_RNN), jnp.float32)
            h_scr[...] = jnp.zeros_like(h_scr)

        x = xp_ref[...]
        out, rec, h_last = _lru_tile(x, ext_p_scr, [h_scr[...]], 1, ROW_TILE, *weights)
        ext_p_scr[0, 0:SUBLANES, :] = rec[ROW_TILE - SUBLANES:]
        h_scr[...] = h_last[0]
        conv_p_ref[...] = rec[ROW_TILE - (CONV_WIDTH - 1):]
        h_p_ref[...] = h_last[0]
        _finish_mixer(x, out, g_ref, b_ref, x1_ref)

    @pl.when(c == SAMPLE_CHUNK)
    def _():
        for s in range(SAMPLE_SEGS):
            ext_s_scr[s, 0:SUBLANES, :] = conv_in_ref[s]
        h_in = [h0_ref[s:s + 1, :] for s in range(SAMPLE_SEGS)]
        x = xs_ref[...]
        out, rec, h_last = _lru_tile(x, ext_s_scr, h_in, SAMPLE_SEGS, DEC_SEQ, *weights)
        for s in range(SAMPLE_SEGS):
            end = (s + 1) * DEC_SEQ
            conv_s_ref[s] = rec[end - (CONV_WIDTH - 1):end]
            h_s_ref[s:s + 1, :] = h_last[s]
        _finish_mixer(x, out, g_ref, b_ref, x1_ref)


_TILES_PER_CHUNK = MOE_CHUNK // ROW_TILE


def _prompt_tile(c, t):
    return jnp.where(c < BATCH, t, _TILES_PER_CHUNK - 1)


def _sample_tile(c, t):
    return jnp.where(c == SAMPLE_CHUNK, t, 0)


def _mixer_act_specs():
    return [
        pl.BlockSpec((None, ROW_TILE, D_MODEL), lambda c, t: (jnp.minimum(c, BATCH - 1), _prompt_tile(c, t), 0)),
        pl.BlockSpec((None, ROW_TILE, D_MODEL), lambda c, t: (0, _sample_tile(c, t), 0)),
    ]


_MIXER_GRID = (N_CHUNKS, _TILES_PER_CHUNK)
_PER_PROMPT = lambda c, t: (jnp.minimum(c, BATCH - 1), 0, 0)
_PER_SAMPLE3 = lambda c, t: (_sample_tile(c, t), 0, 0)
_PER_SAMPLE4 = lambda c, t: (_sample_tile(c, t), 0, 0, 0)


def _lru(xp, xs, conv_pad, h0, w, ln_g, ln_b):
    nt_s = DEC_BATCH // SAMPLE_SEGS
    return pl.pallas_call(
        _lru_kernel,
        grid=_MIXER_GRID,
        in_specs=_mixer_act_specs() + [
            pl.BlockSpec((None, SAMPLE_SEGS, SUBLANES, D_RNN), _PER_SAMPLE4),
            pl.BlockSpec((None, SAMPLE_SEGS, D_RNN), _PER_SAMPLE3),
            _const_spec((D_MODEL, 2 * D_RNN)),
            _const_spec((CONV_WIDTH, D_RNN)),
            _const_spec((1, D_RNN)),
            _const_spec((D_RNN // MXU_EDGE, MXU_EDGE, MXU_EDGE)),
            _const_spec((D_RNN // MXU_EDGE, MXU_EDGE, MXU_EDGE)),
            _const_spec((1, D_RNN)),
            _const_spec((1, D_RNN)),
            _const_spec((1, D_RNN)),
            _const_spec((D_RNN, D_MODEL)),
            _const_spec((1, D_MODEL)),
            _const_spec((1, D_MODEL)),
        ],
        out_specs=[
            pl.BlockSpec((None, ROW_TILE, D_MODEL), lambda c, t: (c, t, 0)),
            pl.BlockSpec((None, CONV_WIDTH - 1, D_RNN), _PER_PROMPT),
            pl.BlockSpec((None, 1, D_RNN), _PER_PROMPT),
            pl.BlockSpec((None, SAMPLE_SEGS, CONV_WIDTH - 1, D_RNN), _PER_SAMPLE4),
            pl.BlockSpec((None, SAMPLE_SEGS, D_RNN), _PER_SAMPLE3),
        ],
        out_shape=[
            _ACT_SHAPE,
            jax.ShapeDtypeStruct((BATCH, CONV_WIDTH - 1, D_RNN), jnp.float32),
            jax.ShapeDtypeStruct((BATCH, 1, D_RNN), jnp.float32),
            jax.ShapeDtypeStruct((nt_s, SAMPLE_SEGS, CONV_WIDTH - 1, D_RNN), jnp.float32),
            jax.ShapeDtypeStruct((nt_s, SAMPLE_SEGS, D_RNN), jnp.float32),
        ],
        scratch_shapes=[
            pltpu.VMEM((1, SUBLANES + ROW_TILE, D_RNN), jnp.float32),
            pltpu.VMEM((1, D_RNN), jnp.float32),
            pltpu.VMEM((SAMPLE_SEGS, SUBLANES + DEC_SEQ, D_RNN), jnp.float32),
        ],
        compiler_params=pltpu.CompilerParams(
            dimension_semantics=("arbitrary", "arbitrary"), vmem_limit_bytes=VMEM_LIMIT),
        name="lru",
    )(xp, xs, conv_pad, h0, *w, ln_g, ln_b)


def _rope(x, cos, sin_signed):
    lane = lax.broadcasted_iota(jnp.int32, (x.shape[0], LANES), 1)
    first_half = (lane & (HEAD_DIM - 1)) < HEAD_DIM // 2
    cols = []
    for c in range(x.shape[1] // LANES):
        xc = x[:, c * LANES:(c + 1) * LANES]
        partner = jnp.where(first_half, pltpu.roll(xc, LANES - HEAD_DIM // 2, 1),
                            pltpu.roll(xc, HEAD_DIM // 2, 1))
        cols.append(xc * cos + partner * sin_signed)
    return cols[0] if len(cols) == 1 else jnp.concatenate(cols, axis=1)


def _dup_heads(kv):
    lane = lax.broadcasted_iota(jnp.int32, kv.shape, 1)
    low = lane < HEAD_DIM
    swapped = pltpu.roll(kv, HEAD_DIM, 1)
    return (jnp.where(low, kv, swapped).astype(jnp.bfloat16),
            jnp.where(low, swapped, kv).astype(jnp.bfloat16))


def _attend(chunks, sinks_ref):
    lane = lax.broadcasted_iota(jnp.int32, (CHUNK, LANES), 1)
    low = lane < HEAD_DIM
    zero = jnp.zeros((CHUNK, LANES), jnp.bfloat16)
    cols_per_group = KV_GROUP * HEAD_DIM // LANES
    units = [(ci, g) for ci in range(len(chunks)) for g in range(N_KV_HEADS)]

    scores = {}
    for ci, g in units:
        q, k_wins, _, valid = chunks[ci]
        stacked = []
        for c in range(g * cols_per_group, (g + 1) * cols_per_group):
            qc = q[:, c * LANES:(c + 1) * LANES]
            stacked.append(jnp.where(low, qc, zero))
            stacked.append(jnp.where(low, zero, qc))
        qst = jnp.concatenate(stacked, axis=0)
        s = lax.dot_general(qst, k_wins[g], (((1,), (1,)), ((), ())),
                            preferred_element_type=jnp.float32)
        scores[ci, g] = s if valid is None else jnp.where(valid, s, NEG_BIG)

    probs = {}
    for ci, g in units:
        s = scores[ci, g]
        heads = []
        for hh in range(KV_GROUP):
            sh = s[hh * CHUNK:(hh + 1) * CHUNK]
            sink = sinks_ref[g * KV_GROUP + hh]
            m = jnp.maximum(jnp.max(sh, -1, keepdims=True), sink)
            p = jnp.exp(sh - m)
            denom = jnp.sum(p, -1, keepdims=True) + jnp.exp(sink - m)
            heads.append((p * (1.0 / denom)).astype(jnp.bfloat16))
        probs[ci, g] = jnp.concatenate(heads, axis=0)

    out_cols = {ci: [] for ci in range(len(chunks))}
    for ci, g in units:
        o = jnp.dot(probs[ci, g], chunks[ci][2][g], preferred_element_type=jnp.float32)
        for j in range(cols_per_group):
            oa = o[(2 * j) * CHUNK:(2 * j + 1) * CHUNK]
            ob = o[(2 * j + 1) * CHUNK:(2 * j + 2) * CHUNK]
            out_cols[ci].append(jnp.where(low, oa, ob))
    return [jnp.concatenate(out_cols[ci], axis=1) for ci in range(len(chunks))]


def _project_qkv(x, wqkv_ref, bqkv_ref, cos, sin):
    nq = N_HEADS * HEAD_DIM
    nk = N_KV_HEADS * HEAD_DIM
    xb = x.astype(jnp.bfloat16)
    kv = jnp.dot(xb, wqkv_ref[:, nq:], preferred_element_type=jnp.float32) + bqkv_ref[:, nq:]
    k = _rope(kv[:, :nk], cos, sin)
    v = kv[:, nk:]
    q = jnp.dot(xb, wqkv_ref[:, :nq], preferred_element_type=jnp.float32) + bqkv_ref[:, :nq]
    q = _rope(q, cos, sin)
    qs = (q * (HEAD_DIM ** -0.5)).astype(jnp.bfloat16)
    return qs, k, v


SUB_TILE = 256


def _sub_tiles():
    return [pl.ds(i * SUB_TILE, SUB_TILE) for i in range(ROW_TILE // SUB_TILE)]


def _project_out(xs, subs, o, wo_ref, g_ref, b_ref, x1_ref):
    for x, sub, i in zip(xs, subs, range(len(subs))):
        out = jnp.dot(o[i * SUB_TILE:(i + 1) * SUB_TILE].astype(jnp.bfloat16), wo_ref[...],
                      preferred_element_type=jnp.float32)
        x1_ref[sub, :] = _layer_norm(DEEPNORM_ALPHA * x + out, g_ref[...], b_ref[...])


def _swa_kernel(xp_ref, xs_ref, ck_ref, cv_ref, wqkv_ref, bqkv_ref, cosp_ref, sinp_ref, coss_ref, sins_ref,
                sinks_ref, wo_ref, g_ref, b_ref, x1_ref, kp_ref, vp_ref, ks_ref, vs_ref,
                k0_scr, k1_scr, v0_scr, v1_scr):
    c = pl.program_id(0)
    t = pl.program_id(1)
    scrs = (k0_scr, k1_scr, v0_scr, v1_scr)

    @pl.when(c < BATCH)
    def _():
        @pl.when(t == 0)
        def _():
            for scr in scrs:
                scr[0:WINDOW, :] = jnp.zeros((WINDOW, LANES), jnp.bfloat16)

        subs = _sub_tiles()
        xs_ = [xp_ref[sub, :] for sub in subs]
        proj = [_project_qkv(x, wqkv_ref, bqkv_ref, cosp_ref[sub, :], sinp_ref[sub, :]) for x, sub in zip(xs_, subs)]
        qs = jnp.concatenate([p[0] for p in proj], axis=0)
        k = jnp.concatenate([p[1] for p in proj], axis=0)
        v = jnp.concatenate([p[2] for p in proj], axis=0)
        kp_ref[...] = k[ROW_TILE - WINDOW:]
        vp_ref[...] = v[ROW_TILE - WINDOW:]
        k0, k1 = _dup_heads(k)
        v0, v1 = _dup_heads(v)
        for scr, val in zip(scrs, (k0, k1, v0, v1)):
            scr[WINDOW:WINDOW + ROW_TILE, :] = val

        span = WINDOW + CHUNK
        key_pos = lax.broadcasted_iota(jnp.int32, (1, span), 1)
        chunks = []
        for ci in range(ROW_TILE // CHUNK):
            lo = ci * CHUNK
            valid = (t * ROW_TILE + lo - WINDOW + key_pos) >= 0
            k_wins = (k0_scr[lo:lo + span, :], k1_scr[lo:lo + span, :])
            v_wins = (v0_scr[lo:lo + span, :], v1_scr[lo:lo + span, :])
            chunks.append((qs[lo:lo + CHUNK], k_wins, v_wins, valid))
        o = jnp.concatenate(_attend(chunks, sinks_ref), axis=0)
        for scr in scrs:
            scr[0:WINDOW, :] = scr[ROW_TILE:ROW_TILE + WINDOW, :]
        _project_out(xs_, subs, o, wo_ref, g_ref, b_ref, x1_ref)

    @pl.when(c == SAMPLE_CHUNK)
    def _():
        subs = _sub_tiles()
        xs_ = [xs_ref[sub, :] for sub in subs]
        proj = [_project_qkv(x, wqkv_ref, bqkv_ref, coss_ref[sub, :], sins_ref[sub, :]) for x, sub in zip(xs_, subs)]
        qs = jnp.concatenate([p[0] for p in proj], axis=0)
        k = jnp.concatenate([p[1] for p in proj], axis=0)
        v = jnp.concatenate([p[2] for p in proj], axis=0)
        knew = _dup_heads(k)
        vnew = _dup_heads(v)
        chunks = []
        for s in range(SAMPLE_SEGS):
            lo = s * DEC_SEQ
            ck = ck_ref[s]
            cv = cv_ref[s]
            kold = _dup_heads(ck)
            vold = _dup_heads(cv)
            k_wins = [jnp.concatenate([kold[g], knew[g][lo:lo + DEC_SEQ]], axis=0) for g in range(N_KV_HEADS)]
            v_wins = [jnp.concatenate([vold[g], vnew[g][lo:lo + DEC_SEQ]], axis=0) for g in range(N_KV_HEADS)]
            chunks.append((qs[lo:lo + DEC_SEQ], k_wins, v_wins, None))
            ks_ref[s, 0:WINDOW - DEC_SEQ, :] = ck[DEC_SEQ:]
            ks_ref[s, WINDOW - DEC_SEQ:WINDOW, :] = k[lo:lo + DEC_SEQ]
            vs_ref[s, 0:WINDOW - DEC_SEQ, :] = cv[DEC_SEQ:]
            vs_ref[s, WINDOW - DEC_SEQ:WINDOW, :] = v[lo:lo + DEC_SEQ]
        o = jnp.concatenate(_attend(chunks, sinks_ref), axis=0)
        _project_out(xs_, subs, o, wo_ref, g_ref, b_ref, x1_ref)


def _swa(xp, xs, ck, cv, w, cos_p, sin_p, cos_s, sin_s, ln_g, ln_b):
    wqkv, bqkv, sinks, wo = w
    nt_s = DEC_BATCH // SAMPLE_SEGS
    kv_lanes = N_KV_HEADS * HEAD_DIM
    cache_spec = pl.BlockSpec((None, SAMPLE_SEGS, WINDOW, kv_lanes), _PER_SAMPLE4)
    prompt_table = pl.BlockSpec((ROW_TILE, LANES), lambda c, t: (_prompt_tile(c, t), 0))
    kv_prompt = pl.BlockSpec((None, WINDOW, kv_lanes), _PER_PROMPT)
    return pl.pallas_call(
        _swa_kernel,
        grid=_MIXER_GRID,
        in_specs=_mixer_act_specs() + [
            cache_spec, cache_spec,
            _const_spec((D_MODEL, QKV_DIM)),
            _const_spec((1, QKV_DIM)),
            prompt_table, prompt_table,
            _const_spec((ROW_TILE, LANES)), _const_spec((ROW_TILE, LANES)),
            pl.BlockSpec(memory_space=pltpu.SMEM),
            _const_spec((N_HEADS * HEAD_DIM, D_MODEL)),
            _const_spec((1, D_MODEL)),
            _const_spec((1, D_MODEL)),
        ],
        out_specs=[
            pl.BlockSpec((None, ROW_TILE, D_MODEL), lambda c, t: (c, t, 0)),
            kv_prompt, kv_prompt, cache_spec, cache_spec,
        ],
        out_shape=[
            _ACT_SHAPE,
            jax.ShapeDtypeStruct((BATCH, WINDOW, kv_lanes), jnp.float32),
            jax.ShapeDtypeStruct((BATCH, WINDOW, kv_lanes), jnp.float32),
            jax.ShapeDtypeStruct((nt_s, SAMPLE_SEGS, WINDOW, kv_lanes), jnp.float32),
            jax.ShapeDtypeStruct((nt_s, SAMPLE_SEGS, WINDOW, kv_lanes), jnp.float32),
        ],
        scratch_shapes=[pltpu.VMEM((WINDOW + ROW_TILE, LANES), jnp.bfloat16) for _ in range(4)],
        compiler_params=pltpu.CompilerParams(
            dimension_semantics=("arbitrary", "arbitrary"), vmem_limit_bytes=VMEM_LIMIT),
        name="swa",
    )(xp, xs, ck, cv, wqkv, bqkv, cos_p, sin_p, cos_s, sin_s, sinks, wo, ln_g, ln_b)


META_TILE_EXPERT, META_N_ACTIVE, META_FILL_LO, META_FILL_HI, META_IS_NEW, META_WEIGHT_SLOT = range(6)


def _router_kernel(x_ref, wr_ref, br_ref, pos_ref, gate_ref, meta_ref):
    f32 = jnp.float32
    x = x_ref[...]
    w = wr_ref[...]
    xh = x.astype(jnp.bfloat16)
    xl = (x - xh.astype(f32)).astype(jnp.bfloat16)
    wh = w.astype(jnp.bfloat16)
    wl = (w - wh.astype(f32)).astype(jnp.bfloat16)
    nt = (((1,), (1,)), ((), ()))
    logits = (lax.dot_general(wh, xh, nt, preferred_element_type=f32)
              + lax.dot_general(wl, xh, nt, preferred_element_type=f32)
              + lax.dot_general(wh, xl, nt, preferred_element_type=f32))
    m = jnp.max(logits, axis=0, keepdims=True)
    e = jnp.exp(logits - m)
    probs = e / jnp.sum(e, axis=0, keepdims=True)
    sel = probs + br_ref[...]
    p = [probs[i:i + 1] for i in range(N_EXPERTS)]
    s = [sel[i:i + 1] for i in range(N_EXPERTS)]

    def first_argmax(vals):
        best, arg = vals[0], jnp.zeros_like(vals[0], dtype=jnp.int32)
        for i in range(1, len(vals)):
            better = vals[i] > best
            best = jnp.where(better, vals[i], best)
            arg = jnp.where(better, i, arg)
        return arg

    group_scores = []
    for gidx in range(N_GROUPS):
        v = s[gidx * EXPERTS_PER_GROUP:(gidx + 1) * EXPERTS_PER_GROUP]
        best = v[0] + v[1]
        for i in range(EXPERTS_PER_GROUP):
            for j in range(i + 1, EXPERTS_PER_GROUP):
                if (i, j) != (0, 1):
                    best = jnp.maximum(best, v[i] + v[j])
        group_scores.append(best)
    best_group = first_argmax(group_scores)
    neg_inf = jnp.full_like(s[0], -jnp.inf)
    masked = [jnp.where(best_group == (i // EXPERTS_PER_GROUP), s[i], neg_inf) for i in range(N_EXPERTS)]
    idx1 = first_argmax(masked)
    masked2 = [jnp.where(idx1 == i, neg_inf, masked[i]) for i in range(N_EXPERTS)]
    idx2 = first_argmax(masked2)
    zero = jnp.zeros_like(p[0])
    w1, w2 = zero, zero
    for i in range(N_EXPERTS):
        w1 = w1 + jnp.where(idx1 == i, p[i], zero)
        w2 = w2 + jnp.where(idx2 == i, p[i], zero)
    tot = w1 + w2
    gate_ref[0:1, :] = w1 / tot
    gate_ref[1:2, :] = w2 / tot

    expert = lax.broadcasted_iota(jnp.int32, (N_EXPERTS, MOE_CHUNK), 0)
    strict_upper = (lax.broadcasted_iota(jnp.int32, (LANES, LANES), 0)
                    < lax.broadcasted_iota(jnp.int32, (LANES, LANES), 1)).astype(jnp.bfloat16)
    carry = jnp.zeros((N_EXPERTS, 1), f32)
    onehots, ranks = [], []
    for idx in (idx1, idx2):
        onehot = (expert == idx).astype(f32)
        before = []
        for blk in range(MOE_CHUNK // LANES):
            oh = onehot[:, blk * LANES:(blk + 1) * LANES]
            before.append(jnp.dot(oh.astype(jnp.bfloat16), strict_upper, preferred_element_type=f32) + carry)
            carry = carry + jnp.sum(oh, axis=1, keepdims=True)
        onehots.append(onehot)
        ranks.append(jnp.sum(onehot * jnp.concatenate(before, axis=1), axis=0, keepdims=True))
    counts = carry
    padded = jnp.floor((counts + (MOE_TM - 1)) * (1.0 / MOE_TM)) * MOE_TM
    strict_lower = (lax.broadcasted_iota(jnp.int32, (N_EXPERTS, N_EXPERTS), 0)
                    > lax.broadcasted_iota(jnp.int32, (N_EXPERTS, N_EXPERTS), 1)).astype(jnp.bfloat16)
    starts = jnp.dot(strict_lower, jnp.broadcast_to(padded, (N_EXPERTS, LANES)).astype(jnp.bfloat16),
                     preferred_element_type=f32)[:, 0:1]
    ends = starts + padded
    for k in range(2):
        pos = jnp.sum(onehots[k] * starts, axis=0, keepdims=True) + ranks[k]
        pos_ref[k:k + 1, :] = pos.astype(jnp.int32)

    lane = lax.broadcasted_iota(jnp.int32, (N_EXPERTS, LANES), 1)
    sub = lax.broadcasted_iota(jnp.int32, (N_EXPERTS, LANES), 0)
    tile_start = (lane * MOE_TM).astype(f32)
    tile_expert = jnp.sum((ends <= tile_start).astype(f32), axis=0, keepdims=True)
    tile_expert = jnp.minimum(tile_expert, N_EXPERTS - 1.0)
    on_diag = sub == lane

    def to_lanes(col):
        return jnp.sum(jnp.where(on_diag, col, 0.0), axis=0, keepdims=True)

    lane1 = lax.broadcasted_iota(jnp.int32, (1, LANES), 1)
    is_new = jnp.where(jnp.logical_or(lane1 == 0, tile_expert != pltpu.roll(tile_expert, 1, 1)), 1.0, 0.0)
    upper_incl = (lax.broadcasted_iota(jnp.int32, (LANES, LANES), 0)
                  <= lax.broadcasted_iota(jnp.int32, (LANES, LANES), 1)).astype(jnp.bfloat16)
    runs = jnp.dot(jnp.broadcast_to(is_new, (SUBLANES, LANES)).astype(jnp.bfloat16), upper_incl,
                   preferred_element_type=f32)[0:1]
    weight_slot = (runs.astype(jnp.int32) - 1) & 1

    n_active = jnp.broadcast_to(ends[N_EXPERTS - 1:N_EXPERTS] * (1.0 / MOE_TM), (1, LANES))
    meta_ref[...] = jnp.zeros(meta_ref.shape, jnp.int32)
    meta_ref[META_IS_NEW:META_IS_NEW + 1, :] = is_new.astype(jnp.int32)
    meta_ref[META_WEIGHT_SLOT:META_WEIGHT_SLOT + 1, :] = weight_slot
    meta_ref[META_TILE_EXPERT:META_TILE_EXPERT + 1, :] = tile_expert.astype(jnp.int32)
    meta_ref[META_N_ACTIVE:META_N_ACTIVE + 1, :] = n_active.astype(jnp.int32)
    meta_ref[META_FILL_LO:META_FILL_LO + 1, :] = to_lanes(starts + counts).astype(jnp.int32)
    meta_ref[META_FILL_HI:META_FILL_HI + 1, :] = to_lanes(ends).astype(jnp.int32)


def _router(x, wr_t, br):
    per_chunk = lambda c: (c, 0, 0)
    return pl.pallas_call(
        _router_kernel,
        grid=(N_CHUNKS,),
        in_specs=[
            pl.BlockSpec((None, MOE_CHUNK, D_MODEL), per_chunk),
            _const_spec((N_EXPERTS, D_MODEL)),
            _const_spec((N_EXPERTS, 1)),
        ],
        out_specs=[
            pl.BlockSpec((None, 2, MOE_CHUNK), per_chunk),
            pl.BlockSpec((None, 2, MOE_CHUNK), per_chunk),
            pl.BlockSpec((None, SUBLANES, LANES), per_chunk),
        ],
        out_shape=[
            jax.ShapeDtypeStruct((N_CHUNKS, 2, MOE_CHUNK), jnp.int32),
            jax.ShapeDtypeStruct((N_CHUNKS, 2, MOE_CHUNK), jnp.float32),
            jax.ShapeDtypeStruct((N_CHUNKS, SUBLANES, LANES), jnp.int32),
        ],
        compiler_params=pltpu.CompilerParams(
            dimension_semantics=("parallel",), vmem_limit_bytes=VMEM_LIMIT),
        name="router",
    )(x, wr_t, br)


def _gather_rows(pair_scr, x_ref, xs_scr, row0, j):
    tok = pair_scr[row0 + j] & (MOE_CHUNK - 1)
    xs_scr[j // SUBLANES, (j % SUBLANES):(j % SUBLANES) + 1, :] = x_ref[pl.ds(tok, 1), :]


def _scatter_rows(pair_scr, y_scr, yo_scr, row0, j):
    y_scr[pl.ds(pair_scr[row0 + j], 1), :] = yo_scr[j // SUBLANES, (j % SUBLANES):(j % SUBLANES) + 1, :]


def _moe_kernel(layer, meta_ref, pos_ref, x_ref, gate_ref,
                wg_hbm, wu_hbm, wd_hbm, g_ref, b_ref, outp_hbm, outs_hbm,
                pair_scr, xs_a, xs_b, yo_a, yo_b, y_scr, wg_buf, wu_buf, wd_buf, w_sem, out_sem):
    c = pl.program_id(0)

    def plan(row, i):
        return meta_ref[(c * SUBLANES + row) * LANES + i]

    n_act = plan(META_N_ACTIVE, 0)

    def weight_copies(t):
        e, s = plan(META_TILE_EXPERT, t), plan(META_WEIGHT_SLOT, t)
        return (pltpu.make_async_copy(wg_hbm.at[layer, e], wg_buf.at[s], w_sem.at[s, 0]),
                pltpu.make_async_copy(wu_hbm.at[layer, e], wu_buf.at[s], w_sem.at[s, 1]),
                pltpu.make_async_copy(wd_hbm.at[layer, e], wd_buf.at[s], w_sem.at[s, 2]))

    def prompt_out_copy(chunk):
        return pltpu.make_async_copy(y_scr.at[pl.ds(0, MOE_CHUNK)], outp_hbm.at[chunk], out_sem.at[0])

    def sample_out_copy():
        return pltpu.make_async_copy(y_scr.at[pl.ds(0, MOE_CHUNK)], outs_hbm.at[0], out_sem.at[0])

    for cp in weight_copies(0):
        cp.start()

    def fill8(g, base):
        for u in range(SUBLANES):
            pair_scr[base + g * SUBLANES + u] = JUNK_PAIR
        return base

    def fill_expert(e, carry):
        lo, hi = plan(META_FILL_LO, e), plan(META_FILL_HI, e)
        lax.fori_loop(0, (hi - lo + SUBLANES - 1) >> 3, fill8, lo)
        return carry
    lax.fori_loop(0, N_EXPERTS, fill_expert, 0)
    lax.fori_loop(0, MOE_TM // SUBLANES, fill8, plan(META_FILL_HI, N_EXPERTS - 1))

    def invert(t, carry):
        pair_scr[pos_ref[t]] = t
        pair_scr[pos_ref[MOE_CHUNK + t]] = MOE_CHUNK + t
        return carry
    lax.fori_loop(0, MOE_CHUNK, invert, 0, unroll=8)

    def gather0(gi, carry):
        for u in range(SUBLANES):
            tok = pair_scr[gi * SUBLANES + u] & (MOE_CHUNK - 1)
            xs_a[gi, u:u + 1, :] = x_ref[pl.ds(tok, 1), :]
        return carry
    lax.fori_loop(0, MOE_TM // SUBLANES, gather0, 0)

    @pl.when(c == 0)
    def _():
        yo_b[...] = jnp.zeros_like(yo_b)

    @pl.when(c > 0)
    def _():
        prompt_out_copy(c - 1).wait()

    def expert_step(t, s, xs_cur, yo_cur, xs_nxt, yo_prv):
        row_nxt = (t + 1) * MOE_TM
        row_prv = jnp.maximum(t - 1, 0) * MOE_TM
        for j in range(MOE_TM):
            _gather_rows(pair_scr, x_ref, xs_nxt, row_nxt, j)
        xb = xs_cur[...].reshape(MOE_TM, D_MODEL).astype(jnp.bfloat16)
        hg = jnp.dot(xb, wg_buf[s], preferred_element_type=jnp.float32)
        hu = jnp.dot(xb, wu_buf[s], preferred_element_type=jnp.float32)
        act = (jax.nn.silu(hg) * hu).astype(jnp.bfloat16)
        yo = jnp.dot(act, wd_buf[s], preferred_element_type=jnp.float32)
        yo_cur[...] = yo.reshape(MOE_TM // SUBLANES, SUBLANES, D_MODEL)
        for j in range(MOE_TM):
            _scatter_rows(pair_scr, y_scr, yo_prv, row_prv, j)

    def tile_iteration(t, carry):
        @pl.when(plan(META_IS_NEW, t) == 1)
        def _():
            for cp in weight_copies(t):
                cp.wait()

        @pl.when(jnp.logical_and(t + 1 < n_act, plan(META_IS_NEW, t + 1) == 1))
        def _():
            for cp in weight_copies(t + 1):
                cp.start()

        for s in range(2):
            in_slot = plan(META_WEIGHT_SLOT, t) == s

            @pl.when(jnp.logical_and((t & 1) == 0, in_slot))
            def _():
                expert_step(t, s, xs_a, yo_a, xs_b, yo_b)

            @pl.when(jnp.logical_and((t & 1) == 1, in_slot))
            def _():
                expert_step(t, s, xs_b, yo_b, xs_a, yo_a)
        return carry
    lax.fori_loop(0, n_act, tile_iteration, 0)

    def drain(yo_last):
        row_last = (n_act - 1) * MOE_TM

        def scatter(gi, carry):
            for u in range(SUBLANES):
                y_scr[pl.ds(pair_scr[row_last + gi * SUBLANES + u], 1), :] = yo_last[gi, u:u + 1, :]
            return carry
        lax.fori_loop(0, MOE_TM // SUBLANES, scatter, 0)

    @pl.when((n_act & 1) == 1)
    def _():
        drain(yo_a)

    @pl.when((n_act & 1) == 0)
    def _():
        drain(yo_b)

    def combine(k, carry):
        lo = pl.multiple_of(k * ROW_TILE, ROW_TILE)
        gates = gate_ref[pl.ds(lo, ROW_TILE), :]
        moe = (gates[:, 0:1] * y_scr[pl.ds(lo, ROW_TILE), :]
               + gates[:, 1:2] * y_scr[pl.ds(MOE_CHUNK + lo, ROW_TILE), :])
        x = x_ref[pl.ds(lo, ROW_TILE), :]
        y_scr[pl.ds(lo, ROW_TILE), :] = _layer_norm(DEEPNORM_ALPHA * x + moe, g_ref[...], b_ref[...])
        return carry
    lax.fori_loop(0, COMBINE_STEPS, combine, 0)

    @pl.when(c < SAMPLE_CHUNK)
    def _():
        prompt_out_copy(c).start()

    @pl.when(c == SAMPLE_CHUNK)
    def _():
        sample_out_copy().start()
        sample_out_copy().wait()


def _moe(x, meta, pos, gates_t, layer, wg, wu, wd, ln_g, ln_b):
    stage = pltpu.VMEM((MOE_TM // SUBLANES, SUBLANES, D_MODEL), jnp.float32)
    per_chunk = lambda c, *_: (c, 0, 0)
    grid_spec = pltpu.PrefetchScalarGridSpec(
        num_scalar_prefetch=1,
        grid=(N_CHUNKS,),
        in_specs=[
            pl.BlockSpec((None, None, 2 * MOE_CHUNK), per_chunk, memory_space=pltpu.SMEM),
            pl.BlockSpec((None, MOE_CHUNK, D_MODEL), per_chunk),
            pl.BlockSpec((None, MOE_CHUNK, 2), per_chunk),
            pl.BlockSpec(memory_space=pl.ANY),
            pl.BlockSpec(memory_space=pl.ANY),
            pl.BlockSpec(memory_space=pl.ANY),
            pl.BlockSpec((1, D_MODEL), lambda c, *_: (0, 0)),
            pl.BlockSpec((1, D_MODEL), lambda c, *_: (0, 0)),
        ],
        out_specs=[pl.BlockSpec(memory_space=pl.ANY), pl.BlockSpec(memory_space=pl.ANY)],
        scratch_shapes=[
            pltpu.SMEM((MOE_ROWS + MOE_TM + SUBLANES,), jnp.int32),
            stage, stage, stage, stage,
            pltpu.VMEM((2 * MOE_CHUNK + SUBLANES, D_MODEL), jnp.float32),
            pltpu.VMEM((2, D_MODEL, D_EXPERT), jnp.bfloat16),
            pltpu.VMEM((2, D_MODEL, D_EXPERT), jnp.bfloat16),
            pltpu.VMEM((2, D_EXPERT, D_MODEL), jnp.bfloat16),
            pltpu.SemaphoreType.DMA((2, 3)),
            pltpu.SemaphoreType.DMA((1,)),
        ],
    )
    return pl.pallas_call(
        functools.partial(_moe_kernel, layer),
        grid_spec=grid_spec,
        out_shape=[jax.ShapeDtypeStruct((BATCH, MOE_CHUNK, D_MODEL), jnp.float32),
                   jax.ShapeDtypeStruct((1, MOE_CHUNK, D_MODEL), jnp.float32)],
        compiler_params=pltpu.CompilerParams(
            dimension_semantics=("arbitrary",), vmem_limit_bytes=VMEM_LIMIT),
        name="moe",
    )(meta.reshape(-1), pos.reshape(N_CHUNKS, 1, 2 * MOE_CHUNK), x, gates_t, wg, wu, wd, ln_g, ln_b)


def _moe_layer(x, wr_t, br, layer, wg, wu, wd, ln_g, ln_b):
    pos, gates, meta = _router(x, wr_t, br)
    return _moe(x, meta, pos, jnp.swapaxes(gates, 1, 2), layer, wg, wu, wd, ln_g, ln_b)


def _rope_tables(pos):
    half = HEAD_DIM // 2
    inv_freq = ROPE_THETA ** (-jnp.arange(half, dtype=jnp.float32) / half)
    ang = pos.astype(jnp.float32)[:, None] * inv_freq[None, :]
    cos = jnp.cos(ang)
    sin = jnp.sin(ang)
    reps = LANES // HEAD_DIM
    cos_t = jnp.tile(jnp.concatenate([cos, cos], axis=1), (1, reps))
    sin_t = jnp.tile(jnp.concatenate([-sin, sin], axis=1), (1, reps))
    return cos_t, sin_t


def _block_diag_tiles(w):
    per = MXU_EDGE // LRU_BLOCK_W
    w4 = w.reshape(LRU_BLOCKS // per, per, LRU_BLOCK_W, LRU_BLOCK_W)
    eye = jnp.eye(per, dtype=w.dtype)
    return jnp.einsum("qaij,ab->qaibj", w4, eye).reshape(LRU_BLOCKS // per, MXU_EDGE, MXU_EDGE)


def kernel(x_prompt, x_sample, state_conv_0, state_h_0, cache_k_1, cache_v_1, state_conv_2, state_h_2,
           cache_k_3, cache_v_3, w_router, b_router, ln_g, ln_b, lru_w_in, lru_conv_w, lru_conv_b,
           lru_w_gates, lru_b_gates, lru_lambda, lru_w_out, attn_w_qkv, attn_b_qkv, attn_sinks, attn_w_o,
           moe_w_gate, moe_w_up, moe_w_down):
    bf = jnp.bfloat16
    in_state = {0: (state_conv_0, state_h_0), 1: (cache_k_1, cache_v_1),
                2: (state_conv_2, state_h_2), 3: (cache_k_3, cache_v_3)}
    nt_s = DEC_BATCH // SAMPLE_SEGS
    kv_lanes = N_KV_HEADS * HEAD_DIM

    wr_t = w_router.T
    br = b_router.reshape(N_EXPERTS, 1)
    cos_p, sin_p = _rope_tables(jnp.arange(SEQ))
    cos_s, sin_s = _rope_tables(PAST_LEN + jnp.arange(DEC_SEQ))
    cos_s = jnp.tile(cos_s, (SAMPLE_SEGS, 1))
    sin_s = jnp.tile(sin_s, (SAMPLE_SEGS, 1))
    wg_all = moe_w_gate.astype(bf)
    wu_all = moe_w_up.astype(bf)
    wd_all = moe_w_down.astype(bf)

    x = x_prompt
    x_s = x_sample.reshape(1, MOE_CHUNK, D_MODEL)
    new_p, new_s = {}, {}
    for i in range(DEPTH):
        slot = i // 2
        g0 = ln_g[i, 0].reshape(1, D_MODEL)
        b0 = ln_b[i, 0].reshape(1, D_MODEL)
        g1 = ln_g[i, 1].reshape(1, D_MODEL)
        b1 = ln_b[i, 1].reshape(1, D_MODEL)
        st_a, st_b = in_state[i]
        if i % 2 == 0:
            gates = lru_w_gates[slot]
            w = (lru_w_in[slot].astype(bf), lru_conv_w[slot], lru_conv_b[slot].reshape(1, D_RNN),
                 _block_diag_tiles(gates[:, :, :LRU_BLOCK_W]).astype(bf),
                 _block_diag_tiles(gates[:, :, LRU_BLOCK_W:]).astype(bf),
                 lru_b_gates[slot][:, :LRU_BLOCK_W].reshape(1, D_RNN),
                 lru_b_gates[slot][:, LRU_BLOCK_W:].reshape(1, D_RNN),
                 lru_lambda[slot].reshape(1, D_RNN), lru_w_out[slot].astype(bf))
            conv_pad = jnp.pad(st_a, ((0, 0), (SUBLANES - (CONV_WIDTH - 1), 0), (0, 0)))
            x1, conv_p, h_p, conv_s, h_s = _lru(
                x, x_s, conv_pad.reshape(nt_s, SAMPLE_SEGS, SUBLANES, D_RNN),
                st_b.reshape(nt_s, SAMPLE_SEGS, D_RNN), w, g0, b0)
            new_p[i] = (conv_p, h_p.reshape(BATCH, D_RNN))
            new_s[i] = (conv_s.reshape(DEC_BATCH, CONV_WIDTH - 1, D_RNN), h_s.reshape(DEC_BATCH, D_RNN))
        else:
            w = (attn_w_qkv[slot].astype(bf), attn_b_qkv[slot].reshape(1, QKV_DIM), attn_sinks[slot],
                 attn_w_o[slot].astype(bf))
            x1, k_p, v_p, k_s, v_s = _swa(
                x, x_s, st_a.reshape(nt_s, SAMPLE_SEGS, WINDOW, kv_lanes),
                st_b.reshape(nt_s, SAMPLE_SEGS, WINDOW, kv_lanes), w, cos_p, sin_p, cos_s, sin_s, g0, b0)
            new_p[i] = (k_p.reshape(BATCH, WINDOW, N_KV_HEADS, HEAD_DIM),
                        v_p.reshape(BATCH, WINDOW, N_KV_HEADS, HEAD_DIM))
            new_s[i] = (k_s.reshape(DEC_BATCH, WINDOW, N_KV_HEADS, HEAD_DIM),
                        v_s.reshape(DEC_BATCH, WINDOW, N_KV_HEADS, HEAD_DIM))
        x, x_s = _moe_layer(x1, wr_t, br, i, wg_all, wu_all, wd_all, g1, b1)
    return (x, x_s.reshape(DEC_BATCH, DEC_SEQ, D_MODEL),
            new_p[0][0], new_s[0][0], new_p[0][1], new_s[0][1],
            new_p[1][0], new_s[1][0], new_p[1][1], new_s[1][1],
            new_p[2][0], new_s[2][0], new_p[2][1], new_s[2][1],
            new_p[3][0], new_s[3][0], new_p[3][1], new_s[3][1])
```

```python
import functools

import jax
import jax.numpy as jnp
from jax import lax
from jax.experimental import pallas as pl
from jax.experimental.pallas import tpu as pltpu

D_MODEL = 1024
BATCH = 8
SEQ = 2048
DEPTH = 4
DEC_BATCH = 32
DEC_SEQ = 64
PAST_LEN = 4096
CHUNK = 64
D_RNN = D_MODEL
CONV_WIDTH = 4
LRU_BLOCKS = 16
LRU_BLOCK_W = D_RNN // LRU_BLOCKS
LRU_C = 8.0
N_HEADS = 16
N_KV_HEADS = 2
HEAD_DIM = 64
KV_GROUP = N_HEADS // N_KV_HEADS
WINDOW = 128
ROPE_THETA = 10000.0
QKV_DIM = (N_HEADS + 2 * N_KV_HEADS) * HEAD_DIM
N_EXPERTS = 16
N_GROUPS = 4
EXPERTS_PER_GROUP = N_EXPERTS // N_GROUPS
D_EXPERT = 512
DEEPNORM_ALPHA = (2.0 * DEPTH) ** 0.25
LN_EPS = 1e-5

MXU_EDGE = 256
LANES = 128
SUBLANES = 8

ROW_TILE = 512
MOE_CHUNK = SEQ
N_CHUNKS = BATCH + DEC_BATCH * DEC_SEQ // MOE_CHUNK
SAMPLE_CHUNK = BATCH
MOE_TM = 256
MOE_ROWS = 2 * MOE_CHUNK + N_EXPERTS * MOE_TM
MOE_TILES = MOE_ROWS // MOE_TM
COMBINE_STEPS = MOE_CHUNK // ROW_TILE
JUNK_PAIR = 2 * MOE_CHUNK
SAMPLE_SEGS = ROW_TILE // DEC_SEQ
NEG_BIG = -1e30
VMEM_LIMIT = 52 * 1024 * 1024

assert DEC_BATCH * DEC_SEQ == MOE_CHUNK and MOE_TILES <= LANES


def _layer_norm(x, g, b):
    mu = jnp.mean(x, -1, keepdims=True)
    xc = x - mu
    var = jnp.mean(xc * xc, -1, keepdims=True)
    return xc * lax.rsqrt(var + LN_EPS) * g + b


def _finish_mixer(x, y, g_ref, b_ref, x1_ref):
    x1_ref[...] = _layer_norm(DEEPNORM_ALPHA * x + y, g_ref[...], b_ref[...])


def _const_spec(shape):
    nd = len(shape)
    return pl.BlockSpec(shape, lambda *_: (0,) * nd)


_ACT_SHAPE = jax.ShapeDtypeStruct((N_CHUNKS, MOE_CHUNK, D_MODEL), jnp.float32)


def _lru_tile(x, ext_scr, h_in, n_seg, seg, win_ref, cw_ref, cb_ref, wr_ref, wi_ref,
              br_ref, bi_ref, lam_ref, wout_ref):
    rows = n_seg * seg
    xb = x.astype(jnp.bfloat16)
    rec = jnp.dot(xb, win_ref[:, D_RNN:], preferred_element_type=jnp.float32)
    cw = cw_ref[...]
    xcs = []
    for s in range(n_seg):
        ext_scr[s, SUBLANES:SUBLANES + seg, :] = rec[s * seg:(s + 1) * seg]
        xc = cb_ref[...] + cw[3:4] * rec[s * seg:(s + 1) * seg]
        for k in range(CONV_WIDTH - 1):
            off = SUBLANES - (CONV_WIDTH - 1) + k
            xc = xc + cw[k:k + 1] * ext_scr[s, off:off + seg, :]
        xcs.append(xc)
    xc = xcs[0] if n_seg == 1 else jnp.concatenate(xcs, axis=0)

    xcb = xc.astype(jnp.bfloat16)
    r_parts, i_parts = [], []
    for q in range(D_RNN // MXU_EDGE):
        blk = xcb[:, q * MXU_EDGE:(q + 1) * MXU_EDGE]
        r_parts.append(jnp.dot(blk, wr_ref[q], preferred_element_type=jnp.float32))
        i_parts.append(jnp.dot(blk, wi_ref[q], preferred_element_type=jnp.float32))
    r = jax.nn.sigmoid(jnp.concatenate(r_parts, axis=1) + br_ref[...])
    ig = jax.nn.sigmoid(jnp.concatenate(i_parts, axis=1) + bi_ref[...])

    z = -lam_ref[...]
    softplus = jnp.maximum(z, 0.0) + jnp.log1p(jnp.exp(-jnp.abs(z)))
    log_a = (-LRU_C * r) * softplus
    a = jnp.exp(log_a)
    b = jnp.sqrt(-jnp.tanh(log_a) * (a * a + 1.0)) * (ig * xc)

    sub = lax.broadcasted_iota(jnp.int32, (rows, D_RNN), 0) & (SUBLANES - 1)
    step = 1
    while step < SUBLANES:
        keep = sub >= step
        a_prev = pltpu.roll(a, step, 0)
        b_prev = pltpu.roll(b, step, 0)
        b = jnp.where(keep, a * b_prev + b, b)
        a = jnp.where(keep, a * a_prev, a)
        step *= 2
    groups_per_seg = seg // SUBLANES
    h_blocks, h_last = [], []
    carry = None
    for gi in range(rows // SUBLANES):
        if gi % groups_per_seg == 0:
            carry = h_in[gi // groups_per_seg]
        lo = gi * SUBLANES
        hb = b[lo:lo + SUBLANES] + a[lo:lo + SUBLANES] * carry
        carry = hb[SUBLANES - 1:SUBLANES]
        h_blocks.append(hb)
        if gi % groups_per_seg == groups_per_seg - 1:
            h_last.append(carry)
    h = jnp.concatenate(h_blocks, axis=0)

    gate_in = jnp.dot(xb, win_ref[:, :D_RNN], preferred_element_type=jnp.float32)
    y = h * jax.nn.gelu(gate_in, approximate=True)
    out = jnp.dot(y.astype(jnp.bfloat16), wout_ref[...], preferred_element_type=jnp.float32)
    return out, rec, h_last


def _lru_kernel(xp_ref, xs_ref, conv_in_ref, h0_ref, win_ref, cw_ref, cb_ref, wr_ref, wi_ref, br_ref,
                bi_ref, lam_ref, wout_ref, g_ref, b_ref, x1_ref, conv_p_ref, h_p_ref, conv_s_ref, h_s_ref,
                ext_p_scr, h_scr, ext_s_scr):
    c = pl.program_id(0)
    t = pl.program_id(1)
    weights = (win_ref, cw_ref, cb_ref, wr_ref, wi_ref, br_ref, bi_ref, lam_ref, wout_ref)

    @pl.when(c < BATCH)
    def _():
        @pl.when(t == 0)
        def _():
            ext_p_scr[0, 0:SUBLANES, :] = jnp.zeros((SUBLANES, D_RNN), jnp.float32)
            h_scr[...] = jnp.zeros_like(h_scr)

        x = xp_ref[...]
        out, rec, h_last = _lru_tile(x, ext_p_scr, [h_scr[...]], 1, ROW_TILE, *weights)
        ext_p_scr[0, 0:SUBLANES, :] = rec[ROW_TILE - SUBLANES:]
        h_scr[...] = h_last[0]
        conv_p_ref[...] = rec[ROW_TILE - (CONV_WIDTH - 1):]
        h_p_ref[...] = h_last[0]
        _finish_mixer(x, out, g_ref, b_ref, x1_ref)

    @pl.when(c == SAMPLE_CHUNK)
    def _():
        for s in range(SAMPLE_SEGS):
            ext_s_scr[s, 0:SUBLANES, :] = conv_in_ref[s]
        h_in = [h0_ref[s:s + 1, :] for s in range(SAMPLE_SEGS)]
        x = xs_ref[...]
        out, rec, h_last = _lru_tile(x, ext_s_scr, h_in, SAMPLE_SEGS, DEC_SEQ, *weights)
        for s in range(SAMPLE_SEGS):
            end = (s + 1) * DEC_SEQ
            conv_s_ref[s] = rec[end - (CONV_WIDTH - 1):end]
            h_s_ref[s:s + 1, :] = h_last[s]
        _finish_mixer(x, out, g_ref, b_ref, x1_ref)


_TILES_PER_CHUNK = MOE_CHUNK // ROW_TILE


def _prompt_tile(c, t):
    return jnp.where(c < BATCH, t, _TILES_PER_CHUNK - 1)


def _sample_tile(c, t):
    return jnp.where(c == SAMPLE_CHUNK, t, 0)


def _mixer_act_specs():
    return [
        pl.BlockSpec((None, ROW_TILE, D_MODEL), lambda c, t: (jnp.minimum(c, BATCH - 1), _prompt_tile(c, t), 0)),
        pl.BlockSpec((None, ROW_TILE, D_MODEL), lambda c, t: (0, _sample_tile(c, t), 0)),
    ]


_MIXER_GRID = (N_CHUNKS, _TILES_PER_CHUNK)
_PER_PROMPT = lambda c, t: (jnp.minimum(c, BATCH - 1), 0, 0)
_PER_SAMPLE3 = lambda c, t: (_sample_tile(c, t), 0, 0)
_PER_SAMPLE4 = lambda c, t: (_sample_tile(c, t), 0, 0, 0)


def _lru(xp, xs, conv_pad, h0, w, ln_g, ln_b):
    nt_s = DEC_BATCH // SAMPLE_SEGS
    return pl.pallas_call(
        _lru_kernel,
        grid=_MIXER_GRID,
        in_specs=_mixer_act_specs() + [
            pl.BlockSpec((None, SAMPLE_SEGS, SUBLANES, D_RNN), _PER_SAMPLE4),
            pl.BlockSpec((None, SAMPLE_SEGS, D_RNN), _PER_SAMPLE3),
            _const_spec((D_MODEL, 2 * D_RNN)),
            _const_spec((CONV_WIDTH, D_RNN)),
            _const_spec((1, D_RNN)),
            _const_spec((D_RNN // MXU_EDGE, MXU_EDGE, MXU_EDGE)),
            _const_spec((D_RNN // MXU_EDGE, MXU_EDGE, MXU_EDGE)),
            _const_spec((1, D_RNN)),
            _const_spec((1, D_RNN)),
            _const_spec((1, D_RNN)),
            _const_spec((D_RNN, D_MODEL)),
            _const_spec((1, D_MODEL)),
            _const_spec((1, D_MODEL)),
        ],
        out_specs=[
            pl.BlockSpec((None, ROW_TILE, D_MODEL), lambda c, t: (c, t, 0)),
            pl.BlockSpec((None, CONV_WIDTH - 1, D_RNN), _PER_PROMPT),
            pl.BlockSpec((None, 1, D_RNN), _PER_PROMPT),
            pl.BlockSpec((None, SAMPLE_SEGS, CONV_WIDTH - 1, D_RNN), _PER_SAMPLE4),
            pl.BlockSpec((None, SAMPLE_SEGS, D_RNN), _PER_SAMPLE3),
        ],
        out_shape=[
            _ACT_SHAPE,
            jax.ShapeDtypeStruct((BATCH, CONV_WIDTH - 1, D_RNN), jnp.float32),
            jax.ShapeDtypeStruct((BATCH, 1, D_RNN), jnp.float32),
            jax.ShapeDtypeStruct((nt_s, SAMPLE_SEGS, CONV_WIDTH - 1, D_RNN), jnp.float32),
            jax.ShapeDtypeStruct((nt_s, SAMPLE_SEGS, D_RNN), jnp.float32),
        ],
        scratch_shapes=[
            pltpu.VMEM((1, SUBLANES + ROW_TILE, D_RNN), jnp.float32),
            pltpu.VMEM((1, D_RNN), jnp.float32),
            pltpu.VMEM((SAMPLE_SEGS, SUBLANES + DEC_SEQ, D_RNN), jnp.float32),
        ],
        compiler_params=pltpu.CompilerParams(
            dimension_semantics=("arbitrary", "arbitrary"), vmem_limit_bytes=VMEM_LIMIT),
        name="lru",
    )(xp, xs, conv_pad, h0, *w, ln_g, ln_b)


def _rope(x, cos, sin_signed):
    lane = lax.broadcasted_iota(jnp.int32, (x.shape[0], LANES), 1)
    first_half = (lane & (HEAD_DIM - 1)) < HEAD_DIM // 2
    cols = []
    for c in range(x.shape[1] // LANES):
        xc = x[:, c * LANES:(c + 1) * LANES]
        partner = jnp.where(first_half, pltpu.roll(xc, LANES - HEAD_DIM // 2, 1),
                            pltpu.roll(xc, HEAD_DIM // 2, 1))
        cols.append(xc * cos + partner * sin_signed)
    return cols[0] if len(cols) == 1 else jnp.concatenate(cols, axis=1)


def _dup_heads(kv):
    lane = lax.broadcasted_iota(jnp.int32, kv.shape, 1)
    low = lane < HEAD_DIM
    swapped = pltpu.roll(kv, HEAD_DIM, 1)
    return (jnp.where(low, kv, swapped).astype(jnp.bfloat16),
            jnp.where(low, swapped, kv).astype(jnp.bfloat16))


def _attend(chunks, sinks_ref):
    lane = lax.broadcasted_iota(jnp.int32, (CHUNK, LANES), 1)
    low = lane < HEAD_DIM
    zero = jnp.zeros((CHUNK, LANES), jnp.bfloat16)
    cols_per_group = KV_GROUP * HEAD_DIM // LANES
    units = [(ci, g) for ci in range(len(chunks)) for g in range(N_KV_HEADS)]

    scores = {}
    for ci, g in units:
        q, k_wins, _, valid = chunks[ci]
        stacked = []
        for c in range(g * cols_per_group, (g + 1) * cols_per_group):
            qc = q[:, c * LANES:(c + 1) * LANES]
            stacked.append(jnp.where(low, qc, zero))
            stacked.append(jnp.where(low, zero, qc))
        qst = jnp.concatenate(stacked, axis=0)
        s = lax.dot_general(qst, k_wins[g], (((1,), (1,)), ((), ())),
                            preferred_element_type=jnp.float32)
        scores[ci, g] = s if valid is None else jnp.where(valid, s, NEG_BIG)

    probs = {}
    for ci, g in units:
        s = scores[ci, g]
        heads = []
        for hh in range(KV_GROUP):
            sh = s[hh * CHUNK:(hh + 1) * CHUNK]
            sink = sinks_ref[g * KV_GROUP + hh]
            m = jnp.maximum(jnp.max(sh, -1, keepdims=True), sink)
            p = jnp.exp(sh - m)
            denom = jnp.sum(p, -1, keepdims=True) + jnp.exp(sink - m)
            heads.append((p * (1.0 / denom)).astype(jnp.bfloat16))
        probs[ci, g] = jnp.concatenate(heads, axis=0)

    out_cols = {ci: [] for ci in range(len(chunks))}
    for ci, g in units:
        o = jnp.dot(probs[ci, g], chunks[ci][2][g], preferred_element_type=jnp.float32)
        for j in range(cols_per_group):
            oa = o[(2 * j) * CHUNK:(2 * j + 1) * CHUNK]
            ob = o[(2 * j + 1) * CHUNK:(2 * j + 2) * CHUNK]
            out_cols[ci].append(jnp.where(low, oa, ob))
    return [jnp.concatenate(out_cols[ci], axis=1) for ci in range(len(chunks))]


def _project_qkv(x, wqkv_ref, bqkv_ref, cos, sin):
    nq = N_HEADS * HEAD_DIM
    nk = N_KV_HEADS * HEAD_DIM
    xb = x.astype(jnp.bfloat16)
    kv = jnp.dot(xb, wqkv_ref[:, nq:], preferred_element_type=jnp.float32) + bqkv_ref[:, nq:]
    k = _rope(kv[:, :nk], cos, sin)
    v = kv[:, nk:]
    q = jnp.dot(xb, wqkv_ref[:, :nq], preferred_element_type=jnp.float32) + bqkv_ref[:, :nq]
    q = _rope(q, cos, sin)
    qs = (q * (HEAD_DIM ** -0.5)).astype(jnp.bfloat16)
    return qs, k, v


SUB_TILE = 256


def _sub_tiles():
    return [pl.ds(i * SUB_TILE, SUB_TILE) for i in range(ROW_TILE // SUB_TILE)]


def _project_out(xs, subs, o, wo_ref, g_ref, b_ref, x1_ref):
    for x, sub, i in zip(xs, subs, range(len(subs))):
        out = jnp.dot(o[i * SUB_TILE:(i + 1) * SUB_TILE].astype(jnp.bfloat16), wo_ref[...],
                      preferred_element_type=jnp.float32)
        x1_ref[sub, :] = _layer_norm(DEEPNORM_ALPHA * x + out, g_ref[...], b_ref[...])


def _swa_kernel(xp_ref, xs_ref, ck_ref, cv_ref, wqkv_ref, bqkv_ref, cosp_ref, sinp_ref, coss_ref, sins_ref,
                sinks_ref, wo_ref, g_ref, b_ref, x1_ref, kp_ref, vp_ref, ks_ref, vs_ref,
                k0_scr, k1_scr, v0_scr, v1_scr):
    c = pl.program_id(0)
    t = pl.program_id(1)
    scrs = (k0_scr, k1_scr, v0_scr, v1_scr)

    @pl.when(c < BATCH)
    def _():
        @pl.when(t == 0)
        def _():
            for scr in scrs:
                scr[0:WINDOW, :] = jnp.zeros((WINDOW, LANES), jnp.bfloat16)

        subs = _sub_tiles()
        xs_ = [xp_ref[sub, :] for sub in subs]
        proj = [_project_qkv(x, wqkv_ref, bqkv_ref, cosp_ref[sub, :], sinp_ref[sub, :]) for x, sub in zip(xs_, subs)]
        qs = jnp.concatenate([p[0] for p in proj], axis=0)
        k = jnp.concatenate([p[1] for p in proj], axis=0)
        v = jnp.concatenate([p[2] for p in proj], axis=0)
        kp_ref[...] = k[ROW_TILE - WINDOW:]
        vp_ref[...] = v[ROW_TILE - WINDOW:]
        k0, k1 = _dup_heads(k)
        v0, v1 = _dup_heads(v)
        for scr, val in zip(scrs, (k0, k1, v0, v1)):
            scr[WINDOW:WINDOW + ROW_TILE, :] = val

        span = WINDOW + CHUNK
        key_pos = lax.broadcasted_iota(jnp.int32, (1, span), 1)
        chunks = []
        for ci in range(ROW_TILE // CHUNK):
            lo = ci * CHUNK
            valid = (t * ROW_TILE + lo - WINDOW + key_pos) >= 0
            k_wins = (k0_scr[lo:lo + span, :], k1_scr[lo:lo + span, :])
            v_wins = (v0_scr[lo:lo + span, :], v1_scr[lo:lo + span, :])
            chunks.append((qs[lo:lo + CHUNK], k_wins, v_wins, valid))
        o = jnp.concatenate(_attend(chunks, sinks_ref), axis=0)
        for scr in scrs:
            scr[0:WINDOW, :] = scr[ROW_TILE:ROW_TILE + WINDOW, :]
        _project_out(xs_, subs, o, wo_ref, g_ref, b_ref, x1_ref)

    @pl.when(c == SAMPLE_CHUNK)
    def _():
        subs = _sub_tiles()
        xs_ = [xs_ref[sub, :] for sub in subs]
        proj = [_project_qkv(x, wqkv_ref, bqkv_ref, coss_ref[sub, :], sins_ref[sub, :]) for x, sub in zip(xs_, subs)]
        qs = jnp.concatenate([p[0] for p in proj], axis=0)
        k = jnp.concatenate([p[1] for p in proj], axis=0)
        v = jnp.concatenate([p[2] for p in proj], axis=0)
        knew = _dup_heads(k)
        vnew = _dup_heads(v)
        chunks = []
        for s in range(SAMPLE_SEGS):
            lo = s * DEC_SEQ
            ck = ck_ref[s]
            cv = cv_ref[s]
            kold = _dup_heads(ck)
            vold = _dup_heads(cv)
            k_wins = [jnp.concatenate([kold[g], knew[g][lo:lo + DEC_SEQ]], axis=0) for g in range(N_KV_HEADS)]
            v_wins = [jnp.concatenate([vold[g], vnew[g][lo:lo + DEC_SEQ]], axis=0) for g in range(N_KV_HEADS)]
            chunks.append((qs[lo:lo + DEC_SEQ], k_wins, v_wins, None))
            ks_ref[s, 0:WINDOW - DEC_SEQ, :] = ck[DEC_SEQ:]
            ks_ref[s, WINDOW - DEC_SEQ:WINDOW, :] = k[lo:lo + DEC_SEQ]
            vs_ref[s, 0:WINDOW - DEC_SEQ, :] = cv[DEC_SEQ:]
            vs_ref[s, WINDOW - DEC_SEQ:WINDOW, :] = v[lo:lo + DEC_SEQ]
        o = jnp.concatenate(_attend(chunks, sinks_ref), axis=0)
        _project_out(xs_, subs, o, wo_ref, g_ref, b_ref, x1_ref)


def _swa(xp, xs, ck, cv, w, cos_p, sin_p, cos_s, sin_s, ln_g, ln_b):
    wqkv, bqkv, sinks, wo = w
    nt_s = DEC_BATCH // SAMPLE_SEGS
    kv_lanes = N_KV_HEADS * HEAD_DIM
    cache_spec = pl.BlockSpec((None, SAMPLE_SEGS, WINDOW, kv_lanes), _PER_SAMPLE4)
    prompt_table = pl.BlockSpec((ROW_TILE, LANES), lambda c, t: (_prompt_tile(c, t), 0))
    kv_prompt = pl.BlockSpec((None, WINDOW, kv_lanes), _PER_PROMPT)
    return pl.pallas_call(
        _swa_kernel,
        grid=_MIXER_GRID,
        in_specs=_mixer_act_specs() + [
            cache_spec, cache_spec,
            _const_spec((D_MODEL, QKV_DIM)),
            _const_spec((1, QKV_DIM)),
            prompt_table, prompt_table,
            _const_spec((ROW_TILE, LANES)), _const_spec((ROW_TILE, LANES)),
            pl.BlockSpec(memory_space=pltpu.SMEM),
            _const_spec((N_HEADS * HEAD_DIM, D_MODEL)),
            _const_spec((1, D_MODEL)),
            _const_spec((1, D_MODEL)),
        ],
        out_specs=[
            pl.BlockSpec((None, ROW_TILE, D_MODEL), lambda c, t: (c, t, 0)),
            kv_prompt, kv_prompt, cache_spec, cache_spec,
        ],
        out_shape=[
            _ACT_SHAPE,
            jax.ShapeDtypeStruct((BATCH, WINDOW, kv_lanes), jnp.float32),
            jax.ShapeDtypeStruct((BATCH, WINDOW, kv_lanes), jnp.float32),
            jax.ShapeDtypeStruct((nt_s, SAMPLE_SEGS, WINDOW, kv_lanes), jnp.float32),
            jax.ShapeDtypeStruct((nt_s, SAMPLE_SEGS, WINDOW, kv_lanes), jnp.float32),
        ],
        scratch_shapes=[pltpu.VMEM((WINDOW + ROW_TILE, LANES), jnp.bfloat16) for _ in range(4)],
        compiler_params=pltpu.CompilerParams(
            dimension_semantics=("arbitrary", "arbitrary"), vmem_limit_bytes=VMEM_LIMIT),
        name="swa",
    )(xp, xs, ck, cv, wqkv, bqkv, cos_p, sin_p, cos_s, sin_s, sinks, wo, ln_g, ln_b)


META_TILE_EXPERT, META_N_ACTIVE, META_FILL_LO, META_FILL_HI, META_IS_NEW, META_WEIGHT_SLOT = range(6)


def _router_kernel(x_ref, wr_ref, br_ref, pos_ref, gate_ref, meta_ref):
    f32 = jnp.float32
    x = x_ref[...]
    w = wr_ref[...]
    xh = x.astype(jnp.bfloat16)
    xl = (x - xh.astype(f32)).astype(jnp.bfloat16)
    wh = w.astype(jnp.bfloat16)
    wl = (w - wh.astype(f32)).astype(jnp.bfloat16)
    nt = (((1,), (1,)), ((), ()))
    logits = (lax.dot_general(wh, xh, nt, preferred_element_type=f32)
              + lax.dot_general(wl, xh, nt, preferred_element_type=f32)
              + lax.dot_general(wh, xl, nt, preferred_element_type=f32))
    m = jnp.max(logits, axis=0, keepdims=True)
    e = jnp.exp(logits - m)
    probs = e / jnp.sum(e, axis=0, keepdims=True)
    sel = probs + br_ref[...]
    p = [probs[i:i + 1] for i in range(N_EXPERTS)]
    s = [sel[i:i + 1] for i in range(N_EXPERTS)]

    def first_argmax(vals):
        best, arg = vals[0], jnp.zeros_like(vals[0], dtype=jnp.int32)
        for i in range(1, len(vals)):
            better = vals[i] > best
            best = jnp.where(better, vals[i], best)
            arg = jnp.where(better, i, arg)
        return arg

    group_scores = []
    for gidx in range(N_GROUPS):
        v = s[gidx * EXPERTS_PER_GROUP:(gidx + 1) * EXPERTS_PER_GROUP]
        best = v[0] + v[1]
        for i in range(EXPERTS_PER_GROUP):
            for j in range(i + 1, EXPERTS_PER_GROUP):
                if (i, j) != (0, 1):
                    best = jnp.maximum(best, v[i] + v[j])
        group_scores.append(best)
    best_group = first_argmax(group_scores)
    neg_inf = jnp.full_like(s[0], -jnp.inf)
    masked = [jnp.where(best_group == (i // EXPERTS_PER_GROUP), s[i], neg_inf) for i in range(N_EXPERTS)]
    idx1 = first_argmax(masked)
    masked2 = [jnp.where(idx1 == i, neg_inf, masked[i]) for i in range(N_EXPERTS)]
    idx2 = first_argmax(masked2)
    zero = jnp.zeros_like(p[0])
    w1, w2 = zero, zero
    for i in range(N_EXPERTS):
        w1 = w1 + jnp.where(idx1 == i, p[i], zero)
        w2 = w2 + jnp.where(idx2 == i, p[i], zero)
    tot = w1 + w2
    gate_ref[0:1, :] = w1 / tot
    gate_ref[1:2, :] = w2 / tot

    expert = lax.broadcasted_iota(jnp.int32, (N_EXPERTS, MOE_CHUNK), 0)
    strict_upper = (lax.broadcasted_iota(jnp.int32, (LANES, LANES), 0)
                    < lax.broadcasted_iota(jnp.int32, (LANES, LANES), 1)).astype(jnp.bfloat16)
    carry = jnp.zeros((N_EXPERTS, 1), f32)
    onehots, ranks = [], []
    for idx in (idx1, idx2):
        onehot = (expert == idx).astype(f32)
        before = []
        for blk in range(MOE_CHUNK // LANES):
            oh = onehot[:, blk * LANES:(blk + 1) * LANES]
            before.append(jnp.dot(oh.astype(jnp.bfloat16), strict_upper, preferred_element_type=f32) + carry)
            carry = carry + jnp.sum(oh, axis=1, keepdims=True)
        onehots.append(onehot)
        ranks.append(jnp.sum(onehot * jnp.concatenate(before, axis=1), axis=0, keepdims=True))
    counts = carry
    padded = jnp.floor((counts + (MOE_TM - 1)) * (1.0 / MOE_TM)) * MOE_TM
    strict_lower = (lax.broadcasted_iota(jnp.int32, (N_EXPERTS, N_EXPERTS), 0)
                    > lax.broadcasted_iota(jnp.int32, (N_EXPERTS, N_EXPERTS), 1)).astype(jnp.bfloat16)
    starts = jnp.dot(strict_lower, jnp.broadcast_to(padded, (N_EXPERTS, LANES)).astype(jnp.bfloat16),
                     preferred_element_type=f32)[:, 0:1]
    ends = starts + padded
    for k in range(2):
        pos = jnp.sum(onehots[k] * starts, axis=0, keepdims=True) + ranks[k]
        pos_ref[k:k + 1, :] = pos.astype(jnp.int32)

    lane = lax.broadcasted_iota(jnp.int32, (N_EXPERTS, LANES), 1)
    sub = lax.broadcasted_iota(jnp.int32, (N_EXPERTS, LANES), 0)
    tile_start = (lane * MOE_TM).astype(f32)
    tile_expert = jnp.sum((ends <= tile_start).astype(f32), axis=0, keepdims=True)
    tile_expert = jnp.minimum(tile_expert, N_EXPERTS - 1.0)
    on_diag = sub == lane

    def to_lanes(col):
        return jnp.sum(jnp.where(on_diag, col, 0.0), axis=0, keepdims=True)

    lane1 = lax.broadcasted_iota(jnp.int32, (1, LANES), 1)
    is_new = jnp.where(jnp.logical_or(lane1 == 0, tile_expert != pltpu.roll(tile_expert, 1, 1)), 1.0, 0.0)
    upper_incl = (lax.broadcasted_iota(jnp.int32, (LANES, LANES), 0)
                  <= lax.broadcasted_iota(jnp.int32, (LANES, LANES), 1)).astype(jnp.bfloat16)
    runs = jnp.dot(jnp.broadcast_to(is_new, (SUBLANES, LANES)).astype(jnp.bfloat16), upper_incl,
                   preferred_element_type=f32)[0:1]
    weight_slot = (runs.astype(jnp.int32) - 1) & 1

    n_active = jnp.broadcast_to(ends[N_EXPERTS - 1:N_EXPERTS] * (1.0 / MOE_TM), (1, LANES))
    meta_ref[...] = jnp.zeros(meta_ref.shape, jnp.int32)
    meta_ref[META_IS_NEW:META_IS_NEW + 1, :] = is_new.astype(jnp.int32)
    meta_ref[META_WEIGHT_SLOT:META_WEIGHT_SLOT + 1, :] = weight_slot
    meta_ref[META_TILE_EXPERT:META_TILE_EXPERT + 1, :] = tile_expert.astype(jnp.int32)
    meta_ref[META_N_ACTIVE:META_N_ACTIVE + 1, :] = n_active.astype(jnp.int32)
    meta_ref[META_FILL_LO:META_FILL_LO + 1, :] = to_lanes(starts + counts).astype(jnp.int32)
    meta_ref[META_FILL_HI:META_FILL_HI + 1, :] = to_lanes(ends).astype(jnp.int32)


def _router(x, wr_t, br):
    per_chunk = lambda c: (c, 0, 0)
    return pl.pallas_call(
        _router_kernel,
        grid=(N_CHUNKS,),
        in_specs=[
            pl.BlockSpec((None, MOE_CHUNK, D_MODEL), per_chunk),
            _const_spec((N_EXPERTS, D_MODEL)),
            _const_spec((N_EXPERTS, 1)),
        ],
        out_specs=[
            pl.BlockSpec((None, 2, MOE_CHUNK), per_chunk),
            pl.BlockSpec((None, 2, MOE_CHUNK), per_chunk),
            pl.BlockSpec((None, SUBLANES, LANES), per_chunk),
        ],
        out_shape=[
            jax.ShapeDtypeStruct((N_CHUNKS, 2, MOE_CHUNK), jnp.int32),
            jax.ShapeDtypeStruct((N_CHUNKS, 2, MOE_CHUNK), jnp.float32),
            jax.ShapeDtypeStruct((N_CHUNKS, SUBLANES, LANES), jnp.int32),
        ],
        compiler_params=pltpu.CompilerParams(
            dimension_semantics=("parallel",), vmem_limit_bytes=VMEM_LIMIT),
        name="router",
    )(x, wr_t, br)


def _gather_rows(pair_scr, x_ref, xs_scr, row0, j):
    tok = pair_scr[row0 + j] & (MOE_CHUNK - 1)
    xs_scr[j // SUBLANES, (j % SUBLANES):(j % SUBLANES) + 1, :] = x_ref[pl.ds(tok, 1), :]


def _scatter_rows(pair_scr, y_scr, yo_scr, row0, j):
    y_scr[pl.ds(pair_scr[row0 + j], 1), :] = yo_scr[j // SUBLANES, (j % SUBLANES):(j % SUBLANES) + 1, :]


def _moe_kernel(layer, meta_ref, pos_ref, x_ref, gate_ref,
                wg_hbm, wu_hbm, wd_hbm, g_ref, b_ref, outp_hbm, outs_hbm,
                pair_scr, xs_a, xs_b, yo_a, yo_b, y_scr, wg_buf, wu_buf, wd_buf, w_sem, out_sem):
    c = pl.program_id(0)

    def plan(row, i):
        return meta_ref[(c * SUBLANES + row) * LANES + i]

    n_act = plan(META_N_ACTIVE, 0)

    def weight_copies(t):
        e, s = plan(META_TILE_EXPERT, t), plan(META_WEIGHT_SLOT, t)
        return (pltpu.make_async_copy(wg_hbm.at[layer, e], wg_buf.at[s], w_sem.at[s, 0]),
                pltpu.make_async_copy(wu_hbm.at[layer, e], wu_buf.at[s], w_sem.at[s, 1]),
                pltpu.make_async_copy(wd_hbm.at[layer, e], wd_buf.at[s], w_sem.at[s, 2]))

    def prompt_out_copy(chunk):
        return pltpu.make_async_copy(y_scr.at[pl.ds(0, MOE_CHUNK)], outp_hbm.at[chunk], out_sem.at[0])

    def sample_out_copy():
        return pltpu.make_async_copy(y_scr.at[pl.ds(0, MOE_CHUNK)], outs_hbm.at[0], out_sem.at[0])

    for cp in weight_copies(0):
        cp.start()

    def fill8(g, base):
        for u in range(SUBLANES):
            pair_scr[base + g * SUBLANES + u] = JUNK_PAIR
        return base

    def fill_expert(e, carry):
        lo, hi = plan(META_FILL_LO, e), plan(META_FILL_HI, e)
        lax.fori_loop(0, (hi - lo + SUBLANES - 1) >> 3, fill8, lo)
        return carry
    lax.fori_loop(0, N_EXPERTS, fill_expert, 0)
    lax.fori_loop(0, MOE_TM // SUBLANES, fill8, plan(META_FILL_HI, N_EXPERTS - 1))

    def invert(t, carry):
        pair_scr[pos_ref[t]] = t
        pair_scr[pos_ref[MOE_CHUNK + t]] = MOE_CHUNK + t
        return carry
    lax.fori_loop(0, MOE_CHUNK, invert, 0, unroll=8)

    def gather0(gi, carry):
        for u in range(SUBLANES):
            tok = pair_scr[gi * SUBLANES + u] & (MOE_CHUNK - 1)
            xs_a[gi, u:u + 1, :] = x_ref[pl.ds(tok, 1), :]
        return carry
    lax.fori_loop(0, MOE_TM // SUBLANES, gather0, 0)

    @pl.when(c == 0)
    def _():
        yo_b[...] = jnp.zeros_like(yo_b)

    @pl.when(c > 0)
    def _():
        prompt_out_copy(c - 1).wait()

    def expert_step(t, s, xs_cur, yo_cur, xs_nxt, yo_prv):
        row_nxt = (t + 1) * MOE_TM
        row_prv = jnp.maximum(t - 1, 0) * MOE_TM
        for j in range(MOE_TM):
            _scatter_rows(pair_scr, y_scr, yo_prv, row_prv, j)
            _gather_rows(pair_scr, x_ref, xs_nxt, row_nxt, j)
        xb = xs_cur[...].reshape(MOE_TM, D_MODEL).astype(jnp.bfloat16)
        hg = jnp.dot(xb, wg_buf[s], preferred_element_type=jnp.float32)
        hu = jnp.dot(xb, wu_buf[s], preferred_element_type=jnp.float32)
        act = (jax.nn.silu(hg) * hu).astype(jnp.bfloat16)
        yo = jnp.dot(act, wd_buf[s], preferred_element_type=jnp.float32)
        yo_cur[...] = yo.reshape(MOE_TM // SUBLANES, SUBLANES, D_MODEL)

    def tile_iteration(t, carry):
        @pl.when(plan(META_IS_NEW, t) == 1)
        def _():
            for cp in weight_copies(t):
                cp.wait()

        @pl.when(jnp.logical_and(t + 1 < n_act, plan(META_IS_NEW, t + 1) == 1))
        def _():
            for cp in weight_copies(t + 1):
                cp.start()

        for s in range(2):
            in_slot = plan(META_WEIGHT_SLOT, t) == s

            @pl.when(jnp.logical_and((t & 1) == 0, in_slot))
            def _():
                expert_step(t, s, xs_a, yo_a, xs_b, yo_b)

            @pl.when(jnp.logical_and((t & 1) == 1, in_slot))
            def _():
                expert_step(t, s, xs_b, yo_b, xs_a, yo_a)
        return carry
    lax.fori_loop(0, n_act, tile_iteration, 0)

    def drain(yo_last):
        row_last = (n_act - 1) * MOE_TM

        def scatter(gi, carry):
            for u in range(SUBLANES):
                y_scr[pl.ds(pair_scr[row_last + gi * SUBLANES + u], 1), :] = yo_last[gi, u:u + 1, :]
            return carry
        lax.fori_loop(0, MOE_TM // SUBLANES, scatter, 0)

    @pl.when((n_act & 1) == 1)
    def _():
        drain(yo_a)

    @pl.when((n_act & 1) == 0)
    def _():
        drain(yo_b)

    def combine(k, carry):
        lo = pl.multiple_of(k * ROW_TILE, ROW_TILE)
        gates = gate_ref[pl.ds(lo, ROW_TILE), :]
        moe = (gates[:, 0:1] * y_scr[pl.ds(lo, ROW_TILE), :]
               + gates[:, 1:2] * y_scr[pl.ds(MOE_CHUNK + lo, ROW_TILE), :])
        x = x_ref[pl.ds(lo, ROW_TILE), :]
        y_scr[pl.ds(lo, ROW_TILE), :] = _layer_norm(DEEPNORM_ALPHA * x + moe, g_ref[...], b_ref[...])
        return carry
    lax.fori_loop(0, COMBINE_STEPS, combine, 0)

    @pl.when(c < SAMPLE_CHUNK)
    def _():
        prompt_out_copy(c).start()

    @pl.when(c == SAMPLE_CHUNK)
    def _():
        sample_out_copy().start()
        sample_out_copy().wait()


def _moe(x, meta, pos, gates_t, layer, wg, wu, wd, ln_g, ln_b):
    stage = pltpu.VMEM((MOE_TM // SUBLANES, SUBLANES, D_MODEL), jnp.float32)
    per_chunk = lambda c, *_: (c, 0, 0)
    grid_spec = pltpu.PrefetchScalarGridSpec(
        num_scalar_prefetch=1,
        grid=(N_CHUNKS,),
        in_specs=[
            pl.BlockSpec((None, None, 2 * MOE_CHUNK), per_chunk, memory_space=pltpu.SMEM),
            pl.BlockSpec((None, MOE_CHUNK, D_MODEL), per_chunk),
            pl.BlockSpec((None, MOE_CHUNK, 2), per_chunk),
            pl.BlockSpec(memory_space=pl.ANY),
            pl.BlockSpec(memory_space=pl.ANY),
            pl.BlockSpec(memory_space=pl.ANY),
            pl.BlockSpec((1, D_MODEL), lambda c, *_: (0, 0)),
            pl.BlockSpec((1, D_MODEL), lambda c, *_: (0, 0)),
        ],
        out_specs=[pl.BlockSpec(memory_space=pl.ANY), pl.BlockSpec(memory_space=pl.ANY)],
        scratch_shapes=[
            pltpu.SMEM((MOE_ROWS + MOE_TM + SUBLANES,), jnp.int32),
            stage, stage, stage, stage,
            pltpu.VMEM((2 * MOE_CHUNK + SUBLANES, D_MODEL), jnp.float32),
            pltpu.VMEM((2, D_MODEL, D_EXPERT), jnp.bfloat16),
            pltpu.VMEM((2, D_MODEL, D_EXPERT), jnp.bfloat16),
            pltpu.VMEM((2, D_EXPERT, D_MODEL), jnp.bfloat16),
            pltpu.SemaphoreType.DMA((2, 3)),
            pltpu.SemaphoreType.DMA((1,)),
        ],
    )
    return pl.pallas_call(
        functools.partial(_moe_kernel, layer),
        grid_spec=grid_spec,
        out_shape=[jax.ShapeDtypeStruct((BATCH, MOE_CHUNK, D_MODEL), jnp.float32),
                   jax.ShapeDtypeStruct((1, MOE_CHUNK, D_MODEL), jnp.float32)],
        compiler_params=pltpu.CompilerParams(
            dimension_semantics=("arbitrary",), vmem_limit_bytes=VMEM_LIMIT),
        name="moe",
    )(meta.reshape(-1), pos.reshape(N_CHUNKS, 1, 2 * MOE_CHUNK), x, gates_t, wg, wu, wd, ln_g, ln_b)


def _moe_layer(x, wr_t, br, layer, wg, wu, wd, ln_g, ln_b):
    pos, gates, meta = _router(x, wr_t, br)
    return _moe(x, meta, pos, jnp.swapaxes(gates, 1, 2), layer, wg, wu, wd, ln_g, ln_b)


def _rope_tables(pos):
    half = HEAD_DIM // 2
    inv_freq = ROPE_THETA ** (-jnp.arange(half, dtype=jnp.float32) / half)
    ang = pos.astype(jnp.float32)[:, None] * inv_freq[None, :]
    cos = jnp.cos(ang)
    sin = jnp.sin(ang)
    reps = LANES // HEAD_DIM
    cos_t = jnp.tile(jnp.concatenate([cos, cos], axis=1), (1, reps))
    sin_t = jnp.tile(jnp.concatenate([-sin, sin], axis=1), (1, reps))
    return cos_t, sin_t


def _block_diag_tiles(w):
    per = MXU_EDGE // LRU_BLOCK_W
    w4 = w.reshape(LRU_BLOCKS // per, per, LRU_BLOCK_W, LRU_BLOCK_W)
    eye = jnp.eye(per, dtype=w.dtype)
    return jnp.einsum("qaij,ab->qaibj", w4, eye).reshape(LRU_BLOCKS // per, MXU_EDGE, MXU_EDGE)


def kernel(x_prompt, x_sample, state_conv_0, state_h_0, cache_k_1, cache_v_1, state_conv_2, state_h_2,
           cache_k_3, cache_v_3, w_router, b_router, ln_g, ln_b, lru_w_in, lru_conv_w, lru_conv_b,
           lru_w_gates, lru_b_gates, lru_lambda, lru_w_out, attn_w_qkv, attn_b_qkv, attn_sinks, attn_w_o,
           moe_w_gate, moe_w_up, moe_w_down):
    bf = jnp.bfloat16
    in_state = {0: (state_conv_0, state_h_0), 1: (cache_k_1, cache_v_1),
                2: (state_conv_2, state_h_2), 3: (cache_k_3, cache_v_3)}
    nt_s = DEC_BATCH // SAMPLE_SEGS
    kv_lanes = N_KV_HEADS * HEAD_DIM

    wr_t = w_router.T
    br = b_router.reshape(N_EXPERTS, 1)
    cos_p, sin_p = _rope_tables(jnp.arange(SEQ))
    cos_s, sin_s = _rope_tables(PAST_LEN + jnp.arange(DEC_SEQ))
    cos_s = jnp.tile(cos_s, (SAMPLE_SEGS, 1))
    sin_s = jnp.tile(sin_s, (SAMPLE_SEGS, 1))
    wg_all = moe_w_gate.astype(bf)
    wu_all = moe_w_up.astype(bf)
    wd_all = moe_w_down.astype(bf)

    x = x_prompt
    x_s = x_sample.reshape(1, MOE_CHUNK, D_MODEL)
    new_p, new_s = {}, {}
    for i in range(DEPTH):
        slot = i // 2
        g0 = ln_g[i, 0].reshape(1, D_MODEL)
        b0 = ln_b[i, 0].reshape(1, D_MODEL)
        g1 = ln_g[i, 1].reshape(1, D_MODEL)
        b1 = ln_b[i, 1].reshape(1, D_MODEL)
        st_a, st_b = in_state[i]
        if i % 2 == 0:
            gates = lru_w_gates[slot]
            w = (lru_w_in[slot].astype(bf), lru_conv_w[slot], lru_conv_b[slot].reshape(1, D_RNN),
                 _block_diag_tiles(gates[:, :, :LRU_BLOCK_W]).astype(bf),
                 _block_diag_tiles(gates[:, :, LRU_BLOCK_W:]).astype(bf),
                 lru_b_gates[slot][:, :LRU_BLOCK_W].reshape(1, D_RNN),
                 lru_b_gates[slot][:, LRU_BLOCK_W:].reshape(1, D_RNN),
                 lru_lambda[slot].reshape(1, D_RNN), lru_w_out[slot].astype(bf))
            conv_pad = jnp.pad(st_a, ((0, 0), (SUBLANES - (CONV_WIDTH - 1), 0), (0, 0)))
            x1, conv_p, h_p, conv_s, h_s = _lru(
                x, x_s, conv_pad.reshape(nt_s, SAMPLE_SEGS, SUBLANES, D_RNN),
                st_b.reshape(nt_s, SAMPLE_SEGS, D_RNN), w, g0, b0)
            new_p[i] = (conv_p, h_p.reshape(BATCH, D_RNN))
            new_s[i] = (conv_s.reshape(DEC_BATCH, CONV_WIDTH - 1, D_RNN), h_s.reshape(DEC_BATCH, D_RNN))
        else:
            w = (attn_w_qkv[slot].astype(bf), attn_b_qkv[slot].reshape(1, QKV_DIM), attn_sinks[slot],
                 attn_w_o[slot].astype(bf))
            x1, k_p, v_p, k_s, v_s = _swa(
                x, x_s, st_a.reshape(nt_s, SAMPLE_SEGS, WINDOW, kv_lanes),
                st_b.reshape(nt_s, SAMPLE_SEGS, WINDOW, kv_lanes), w, cos_p, sin_p, cos_s, sin_s, g0, b0)
            new_p[i] = (k_p.reshape(BATCH, WINDOW, N_KV_HEADS, HEAD_DIM),
                        v_p.reshape(BATCH, WINDOW, N_KV_HEADS, HEAD_DIM))
            new_s[i] = (k_s.reshape(DEC_BATCH, WINDOW, N_KV_HEADS, HEAD_DIM),
                        v_s.reshape(DEC_BATCH, WINDOW, N_KV_HEADS, HEAD_DIM))
        x, x_s = _moe_layer(x1, wr_t, br, i, wg_all, wu_all, wd_all, g1, b1)
    return (x, x_s.reshape(DEC_BATCH, DEC_SEQ, D_MODEL),
            new_p[0][0], new_s[0][0], new_p[0][1], new_s[0][1],
            new_p[1][0], new_s[1][0], new_p[1][1], new_s[1][1],
            new_p[2][0], new_s[2][0], new_p[2][1], new_s[2][1],
            new_p[3][0], new_s[3][0], new_p[3][1], new_s[3][1])
```

```python
import functools

import jax
import jax.numpy as jnp
from jax import lax
from jax.experimental import pallas as pl
from jax.experimental.pallas import tpu as pltpu

D_MODEL = 1024
BATCH = 8
SEQ = 2048
DEPTH = 4
DEC_BATCH = 32
DEC_SEQ = 64
PAST_LEN = 4096
CHUNK = 64
D_RNN = D_MODEL
CONV_WIDTH = 4
LRU_BLOCKS = 16
LRU_BLOCK_W = D_RNN // LRU_BLOCKS
LRU_C = 8.0
N_HEADS = 16
N_KV_HEADS = 2
HEAD_DIM = 64
KV_GROUP = N_HEADS // N_KV_HEADS
WINDOW = 128
ROPE_THETA = 10000.0
QKV_DIM = (N_HEADS + 2 * N_KV_HEADS) * HEAD_DIM
N_EXPERTS = 16
N_GROUPS = 4
EXPERTS_PER_GROUP = N_EXPERTS // N_GROUPS
D_EXPERT = 512
DEEPNORM_ALPHA = (2.0 * DEPTH) ** 0.25
LN_EPS = 1e-5

MXU_EDGE = 256
LANES = 128
SUBLANES = 8

ROW_TILE = 512
MOE_CHUNK = SEQ
N_CHUNKS = BATCH + DEC_BATCH * DEC_SEQ // MOE_CHUNK
SAMPLE_CHUNK = BATCH
MOE_TM = 256
MOE_ROWS = 2 * MOE_CHUNK + N_EXPERTS * MOE_TM
MOE_TILES = MOE_ROWS // MOE_TM
COMBINE_STEPS = MOE_CHUNK // ROW_TILE
JUNK_PAIR = 2 * MOE_CHUNK
SAMPLE_SEGS = ROW_TILE // DEC_SEQ
NEG_BIG = -1e30
VMEM_LIMIT = 52 * 1024 * 1024

assert DEC_BATCH * DEC_SEQ == MOE_CHUNK and MOE_TILES <= LANES


def _layer_norm(x, g, b):
    mu = jnp.mean(x, -1, keepdims=True)
    xc = x - mu
    var = jnp.mean(xc * xc, -1, keepdims=True)
    return xc * lax.rsqrt(var + LN_EPS) * g + b


def _finish_mixer(x, y, g_ref, b_ref, x1_ref):
    x1_ref[...] = _layer_norm(DEEPNORM_ALPHA * x + y, g_ref[...], b_ref[...])


def _const_spec(shape):
    nd = len(shape)
    return pl.BlockSpec(shape, lambda *_: (0,) * nd)


_ACT_SHAPE = jax.ShapeDtypeStruct((N_CHUNKS, MOE_CHUNK, D_MODEL), jnp.float32)


def _lru_tile(x, ext_scr, h_in, n_seg, seg, win_ref, cw_ref, cb_ref, wr_ref, wi_ref,
              br_ref, bi_ref, lam_ref, wout_ref):
    rows = n_seg * seg
    xb = x.astype(jnp.bfloat16)
    rec = jnp.dot(xb, win_ref[:, D_RNN:], preferred_element_type=jnp.float32)
    cw = cw_ref[...]
    xcs = []
    for s in range(n_seg):
        ext_scr[s, SUBLANES:SUBLANES + seg, :] = rec[s * seg:(s + 1) * seg]
        xc = cb_ref[...] + cw[3:4] * rec[s * seg:(s + 1) * seg]
        for k in range(CONV_WIDTH - 1):
            off = SUBLANES - (CONV_WIDTH - 1) + k
            xc = xc + cw[k:k + 1] * ext_scr[s, off:off + seg, :]
        xcs.append(xc)
    xc = xcs[0] if n_seg == 1 else jnp.concatenate(xcs, axis=0)

    xcb = xc.astype(jnp.bfloat16)
    r_parts, i_parts = [], []
    for q in range(D_RNN // MXU_EDGE):
        blk = xcb[:, q * MXU_EDGE:(q + 1) * MXU_EDGE]
        r_parts.append(jnp.dot(blk, wr_ref[q], preferred_element_type=jnp.float32))
        i_parts.append(jnp.dot(blk, wi_ref[q], preferred_element_type=jnp.float32))
    r = jax.nn.sigmoid(jnp.concatenate(r_parts, axis=1) + br_ref[...])
    ig = jax.nn.sigmoid(jnp.concatenate(i_parts, axis=1) + bi_ref[...])

    z = -lam_ref[...]
    softplus = jnp.maximum(z, 0.0) + jnp.log1p(jnp.exp(-jnp.abs(z)))
    log_a = (-LRU_C * r) * softplus
    a = jnp.exp(log_a)
    b = jnp.sqrt(-jnp.tanh(log_a) * (a * a + 1.0)) * (ig * xc)

    sub = lax.broadcasted_iota(jnp.int32, (rows, D_RNN), 0) & (SUBLANES - 1)
    step = 1
    while step < SUBLANES:
        keep = sub >= step
        a_prev = pltpu.roll(a, step, 0)
        b_prev = pltpu.roll(b, step, 0)
        b = jnp.where(keep, a * b_prev + b, b)
        a = jnp.where(keep, a * a_prev, a)
        step *= 2
    groups_per_seg = seg // SUBLANES
    h_blocks, h_last = [], []
    carry = None
    for gi in range(rows // SUBLANES):
        if gi % groups_per_seg == 0:
            carry = h_in[gi // groups_per_seg]
        lo = gi * SUBLANES
        hb = b[lo:lo + SUBLANES] + a[lo:lo + SUBLANES] * carry
        carry = hb[SUBLANES - 1:SUBLANES]
        h_blocks.append(hb)
        if gi % groups_per_seg == groups_per_seg - 1:
            h_last.append(carry)
    h = jnp.concatenate(h_blocks, axis=0)

    gate_in = jnp.dot(xb, win_ref[:, :D_RNN], preferred_element_type=jnp.float32)
    y = h * jax.nn.gelu(gate_in, approximate=True)
    out = jnp.dot(y.astype(jnp.bfloat16), wout_ref[...], preferred_element_type=jnp.float32)
    return out, rec, h_last


def _lru_kernel(xp_ref, xs_ref, conv_in_ref, h0_ref, win_ref, cw_ref, cb_ref, wr_ref, wi_ref, br_ref,
                bi_ref, lam_ref, wout_ref, g_ref, b_ref, x1_ref, conv_p_ref, h_p_ref, conv_s_ref, h_s_ref,
                ext_p_scr, h_scr, ext_s_scr):
    c = pl.program_id(0)
    t = pl.program_id(1)
    weights = (win_ref, cw_ref, cb_ref, wr_ref, wi_ref, br_ref, bi_ref, lam_ref, wout_ref)

    @pl.when(c < BATCH)
    def _():
        @pl.when(t == 0)
        def _():
            ext_p_scr[0, 0:SUBLANES, :] = jnp.zeros((SUBLANES, D_RNN), jnp.float32)
            h_scr[...] = jnp.zeros_like(h_scr)

        x = xp_ref[...]
        out, rec, h_last = _lru_tile(x, ext_p_scr, [h_scr[...]], 1, ROW_TILE, *weights)
        ext_p_scr[0, 0:SUBLANES, :] = rec[ROW_TILE - SUBLANES:]
        h_scr[...] = h_last[0]
        conv_p_ref[...] = rec[ROW_TILE - (CONV_WIDTH - 1):]
        h_p_ref[...] = h_last[0]
        _finish_mixer(x, out, g_ref, b_ref, x1_ref)

    @pl.when(c == SAMPLE_CHUNK)
    def _():
        for s in range(SAMPLE_SEGS):
            ext_s_scr[s, 0:SUBLANES, :] = conv_in_ref[s]
        h_in = [h0_ref[s:s + 1, :] for s in range(SAMPLE_SEGS)]
        x = xs_ref[...]
        out, rec, h_last = _lru_tile(x, ext_s_scr, h_in, SAMPLE_SEGS, DEC_SEQ, *weights)
        for s in range(SAMPLE_SEGS):
            end = (s + 1) * DEC_SEQ
            conv_s_ref[s] = rec[end - (CONV_WIDTH - 1):end]
            h_s_ref[s:s + 1, :] = h_last[s]
        _finish_mixer(x, out, g_ref, b_ref, x1_ref)


_TILES_PER_CHUNK = MOE_CHUNK // ROW_TILE


def _prompt_tile(c, t):
    return jnp.where(c < BATCH, t, _TILES_PER_CHUNK - 1)


def _sample_tile(c, t):
    return jnp.where(c == SAMPLE_CHUNK, t, 0)


def _mixer_act_specs():
    return [
        pl.BlockSpec((None, ROW_TILE, D_MODEL), lambda c, t: (jnp.minimum(c, BATCH - 1), _prompt_tile(c, t), 0)),
        pl.BlockSpec((None, ROW_TILE, D_MODEL), lambda c, t: (0, _sample_tile(c, t), 0)),
    ]


_MIXER_GRID = (N_CHUNKS, _TILES_PER_CHUNK)
_PER_PROMPT = lambda c, t: (jnp.minimum(c, BATCH - 1), 0, 0)
_PER_SAMPLE3 = lambda c, t: (_sample_tile(c, t), 0, 0)
_PER_SAMPLE4 = lambda c, t: (_sample_tile(c, t), 0, 0, 0)


def _lru(xp, xs, conv_pad, h0, w, ln_g, ln_b):
    nt_s = DEC_BATCH // SAMPLE_SEGS
    return pl.pallas_call(
        _lru_kernel,
        grid=_MIXER_GRID,
        in_specs=_mixer_act_specs() + [
            pl.BlockSpec((None, SAMPLE_SEGS, SUBLANES, D_RNN), _PER_SAMPLE4),
            pl.BlockSpec((None, SAMPLE_SEGS, D_RNN), _PER_SAMPLE3),
            _const_spec((D_MODEL, 2 * D_RNN)),
            _const_spec((CONV_WIDTH, D_RNN)),
            _const_spec((1, D_RNN)),
            _const_spec((D_RNN // MXU_EDGE, MXU_EDGE, MXU_EDGE)),
            _const_spec((D_RNN // MXU_EDGE, MXU_EDGE, MXU_EDGE)),
            _const_spec((1, D_RNN)),
            _const_spec((1, D_RNN)),
            _const_spec((1, D_RNN)),
            _const_spec((D_RNN, D_MODEL)),
            _const_spec((1, D_MODEL)),
            _const_spec((1, D_MODEL)),
        ],
        out_specs=[
            pl.BlockSpec((None, ROW_TILE, D_MODEL), lambda c, t: (c, t, 0)),
            pl.BlockSpec((None, CONV_WIDTH - 1, D_RNN), _PER_PROMPT),
            pl.BlockSpec((None, 1, D_RNN), _PER_PROMPT),
            pl.BlockSpec((None, SAMPLE_SEGS, CONV_WIDTH - 1, D_RNN), _PER_SAMPLE4),
            pl.BlockSpec((None, SAMPLE_SEGS, D_RNN), _PER_SAMPLE3),
        ],
        out_shape=[
            _ACT_SHAPE,
            jax.ShapeDtypeStruct((BATCH, CONV_WIDTH - 1, D_RNN), jnp.float32),
            jax.ShapeDtypeStruct((BATCH, 1, D_RNN), jnp.float32),
            jax.ShapeDtypeStruct((nt_s, SAMPLE_SEGS, CONV_WIDTH - 1, D_RNN), jnp.float32),
            jax.ShapeDtypeStruct((nt_s, SAMPLE_SEGS, D_RNN), jnp.float32),
        ],
        scratch_shapes=[
            pltpu.VMEM((1, SUBLANES + ROW_TILE, D_RNN), jnp.float32),
            pltpu.VMEM((1, D_RNN), jnp.float32),
            pltpu.VMEM((SAMPLE_SEGS, SUBLANES + DEC_SEQ, D_RNN), jnp.float32),
        ],
        compiler_params=pltpu.CompilerParams(
            dimension_semantics=("arbitrary", "arbitrary"), vmem_limit_bytes=VMEM_LIMIT),
        name="lru",
    )(xp, xs, conv_pad, h0, *w, ln_g, ln_b)


def _rope(x, cos, sin_signed):
    lane = lax.broadcasted_iota(jnp.int32, (x.shape[0], LANES), 1)
    first_half = (lane & (HEAD_DIM - 1)) < HEAD_DIM // 2
    cols = []
    for c in range(x.shape[1] // LANES):
        xc = x[:, c * LANES:(c + 1) * LANES]
        partner = jnp.where(first_half, pltpu.roll(xc, LANES - HEAD_DIM // 2, 1),
                            pltpu.roll(xc, HEAD_DIM // 2, 1))
        cols.append(xc * cos + partner * sin_signed)
    return cols[0] if len(cols) == 1 else jnp.concatenate(cols, axis=1)


def _dup_heads(kv):
    lane = lax.broadcasted_iota(jnp.int32, kv.shape, 1)
    low = lane < HEAD_DIM
    swapped = pltpu.roll(kv, HEAD_DIM, 1)
    return (jnp.where(low, kv, swapped).astype(jnp.bfloat16),
            jnp.where(low, swapped, kv).astype(jnp.bfloat16))


def _attend(chunks, sinks_ref):
    lane = lax.broadcasted_iota(jnp.int32, (CHUNK, LANES), 1)
    low = lane < HEAD_DIM
    zero = jnp.zeros((CHUNK, LANES), jnp.bfloat16)
    cols_per_group = KV_GROUP * HEAD_DIM // LANES
    units = [(ci, g) for ci in range(len(chunks)) for g in range(N_KV_HEADS)]

    scores = {}
    for ci, g in units:
        q, k_wins, _, valid = chunks[ci]
        stacked = []
        for c in range(g * cols_per_group, (g + 1) * cols_per_group):
            qc = q[:, c * LANES:(c + 1) * LANES]
            stacked.append(jnp.where(low, qc, zero))
            stacked.append(jnp.where(low, zero, qc))
        qst = jnp.concatenate(stacked, axis=0)
        s = lax.dot_general(qst, k_wins[g], (((1,), (1,)), ((), ())),
                            preferred_element_type=jnp.float32)
        scores[ci, g] = s if valid is None else jnp.where(valid, s, NEG_BIG)

    probs = {}
    for ci, g in units:
        s = scores[ci, g]
        heads = []
        for hh in range(KV_GROUP):
            sh = s[hh * CHUNK:(hh + 1) * CHUNK]
            sink = sinks_ref[g * KV_GROUP + hh]
            m = jnp.maximum(jnp.max(sh, -1, keepdims=True), sink)
            p = jnp.exp(sh - m)
            denom = jnp.sum(p, -1, keepdims=True) + jnp.exp(sink - m)
            heads.append((p * (1.0 / denom)).astype(jnp.bfloat16))
        probs[ci, g] = jnp.concatenate(heads, axis=0)

    out_cols = {ci: [] for ci in range(len(chunks))}
    for ci, g in units:
        o = jnp.dot(probs[ci, g], chunks[ci][2][g], preferred_element_type=jnp.float32)
        for j in range(cols_per_group):
            oa = o[(2 * j) * CHUNK:(2 * j + 1) * CHUNK]
            ob = o[(2 * j + 1) * CHUNK:(2 * j + 2) * CHUNK]
            out_cols[ci].append(jnp.where(low, oa, ob))
    return [jnp.concatenate(out_cols[ci], axis=1) for ci in range(len(chunks))]


def _project_qkv(x, wqkv_ref, bqkv_ref, cos, sin):
    nq = N_HEADS * HEAD_DIM
    nk = N_KV_HEADS * HEAD_DIM
    xb = x.astype(jnp.bfloat16)
    kv = jnp.dot(xb, wqkv_ref[:, nq:], preferred_element_type=jnp.float32) + bqkv_ref[:, nq:]
    k = _rope(kv[:, :nk], cos, sin)
    v = kv[:, nk:]
    q = jnp.dot(xb, wqkv_ref[:, :nq], preferred_element_type=jnp.float32) + bqkv_ref[:, :nq]
    q = _rope(q, cos, sin)
    qs = (q * (HEAD_DIM ** -0.5)).astype(jnp.bfloat16)
    return qs, k, v


SUB_TILE = 256


def _sub_tiles():
    return [pl.ds(i * SUB_TILE, SUB_TILE) for i in range(ROW_TILE // SUB_TILE)]


def _project_out(xs, subs, o, wo_ref, g_ref, b_ref, x1_ref):
    for x, sub, i in zip(xs, subs, range(len(subs))):
        out = jnp.dot(o[i * SUB_TILE:(i + 1) * SUB_TILE].astype(jnp.bfloat16), wo_ref[...],
                      preferred_element_type=jnp.float32)
        x1_ref[sub, :] = _layer_norm(DEEPNORM_ALPHA * x + out, g_ref[...], b_ref[...])


def _swa_kernel(xp_ref, xs_ref, ck_ref, cv_ref, wqkv_ref, bqkv_ref, cosp_ref, sinp_ref, coss_ref, sins_ref,
                sinks_ref, wo_ref, g_ref, b_ref, x1_ref, kp_ref, vp_ref, ks_ref, vs_ref,
                k0_scr, k1_scr, v0_scr, v1_scr):
    c = pl.program_id(0)
    t = pl.program_id(1)
    scrs = (k0_scr, k1_scr, v0_scr, v1_scr)

    @pl.when(c < BATCH)
    def _():
        @pl.when(t == 0)
        def _():
            for scr in scrs:
                scr[0:WINDOW, :] = jnp.zeros((WINDOW, LANES), jnp.bfloat16)

        subs = _sub_tiles()
        xs_ = [xp_ref[sub, :] for sub in subs]
        proj = [_project_qkv(x, wqkv_ref, bqkv_ref, cosp_ref[sub, :], sinp_ref[sub, :]) for x, sub in zip(xs_, subs)]
        qs = jnp.concatenate([p[0] for p in proj], axis=0)
        k = jnp.concatenate([p[1] for p in proj], axis=0)
        v = jnp.concatenate([p[2] for p in proj], axis=0)
        kp_ref[...] = k[ROW_TILE - WINDOW:]
        vp_ref[...] = v[ROW_TILE - WINDOW:]
        k0, k1 = _dup_heads(k)
        v0, v1 = _dup_heads(v)
        for scr, val in zip(scrs, (k0, k1, v0, v1)):
            scr[WINDOW:WINDOW + ROW_TILE, :] = val

        span = WINDOW + CHUNK
        key_pos = lax.broadcasted_iota(jnp.int32, (1, span), 1)
        chunks = []
        for ci in range(ROW_TILE // CHUNK):
            lo = ci * CHUNK
            valid = (t * ROW_TILE + lo - WINDOW + key_pos) >= 0
            k_wins = (k0_scr[lo:lo + span, :], k1_scr[lo:lo + span, :])
            v_wins = (v0_scr[lo:lo + span, :], v1_scr[lo:lo + span, :])
            chunks.append((qs[lo:lo + CHUNK], k_wins, v_wins, valid))
        o = jnp.concatenate(_attend(chunks, sinks_ref), axis=0)
        for scr in scrs:
            scr[0:WINDOW, :] = scr[ROW_TILE:ROW_TILE + WINDOW, :]
        _project_out(xs_, subs, o, wo_ref, g_ref, b_ref, x1_ref)

    @pl.when(c == SAMPLE_CHUNK)
    def _():
        subs = _sub_tiles()
        xs_ = [xs_ref[sub, :] for sub in subs]
        proj = [_project_qkv(x, wqkv_ref, bqkv_ref, coss_ref[sub, :], sins_ref[sub, :]) for x, sub in zip(xs_, subs)]
        qs = jnp.concatenate([p[0] for p in proj], axis=0)
        k = jnp.concatenate([p[1] for p in proj], axis=0)
        v = jnp.concatenate([p[2] for p in proj], axis=0)
        knew = _dup_heads(k)
        vnew = _dup_heads(v)
        chunks = []
        for s in range(SAMPLE_SEGS):
            lo = s * DEC_SEQ
            ck = ck_ref[s]
            cv = cv_ref[s]
            kold = _dup_heads(ck)
            vold = _dup_heads(cv)
            k_wins = [jnp.concatenate([kold[g], knew[g][lo:lo + DEC_SEQ]], axis=0) for g in range(N_KV_HEADS)]
            v_wins = [jnp.concatenate([vold[g], vnew[g][lo:lo + DEC_SEQ]], axis=0) for g in range(N_KV_HEADS)]
            chunks.append((qs[lo:lo + DEC_SEQ], k_wins, v_wins, None))
            ks_ref[s, 0:WINDOW - DEC_SEQ, :] = ck[DEC_SEQ:]
            ks_ref[s, WINDOW - DEC_SEQ:WINDOW, :] = k[lo:lo + DEC_SEQ]
            vs_ref[s, 0:WINDOW - DEC_SEQ, :] = cv[DEC_SEQ:]
            vs_ref[s, WINDOW - DEC_SEQ:WINDOW, :] = v[lo:lo + DEC_SEQ]
        o = jnp.concatenate(_attend(chunks, sinks_ref), axis=0)
        _project_out(xs_, subs, o, wo_ref, g_ref, b_ref, x1_ref)


def _swa(xp, xs, ck, cv, w, cos_p, sin_p, cos_s, sin_s, ln_g, ln_b):
    wqkv, bqkv, sinks, wo = w
    nt_s = DEC_BATCH // SAMPLE_SEGS
    kv_lanes = N_KV_HEADS * HEAD_DIM
    cache_spec = pl.BlockSpec((None, SAMPLE_SEGS, WINDOW, kv_lanes), _PER_SAMPLE4)
    prompt_table = pl.BlockSpec((ROW_TILE, LANES), lambda c, t: (_prompt_tile(c, t), 0))
    kv_prompt = pl.BlockSpec((None, WINDOW, kv_lanes), _PER_PROMPT)
    return pl.pallas_call(
        _swa_kernel,
        grid=_MIXER_GRID,
        in_specs=_mixer_act_specs() + [
            cache_spec, cache_spec,
            _const_spec((D_MODEL, QKV_DIM)),
            _const_spec((1, QKV_DIM)),
            prompt_table, prompt_table,
            _const_spec((ROW_TILE, LANES)), _const_spec((ROW_TILE, LANES)),
            pl.BlockSpec(memory_space=pltpu.SMEM),
            _const_spec((N_HEADS * HEAD_DIM, D_MODEL)),
            _const_spec((1, D_MODEL)),
            _const_spec((1, D_MODEL)),
        ],
        out_specs=[
            pl.BlockSpec((None, ROW_TILE, D_MODEL), lambda c, t: (c, t, 0)),
            kv_prompt, kv_prompt, cache_spec, cache_spec,
        ],
        out_shape=[
            _ACT_SHAPE,
            jax.ShapeDtypeStruct((BATCH, WINDOW, kv_lanes), jnp.float32),
            jax.ShapeDtypeStruct((BATCH, WINDOW, kv_lanes), jnp.float32),
            jax.ShapeDtypeStruct((nt_s, SAMPLE_SEGS, WINDOW, kv_lanes), jnp.float32),
            jax.ShapeDtypeStruct((nt_s, SAMPLE_SEGS, WINDOW, kv_lanes), jnp.float32),
        ],
        scratch_shapes=[pltpu.VMEM((WINDOW + ROW_TILE, LANES), jnp.bfloat16) for _ in range(4)],
        compiler_params=pltpu.CompilerParams(
            dimension_semantics=("arbitrary", "arbitrary"), vmem_limit_bytes=VMEM_LIMIT),
        name="swa",
    )(xp, xs, ck, cv, wqkv, bqkv, cos_p, sin_p, cos_s, sin_s, sinks, wo, ln_g, ln_b)


(META_TILE_EXPERT, META_N_ACTIVE, META_FILL_LO, META_FILL_HI, META_IS_NEW, META_TILE_RUN, META_RUN_EXPERT,
 META_N_RUNS) = range(SUBLANES)
WEIGHT_SLOTS = 3


def _router_kernel(x_ref, wr_ref, br_ref, pos_ref, gate_ref, meta_ref):
    f32 = jnp.float32
    x = x_ref[...]
    w = wr_ref[...]
    xh = x.astype(jnp.bfloat16)
    xl = (x - xh.astype(f32)).astype(jnp.bfloat16)
    wh = w.astype(jnp.bfloat16)
    wl = (w - wh.astype(f32)).astype(jnp.bfloat16)
    nt = (((1,), (1,)), ((), ()))
    logits = (lax.dot_general(wh, xh, nt, preferred_element_type=f32)
              + lax.dot_general(wl, xh, nt, preferred_element_type=f32)
              + lax.dot_general(wh, xl, nt, preferred_element_type=f32))
    m = jnp.max(logits, axis=0, keepdims=True)
    e = jnp.exp(logits - m)
    probs = e / jnp.sum(e, axis=0, keepdims=True)
    sel = probs + br_ref[...]
    p = [probs[i:i + 1] for i in range(N_EXPERTS)]
    s = [sel[i:i + 1] for i in range(N_EXPERTS)]

    def first_argmax(vals):
        best, arg = vals[0], jnp.zeros_like(vals[0], dtype=jnp.int32)
        for i in range(1, len(vals)):
            better = vals[i] > best
            best = jnp.where(better, vals[i], best)
            arg = jnp.where(better, i, arg)
        return arg

    group_scores = []
    for gidx in range(N_GROUPS):
        v = s[gidx * EXPERTS_PER_GROUP:(gidx + 1) * EXPERTS_PER_GROUP]
        best = v[0] + v[1]
        for i in range(EXPERTS_PER_GROUP):
            for j in range(i + 1, EXPERTS_PER_GROUP):
                if (i, j) != (0, 1):
                    best = jnp.maximum(best, v[i] + v[j])
        group_scores.append(best)
    best_group = first_argmax(group_scores)
    neg_inf = jnp.full_like(s[0], -jnp.inf)
    masked = [jnp.where(best_group == (i // EXPERTS_PER_GROUP), s[i], neg_inf) for i in range(N_EXPERTS)]
    idx1 = first_argmax(masked)
    masked2 = [jnp.where(idx1 == i, neg_inf, masked[i]) for i in range(N_EXPERTS)]
    idx2 = first_argmax(masked2)
    zero = jnp.zeros_like(p[0])
    w1, w2 = zero, zero
    for i in range(N_EXPERTS):
        w1 = w1 + jnp.where(idx1 == i, p[i], zero)
        w2 = w2 + jnp.where(idx2 == i, p[i], zero)
    tot = w1 + w2
    gate_ref[0:1, :] = w1 / tot
    gate_ref[1:2, :] = w2 / tot

    expert = lax.broadcasted_iota(jnp.int32, (N_EXPERTS, MOE_CHUNK), 0)
    strict_upper = (lax.broadcasted_iota(jnp.int32, (LANES, LANES), 0)
                    < lax.broadcasted_iota(jnp.int32, (LANES, LANES), 1)).astype(jnp.bfloat16)
    carry = jnp.zeros((N_EXPERTS, 1), f32)
    onehots, ranks = [], []
    for idx in (idx1, idx2):
        onehot = (expert == idx).astype(f32)
        before = []
        for blk in range(MOE_CHUNK // LANES):
            oh = onehot[:, blk * LANES:(blk + 1) * LANES]
            before.append(jnp.dot(oh.astype(jnp.bfloat16), strict_upper, preferred_element_type=f32) + carry)
            carry = carry + jnp.sum(oh, axis=1, keepdims=True)
        onehots.append(onehot)
        ranks.append(jnp.sum(onehot * jnp.concatenate(before, axis=1), axis=0, keepdims=True))
    counts = carry
    padded = jnp.floor((counts + (MOE_TM - 1)) * (1.0 / MOE_TM)) * MOE_TM
    strict_lower = (lax.broadcasted_iota(jnp.int32, (N_EXPERTS, N_EXPERTS), 0)
                    > lax.broadcasted_iota(jnp.int32, (N_EXPERTS, N_EXPERTS), 1)).astype(jnp.bfloat16)
    starts = jnp.dot(strict_lower, jnp.broadcast_to(padded, (N_EXPERTS, LANES)).astype(jnp.bfloat16),
                     preferred_element_type=f32)[:, 0:1]
    ends = starts + padded
    for k in range(2):
        pos = jnp.sum(onehots[k] * starts, axis=0, keepdims=True) + ranks[k]
        pos_ref[k:k + 1, :] = pos.astype(jnp.int32)

    lane = lax.broadcasted_iota(jnp.int32, (N_EXPERTS, LANES), 1)
    sub = lax.broadcasted_iota(jnp.int32, (N_EXPERTS, LANES), 0)
    tile_start = (lane * MOE_TM).astype(f32)
    tile_expert = jnp.sum((ends <= tile_start).astype(f32), axis=0, keepdims=True)
    tile_expert = jnp.minimum(tile_expert, N_EXPERTS - 1.0)
    on_diag = sub == lane

    def to_lanes(col):
        return jnp.sum(jnp.where(on_diag, col, 0.0), axis=0, keepdims=True)

    n_active = jnp.broadcast_to(ends[N_EXPERTS - 1:N_EXPERTS] * (1.0 / MOE_TM), (1, LANES))
    lane1 = lax.broadcasted_iota(jnp.int32, (1, LANES), 1)
    active = lane1.astype(f32) < n_active
    starts_run = jnp.logical_or(lane1 == 0, tile_expert != pltpu.roll(tile_expert, 1, 1))
    is_new = jnp.where(jnp.logical_and(active, starts_run), 1.0, 0.0)
    upper_incl = (lax.broadcasted_iota(jnp.int32, (LANES, LANES), 0)
                  <= lax.broadcasted_iota(jnp.int32, (LANES, LANES), 1)).astype(jnp.bfloat16)
    tile_run = jnp.dot(jnp.broadcast_to(is_new, (SUBLANES, LANES)).astype(jnp.bfloat16), upper_incl,
                       preferred_element_type=f32)[0:1] - 1.0
    first_of_run = jnp.logical_and(is_new > 0.0, tile_run == sub.astype(f32))
    run_expert = jnp.sum(jnp.where(first_of_run, tile_expert, 0.0), axis=1, keepdims=True)
    n_runs = jnp.broadcast_to(jnp.sum(is_new, axis=1, keepdims=True), (1, LANES))

    meta_ref[...] = jnp.zeros(meta_ref.shape, jnp.int32)
    meta_ref[META_IS_NEW:META_IS_NEW + 1, :] = is_new.astype(jnp.int32)
    meta_ref[META_TILE_RUN:META_TILE_RUN + 1, :] = tile_run.astype(jnp.int32)
    meta_ref[META_RUN_EXPERT:META_RUN_EXPERT + 1, :] = to_lanes(run_expert).astype(jnp.int32)
    meta_ref[META_N_RUNS:META_N_RUNS + 1, :] = n_runs.astype(jnp.int32)
    meta_ref[META_TILE_EXPERT:META_TILE_EXPERT + 1, :] = tile_expert.astype(jnp.int32)
    meta_ref[META_N_ACTIVE:META_N_ACTIVE + 1, :] = n_active.astype(jnp.int32)
    meta_ref[META_FILL_LO:META_FILL_LO + 1, :] = to_lanes(starts + counts).astype(jnp.int32)
    meta_ref[META_FILL_HI:META_FILL_HI + 1, :] = to_lanes(ends).astype(jnp.int32)


def _router(x, wr_t, br):
    per_chunk = lambda c: (c, 0, 0)
    return pl.pallas_call(
        _router_kernel,
        grid=(N_CHUNKS,),
        in_specs=[
            pl.BlockSpec((None, MOE_CHUNK, D_MODEL), per_chunk),
            _const_spec((N_EXPERTS, D_MODEL)),
            _const_spec((N_EXPERTS, 1)),
        ],
        out_specs=[
            pl.BlockSpec((None, 2, MOE_CHUNK), per_chunk),
            pl.BlockSpec((None, 2, MOE_CHUNK), per_chunk),
            pl.BlockSpec((None, SUBLANES, LANES), per_chunk),
        ],
        out_shape=[
            jax.ShapeDtypeStruct((N_CHUNKS, 2, MOE_CHUNK), jnp.int32),
            jax.ShapeDtypeStruct((N_CHUNKS, 2, MOE_CHUNK), jnp.float32),
            jax.ShapeDtypeStruct((N_CHUNKS, SUBLANES, LANES), jnp.int32),
        ],
        compiler_params=pltpu.CompilerParams(
            dimension_semantics=("parallel",), vmem_limit_bytes=VMEM_LIMIT),
        name="router",
    )(x, wr_t, br)


def _gather_rows(pair_scr, x_ref, xs_scr, row0, j):
    tok = pair_scr[row0 + j] & (MOE_CHUNK - 1)
    xs_scr[j // SUBLANES, (j % SUBLANES):(j % SUBLANES) + 1, :] = x_ref[pl.ds(tok, 1), :]


def _scatter_rows(pair_scr, y_scr, yo_scr, row0, j):
    y_scr[pl.ds(pair_scr[row0 + j], 1), :] = yo_scr[j // SUBLANES, (j % SUBLANES):(j % SUBLANES) + 1, :]


def _moe_kernel(layer, meta_ref, pos_ref, x_ref, gate_ref,
                wg_hbm, wu_hbm, wd_hbm, g_ref, b_ref, outp_hbm, outs_hbm,
                pair_scr, xs_a, xs_b, yo_a, yo_b, y_scr, wg_buf, wu_buf, wd_buf, w_sem, out_sem):
    c = pl.program_id(0)

    def plan(row, i):
        return meta_ref[(c * SUBLANES + row) * LANES + i]

    n_act = plan(META_N_ACTIVE, 0)

    n_runs = plan(META_N_RUNS, 0)

    def weight_copies(run):
        e, s = plan(META_RUN_EXPERT, run), lax.rem(run, WEIGHT_SLOTS)
        return (pltpu.make_async_copy(wg_hbm.at[layer, e], wg_buf.at[s], w_sem.at[s, 0]),
                pltpu.make_async_copy(wu_hbm.at[layer, e], wu_buf.at[s], w_sem.at[s, 1]),
                pltpu.make_async_copy(wd_hbm.at[layer, e], wd_buf.at[s], w_sem.at[s, 2]))

    def prompt_out_copy(chunk):
        return pltpu.make_async_copy(y_scr.at[pl.ds(0, MOE_CHUNK)], outp_hbm.at[chunk], out_sem.at[0])

    def sample_out_copy():
        return pltpu.make_async_copy(y_scr.at[pl.ds(0, MOE_CHUNK)], outs_hbm.at[0], out_sem.at[0])

    for ahead in range(WEIGHT_SLOTS - 1):
        @pl.when(ahead < n_runs)
        def _():
            for cp in weight_copies(ahead):
                cp.start()

    def fill8(g, base):
        for u in range(SUBLANES):
            pair_scr[base + g * SUBLANES + u] = JUNK_PAIR
        return base

    def fill_expert(e, carry):
        lo, hi = plan(META_FILL_LO, e), plan(META_FILL_HI, e)
        lax.fori_loop(0, (hi - lo + SUBLANES - 1) >> 3, fill8, lo)
        return carry
    lax.fori_loop(0, N_EXPERTS, fill_expert, 0)
    lax.fori_loop(0, MOE_TM // SUBLANES, fill8, plan(META_FILL_HI, N_EXPERTS - 1))

    def invert(t, carry):
        pair_scr[pos_ref[t]] = t
        pair_scr[pos_ref[MOE_CHUNK + t]] = MOE_CHUNK + t
        return carry
    lax.fori_loop(0, MOE_CHUNK, invert, 0, unroll=8)

    def gather0(gi, carry):
        for u in range(SUBLANES):
            tok = pair_scr[gi * SUBLANES + u] & (MOE_CHUNK - 1)
            xs_a[gi, u:u + 1, :] = x_ref[pl.ds(tok, 1), :]
        return carry
    lax.fori_loop(0, MOE_TM // SUBLANES, gather0, 0)

    @pl.when(c == 0)
    def _():
        yo_b[...] = jnp.zeros_like(yo_b)

    @pl.when(c > 0)
    def _():
        prompt_out_copy(c - 1).wait()

    def expert_step(t, s, xs_cur, yo_cur, xs_nxt, yo_prv):
        row_nxt = (t + 1) * MOE_TM
        row_prv = jnp.maximum(t - 1, 0) * MOE_TM
        for j in range(MOE_TM):
            _scatter_rows(pair_scr, y_scr, yo_prv, row_prv, j)
            _gather_rows(pair_scr, x_ref, xs_nxt, row_nxt, j)
        xb = xs_cur[...].reshape(MOE_TM, D_MODEL).astype(jnp.bfloat16)
        hg = jnp.dot(xb, wg_buf[s], preferred_element_type=jnp.float32)
        hu = jnp.dot(xb, wu_buf[s], preferred_element_type=jnp.float32)
        act = (jax.nn.silu(hg) * hu).astype(jnp.bfloat16)
        yo = jnp.dot(act, wd_buf[s], preferred_element_type=jnp.float32)
        yo_cur[...] = yo.reshape(MOE_TM // SUBLANES, SUBLANES, D_MODEL)

    def tile_iteration(t, carry):
        run = plan(META_TILE_RUN, t)
        starts_run = plan(META_IS_NEW, t) == 1

        @pl.when(starts_run)
        def _():
            for cp in weight_copies(run):
                cp.wait()

        @pl.when(jnp.logical_and(starts_run, run + WEIGHT_SLOTS - 1 < n_runs))
        def _():
            for cp in weight_copies(run + WEIGHT_SLOTS - 1):
                cp.start()

        slot = lax.rem(run, WEIGHT_SLOTS)
        for s in range(WEIGHT_SLOTS):
            in_slot = slot == s

            @pl.when(jnp.logical_and((t & 1) == 0, in_slot))
            def _():
                expert_step(t, s, xs_a, yo_a, xs_b, yo_b)

            @pl.when(jnp.logical_and((t & 1) == 1, in_slot))
            def _():
                expert_step(t, s, xs_b, yo_b, xs_a, yo_a)
        return carry
    lax.fori_loop(0, n_act, tile_iteration, 0)

    def drain(yo_last):
        row_last = (n_act - 1) * MOE_TM

        def scatter(gi, carry):
            for u in range(SUBLANES):
                y_scr[pl.ds(pair_scr[row_last + gi * SUBLANES + u], 1), :] = yo_last[gi, u:u + 1, :]
            return carry
        lax.fori_loop(0, MOE_TM // SUBLANES, scatter, 0)

    @pl.when((n_act & 1) == 1)
    def _():
        drain(yo_a)

    @pl.when((n_act & 1) == 0)
    def _():
        drain(yo_b)

    def combine(k, carry):
        lo = pl.multiple_of(k * ROW_TILE, ROW_TILE)
        gates = gate_ref[pl.ds(lo, ROW_TILE), :]
        moe = (gates[:, 0:1] * y_scr[pl.ds(lo, ROW_TILE), :]
               + gates[:, 1:2] * y_scr[pl.ds(MOE_CHUNK + lo, ROW_TILE), :])
        x = x_ref[pl.ds(lo, ROW_TILE), :]
        y_scr[pl.ds(lo, ROW_TILE), :] = _layer_norm(DEEPNORM_ALPHA * x + moe, g_ref[...], b_ref[...])
        return carry
    lax.fori_loop(0, COMBINE_STEPS, combine, 0)

    @pl.when(c < SAMPLE_CHUNK)
    def _():
        prompt_out_copy(c).start()

    @pl.when(c == SAMPLE_CHUNK)
    def _():
        sample_out_copy().start()
        sample_out_copy().wait()


def _moe(x, meta, pos, gates_t, layer, wg, wu, wd, ln_g, ln_b):
    stage = pltpu.VMEM((MOE_TM // SUBLANES, SUBLANES, D_MODEL), jnp.float32)
    per_chunk = lambda c, *_: (c, 0, 0)
    grid_spec = pltpu.PrefetchScalarGridSpec(
        num_scalar_prefetch=1,
        grid=(N_CHUNKS,),
        in_specs=[
            pl.BlockSpec((None, None, 2 * MOE_CHUNK), per_chunk, memory_space=pltpu.SMEM),
            pl.BlockSpec((None, MOE_CHUNK, D_MODEL), per_chunk),
            pl.BlockSpec((None, MOE_CHUNK, 2), per_chunk),
            pl.BlockSpec(memory_space=pl.ANY),
            pl.BlockSpec(memory_space=pl.ANY),
            pl.BlockSpec(memory_space=pl.ANY),
            pl.BlockSpec((1, D_MODEL), lambda c, *_: (0, 0)),
            pl.BlockSpec((1, D_MODEL), lambda c, *_: (0, 0)),
        ],
        out_specs=[pl.BlockSpec(memory_space=pl.ANY), pl.BlockSpec(memory_space=pl.ANY)],
        scratch_shapes=[
            pltpu.SMEM((MOE_ROWS + MOE_TM + SUBLANES,), jnp.int32),
            stage, stage, stage, stage,
            pltpu.VMEM((2 * MOE_CHUNK + SUBLANES, D_MODEL), jnp.float32),
            pltpu.VMEM((WEIGHT_SLOTS, D_MODEL, D_EXPERT), jnp.bfloat16),
            pltpu.VMEM((WEIGHT_SLOTS, D_MODEL, D_EXPERT), jnp.bfloat16),
            pltpu.VMEM((WEIGHT_SLOTS, D_EXPERT, D_MODEL), jnp.bfloat16),
            pltpu.SemaphoreType.DMA((WEIGHT_SLOTS, 3)),
            pltpu.SemaphoreType.DMA((1,)),
        ],
    )
    return pl.pallas_call(
        functools.partial(_moe_kernel, layer),
        grid_spec=grid_spec,
        out_shape=[jax.ShapeDtypeStruct((BATCH, MOE_CHUNK, D_MODEL), jnp.float32),
                   jax.ShapeDtypeStruct((1, MOE_CHUNK, D_MODEL), jnp.float32)],
        compiler_params=pltpu.CompilerParams(
            dimension_semantics=("arbitrary",), vmem_limit_bytes=VMEM_LIMIT),
        name="moe",
    )(meta.reshape(-1), pos.reshape(N_CHUNKS, 1, 2 * MOE_CHUNK), x, gates_t, wg, wu, wd, ln_g, ln_b)


def _moe_layer(x, wr_t, br, layer, wg, wu, wd, ln_g, ln_b):
    pos, gates, meta = _router(x, wr_t, br)
    return _moe(x, meta, pos, jnp.swapaxes(gates, 1, 2), layer, wg, wu, wd, ln_g, ln_b)


def _rope_tables(pos):
    half = HEAD_DIM // 2
    inv_freq = ROPE_THETA ** (-jnp.arange(half, dtype=jnp.float32) / half)
    ang = pos.astype(jnp.float32)[:, None] * inv_freq[None, :]
    cos = jnp.cos(ang)
    sin = jnp.sin(ang)
    reps = LANES // HEAD_DIM
    cos_t = jnp.tile(jnp.concatenate([cos, cos], axis=1), (1, reps))
    sin_t = jnp.tile(jnp.concatenate([-sin, sin], axis=1), (1, reps))
    return cos_t, sin_t


def _block_diag_tiles(w):
    per = MXU_EDGE // LRU_BLOCK_W
    w4 = w.reshape(LRU_BLOCKS // per, per, LRU_BLOCK_W, LRU_BLOCK_W)
    eye = jnp.eye(per, dtype=w.dtype)
    return jnp.einsum("qaij,ab->qaibj", w4, eye).reshape(LRU_BLOCKS // per, MXU_EDGE, MXU_EDGE)


def kernel(x_prompt, x_sample, state_conv_0, state_h_0, cache_k_1, cache_v_1, state_conv_2, state_h_2,
           cache_k_3, cache_v_3, w_router, b_router, ln_g, ln_b, lru_w_in, lru_conv_w, lru_conv_b,
           lru_w_gates, lru_b_gates, lru_lambda, lru_w_out, attn_w_qkv, attn_b_qkv, attn_sinks, attn_w_o,
           moe_w_gate, moe_w_up, moe_w_down):
    bf = jnp.bfloat16
    in_state = {0: (state_conv_0, state_h_0), 1: (cache_k_1, cache_v_1),
                2: (state_conv_2, state_h_2), 3: (cache_k_3, cache_v_3)}
    nt_s = DEC_BATCH // SAMPLE_SEGS
    kv_lanes = N_KV_HEADS * HEAD_DIM

    wr_t = w_router.T
    br = b_router.reshape(N_EXPERTS, 1)
    cos_p, sin_p = _rope_tables(jnp.arange(SEQ))
    cos_s, sin_s = _rope_tables(PAST_LEN + jnp.arange(DEC_SEQ))
    cos_s = jnp.tile(cos_s, (SAMPLE_SEGS, 1))
    sin_s = jnp.tile(sin_s, (SAMPLE_SEGS, 1))
    wg_all = moe_w_gate.astype(bf)
    wu_all = moe_w_up.astype(bf)
    wd_all = moe_w_down.astype(bf)

    x = x_prompt
    x_s = x_sample.reshape(1, MOE_CHUNK, D_MODEL)
    new_p, new_s = {}, {}
    for i in range(DEPTH):
        slot = i // 2
        g0 = ln_g[i, 0].reshape(1, D_MODEL)
        b0 = ln_b[i, 0].reshape(1, D_MODEL)
        g1 = ln_g[i, 1].reshape(1, D_MODEL)
        b1 = ln_b[i, 1].reshape(1, D_MODEL)
        st_a, st_b = in_state[i]
        if i % 2 == 0:
            gates = lru_w_gates[slot]
            w = (lru_w_in[slot].astype(bf), lru_conv_w[slot], lru_conv_b[slot].reshape(1, D_RNN),
                 _block_diag_tiles(gates[:, :, :LRU_BLOCK_W]).astype(bf),
                 _block_diag_tiles(gates[:, :, LRU_BLOCK_W:]).astype(bf),
                 lru_b_gates[slot][:, :LRU_BLOCK_W].reshape(1, D_RNN),
                 lru_b_gates[slot][:, LRU_BLOCK_W:].reshape(1, D_RNN),
                 lru_lambda[slot].reshape(1, D_RNN), lru_w_out[slot].astype(bf))
            conv_pad = jnp.pad(st_a, ((0, 0), (SUBLANES - (CONV_WIDTH - 1), 0), (0, 0)))
            x1, conv_p, h_p, conv_s, h_s = _lru(
                x, x_s, conv_pad.reshape(nt_s, SAMPLE_SEGS, SUBLANES, D_RNN),
                st_b.reshape(nt_s, SAMPLE_SEGS, D_RNN), w, g0, b0)
            new_p[i] = (conv_p, h_p.reshape(BATCH, D_RNN))
            new_s[i] = (conv_s.reshape(DEC_BATCH, CONV_WIDTH - 1, D_RNN), h_s.reshape(DEC_BATCH, D_RNN))
        else:
            w = (attn_w_qkv[slot].astype(bf), attn_b_qkv[slot].reshape(1, QKV_DIM), attn_sinks[slot],
                 attn_w_o[slot].astype(bf))
            x1, k_p, v_p, k_s, v_s = _swa(
                x, x_s, st_a.reshape(nt_s, SAMPLE_SEGS, WINDOW, kv_lanes),
                st_b.reshape(nt_s, SAMPLE_SEGS, WINDOW, kv_lanes), w, cos_p, sin_p, cos_s, sin_s, g0, b0)
            new_p[i] = (k_p.reshape(BATCH, WINDOW, N_KV_HEADS, HEAD_DIM),
                        v_p.reshape(BATCH, WINDOW, N_KV_HEADS, HEAD_DIM))
            new_s[i] = (k_s.reshape(DEC_BATCH, WINDOW, N_KV_HEADS, HEAD_DIM),
                        v_s.reshape(DEC_BATCH, WINDOW, N_KV_HEADS, HEAD_DIM))
        x, x_s = _moe_layer(x1, wr_t, br, i, wg_all, wu_all, wd_all, g1, b1)
    return (x, x_s.reshape(DEC_BATCH, DEC_SEQ, D_MODEL),
            new_p[0][0], new_s[0][0], new_p[0][1], new_s[0][1],
            new_p[1][0], new_s[1][0], new_p[1][1], new_s[1][1],
            new_p[2][0], new_s[2][0], new_p[2][1], new_s[2][1],
            new_p[3][0], new_s[3][0], new_p[3][1], new_s[3][1])
```

```python
import functools

import jax
import jax.numpy as jnp
from jax import lax
from jax.experimental import pallas as pl
from jax.experimental.pallas import tpu as pltpu

D_MODEL = 1024
BATCH = 8
SEQ = 2048
DEPTH = 4
DEC_BATCH = 32
DEC_SEQ = 64
PAST_LEN = 4096
CHUNK = 64
D_RNN = D_MODEL
CONV_WIDTH = 4
LRU_BLOCKS = 16
LRU_BLOCK_W = D_RNN // LRU_BLOCKS
LRU_C = 8.0
N_HEADS = 16
N_KV_HEADS = 2
HEAD_DIM = 64
KV_GROUP = N_HEADS // N_KV_HEADS
WINDOW = 128
ROPE_THETA = 10000.0
QKV_DIM = (N_HEADS + 2 * N_KV_HEADS) * HEAD_DIM
N_EXPERTS = 16
N_GROUPS = 4
EXPERTS_PER_GROUP = N_EXPERTS // N_GROUPS
D_EXPERT = 512
DEEPNORM_ALPHA = (2.0 * DEPTH) ** 0.25
LN_EPS = 1e-5

MXU_EDGE = 256
LANES = 128
SUBLANES = 8

ROW_TILE = 512
MOE_CHUNK = SEQ
N_CHUNKS = BATCH + DEC_BATCH * DEC_SEQ // MOE_CHUNK
SAMPLE_CHUNK = BATCH
MOE_TM = 256
MOE_ROWS = 2 * MOE_CHUNK + N_EXPERTS * MOE_TM
MOE_TILES = MOE_ROWS // MOE_TM
COMBINE_STEPS = MOE_CHUNK // ROW_TILE
JUNK_PAIR = 2 * MOE_CHUNK
SAMPLE_SEGS = ROW_TILE // DEC_SEQ
NEG_BIG = -1e30
VMEM_LIMIT = 52 * 1024 * 1024

assert DEC_BATCH * DEC_SEQ == MOE_CHUNK and MOE_TILES <= LANES


def _layer_norm(x, g, b):
    mu = jnp.mean(x, -1, keepdims=True)
    xc = x - mu
    var = jnp.mean(xc * xc, -1, keepdims=True)
    return xc * lax.rsqrt(var + LN_EPS) * g + b


def _const_spec(shape):
    nd = len(shape)
    return pl.BlockSpec(shape, lambda *_: (0,) * nd)


_ACT_SHAPE = jax.ShapeDtypeStruct((N_CHUNKS, MOE_CHUNK, D_MODEL), jnp.float32)


LRU_SEGS = SUBLANES
LRU_SEG = ROW_TILE // LRU_SEGS
assert LRU_SEG == DEC_SEQ


def _lru_tile(x_tm, x_sm, pre, h_init, chain, perm_ref, win_ref, cw_ref, cb_ref, wr_ref, wi_ref, br_ref, bi_ref,
              lam_ref, wout_ref, g_ref, b_ref):
    rows = ROW_TILE
    xb = x_tm.astype(jnp.bfloat16)
    rec = jnp.dot(xb, win_ref[:, D_RNN:], preferred_element_type=jnp.float32)
    if chain:
        sub = lax.broadcasted_iota(jnp.int32, (SUBLANES, D_RNN), 0)
        pre = [jnp.where(sub == 0, pre[j], pltpu.roll(rec[rows - (CONV_WIDTH - 1 - j) * SUBLANES:
                                                          rows - (CONV_WIDTH - 2 - j) * SUBLANES], 1, 0))
               for j in range(CONV_WIDTH - 1)]
    ext = jnp.concatenate(pre + [rec], axis=0)
    cw = cw_ref[...]
    xc = cb_ref[...] + cw[CONV_WIDTH - 1:CONV_WIDTH] * rec
    for j in range(CONV_WIDTH - 1):
        xc = xc + cw[j:j + 1] * ext[j * SUBLANES:j * SUBLANES + rows]

    xcb = xc.astype(jnp.bfloat16)
    r_parts, i_parts = [], []
    for q in range(D_RNN // MXU_EDGE):
        blk = xcb[:, q * MXU_EDGE:(q + 1) * MXU_EDGE]
        r_parts.append(jnp.dot(blk, wr_ref[q], preferred_element_type=jnp.float32))
        i_parts.append(jnp.dot(blk, wi_ref[q], preferred_element_type=jnp.float32))
    r = jax.nn.sigmoid(jnp.concatenate(r_parts, axis=1) + br_ref[...])
    ig = jax.nn.sigmoid(jnp.concatenate(i_parts, axis=1) + bi_ref[...])

    z = -lam_ref[...]
    softplus = jnp.maximum(z, 0.0) + jnp.log1p(jnp.exp(-jnp.abs(z)))
    log_a = (-LRU_C * r) * softplus
    a = jnp.exp(log_a)
    b = jnp.sqrt(-jnp.tanh(log_a) * (a * a + 1.0)) * (ig * xc)

    steps = lambda v, t: v[t * SUBLANES:(t + 1) * SUBLANES]
    if chain:
        h = jnp.zeros((SUBLANES, D_RNN), jnp.float32)
        prod = jnp.ones((SUBLANES, D_RNN), jnp.float32)
        local, prods = [], []
        for t in range(LRU_SEG):
            h = steps(a, t) * h + steps(b, t)
            prod = steps(a, t) * prod
            local.append(h)
            prods.append(prod)
        carry, inits = h_init, []
        for s in range(LRU_SEGS):
            inits.append(carry)
            carry = prod[s:s + 1] * carry + h[s:s + 1]
        init = jnp.concatenate(inits, axis=0)
        hs = [local[t] + prods[t] * init for t in range(LRU_SEG)]
    else:
        h = h_init
        hs = []
        for t in range(LRU_SEG):
            h = steps(a, t) * h + steps(b, t)
            hs.append(h)
    h_end = hs[-1]

    gate_in = jnp.dot(xb, win_ref[:, :D_RNN], preferred_element_type=jnp.float32)
    y = (jnp.concatenate(hs, axis=0) * jax.nn.gelu(gate_in, approximate=True)).astype(jnp.bfloat16)
    y_sm = jnp.dot(perm_ref[...], y, preferred_element_type=jnp.float32).astype(jnp.bfloat16)
    out = jnp.dot(y_sm, wout_ref[...], preferred_element_type=jnp.float32)
    return _layer_norm(DEEPNORM_ALPHA * x_sm + out, g_ref[...], b_ref[...]), rec, h_end


def _lru_kernel(xp_ref, xs_ref, conv_in_ref, h0_ref, perm_ref, win_ref, cw_ref, cb_ref, wr_ref, wi_ref, br_ref,
                bi_ref, lam_ref, wout_ref, g_ref, b_ref, x1_ref, conv_p_ref, h_p_ref, conv_s_ref, h_s_ref,
                tail_scr, h_scr):
    c = pl.program_id(0)
    t = pl.program_id(1)
    weights = (perm_ref, win_ref, cw_ref, cb_ref, wr_ref, wi_ref, br_ref, bi_ref, lam_ref, wout_ref, g_ref, b_ref)
    rows = ROW_TILE

    def interleaved(x_ref):
        return jnp.concatenate([x_ref[:, g, :] for g in range(LRU_SEG)], axis=0)

    @pl.when(c < BATCH)
    def _():
        @pl.when(t == 0)
        def _():
            tail_scr[...] = jnp.zeros_like(tail_scr)
            h_scr[...] = jnp.zeros_like(h_scr)

        pre = [tail_scr[j:j + 1, :] for j in range(CONV_WIDTH - 1)]
        x1, rec, h_end = _lru_tile(interleaved(xp_ref), xp_ref[...].reshape(rows, D_MODEL), pre, h_scr[...], True,
                                   *weights)
        last = LRU_SEGS - 1
        for j in range(CONV_WIDTH - 1):
            row = (LRU_SEG - (CONV_WIDTH - 1) + j) * SUBLANES + last
            tail_scr[j:j + 1, :] = rec[row:row + 1]
            conv_p_ref[j:j + 1, :] = rec[row:row + 1]
        h_scr[...] = h_end[last:last + 1]
        h_p_ref[...] = h_end[last:last + 1]
        x1_ref[...] = x1

    @pl.when(c == SAMPLE_CHUNK)
    def _():
        pre = [conv_in_ref[:, j, :] for j in range(CONV_WIDTH - 1)]
        x1, rec, h_end = _lru_tile(interleaved(xs_ref), xs_ref[...].reshape(rows, D_MODEL), pre, h0_ref[...], False,
                                   *weights)
        for j in range(CONV_WIDTH - 1):
            lo = (LRU_SEG - (CONV_WIDTH - 1) + j) * SUBLANES
            conv_s_ref[:, j, :] = rec[lo:lo + SUBLANES]
        h_s_ref[...] = h_end
        x1_ref[...] = x1


_TILES_PER_CHUNK = MOE_CHUNK // ROW_TILE


def _prompt_tile(c, t):
    return jnp.where(c < BATCH, t, _TILES_PER_CHUNK - 1)


def _sample_tile(c, t):
    return jnp.where(c == SAMPLE_CHUNK, t, 0)


def _mixer_act_specs():
    return [
        pl.BlockSpec((None, ROW_TILE, D_MODEL), lambda c, t: (jnp.minimum(c, BATCH - 1), _prompt_tile(c, t), 0)),
        pl.BlockSpec((None, ROW_TILE, D_MODEL), lambda c, t: (0, _sample_tile(c, t), 0)),
    ]


_MIXER_GRID = (N_CHUNKS, _TILES_PER_CHUNK)
_PER_PROMPT = lambda c, t: (jnp.minimum(c, BATCH - 1), 0, 0)
_PER_SAMPLE3 = lambda c, t: (_sample_tile(c, t), 0, 0)
_PER_SAMPLE4 = lambda c, t: (_sample_tile(c, t), 0, 0, 0)


def _lru(xp, xs, conv_in, h0, w, ln_g, ln_b):
    seg_tile = (None, LRU_SEGS, LRU_SEG, D_MODEL)
    row = lax.broadcasted_iota(jnp.int32, (ROW_TILE, ROW_TILE), 0)
    col = lax.broadcasted_iota(jnp.int32, (ROW_TILE, ROW_TILE), 1)
    perm = (col == (row % LRU_SEG) * LRU_SEGS + row // LRU_SEG).astype(jnp.bfloat16)
    return pl.pallas_call(
        _lru_kernel,
        grid=_MIXER_GRID,
        in_specs=[
            pl.BlockSpec(seg_tile, lambda c, t: (jnp.minimum(c, BATCH - 1), _prompt_tile(c, t), 0, 0)),
            pl.BlockSpec(seg_tile, lambda c, t: (0, _sample_tile(c, t), 0, 0)),
            pl.BlockSpec((LRU_SEGS, CONV_WIDTH - 1, D_RNN), _PER_SAMPLE3),
            pl.BlockSpec((LRU_SEGS, D_RNN), lambda c, t: (_sample_tile(c, t), 0)),
            _const_spec((ROW_TILE, ROW_TILE)),
            _const_spec((D_MODEL, 2 * D_RNN)),
            _const_spec((CONV_WIDTH, D_RNN)),
            _const_spec((1, D_RNN)),
            _const_spec((D_RNN // MXU_EDGE, MXU_EDGE, MXU_EDGE)),
            _const_spec((D_RNN // MXU_EDGE, MXU_EDGE, MXU_EDGE)),
            _const_spec((1, D_RNN)),
            _const_spec((1, D_RNN)),
            _const_spec((1, D_RNN)),
            _const_spec((D_RNN, D_MODEL)),
            _const_spec((1, D_MODEL)),
            _const_spec((1, D_MODEL)),
        ],
        out_specs=[
            pl.BlockSpec((None, ROW_TILE, D_MODEL), lambda c, t: (c, t, 0)),
            pl.BlockSpec((None, CONV_WIDTH - 1, D_RNN), _PER_PROMPT),
            pl.BlockSpec((None, 1, D_RNN), _PER_PROMPT),
            pl.BlockSpec((LRU_SEGS, CONV_WIDTH - 1, D_RNN), _PER_SAMPLE3),
            pl.BlockSpec((LRU_SEGS, D_RNN), lambda c, t: (_sample_tile(c, t), 0)),
        ],
        out_shape=[
            _ACT_SHAPE,
            jax.ShapeDtypeStruct((BATCH, CONV_WIDTH - 1, D_RNN), jnp.float32),
            jax.ShapeDtypeStruct((BATCH, 1, D_RNN), jnp.float32),
            jax.ShapeDtypeStruct((DEC_BATCH, CONV_WIDTH - 1, D_RNN), jnp.float32),
            jax.ShapeDtypeStruct((DEC_BATCH, D_RNN), jnp.float32),
        ],
        scratch_shapes=[
            pltpu.VMEM((SUBLANES, D_RNN), jnp.float32),
            pltpu.VMEM((1, D_RNN), jnp.float32),
        ],
        compiler_params=pltpu.CompilerParams(
            dimension_semantics=("arbitrary", "arbitrary"), vmem_limit_bytes=VMEM_LIMIT),
        name="lru",
    )(xp.reshape(BATCH, SEQ // LRU_SEG, LRU_SEG, D_MODEL), xs.reshape(1, MOE_CHUNK // LRU_SEG, LRU_SEG, D_MODEL),
      conv_in, h0, perm, *w, ln_g, ln_b)


def _rope(x, cos, sin_signed):
    lane = lax.broadcasted_iota(jnp.int32, (x.shape[0], LANES), 1)
    first_half = (lane & (HEAD_DIM - 1)) < HEAD_DIM // 2
    cols = []
    for c in range(x.shape[1] // LANES):
        xc = x[:, c * LANES:(c + 1) * LANES]
        partner = jnp.where(first_half, pltpu.roll(xc, LANES - HEAD_DIM // 2, 1),
                            pltpu.roll(xc, HEAD_DIM // 2, 1))
        cols.append(xc * cos + partner * sin_signed)
    return cols[0] if len(cols) == 1 else jnp.concatenate(cols, axis=1)


def _dup_heads(kv):
    lane = lax.broadcasted_iota(jnp.int32, kv.shape, 1)
    low = lane < HEAD_DIM
    swapped = pltpu.roll(kv, HEAD_DIM, 1)
    return (jnp.where(low, kv, swapped).astype(jnp.bfloat16),
            jnp.where(low, swapped, kv).astype(jnp.bfloat16))


def _attend(chunks, sinks_ref):
    lane = lax.broadcasted_iota(jnp.int32, (CHUNK, LANES), 1)
    low = lane < HEAD_DIM
    zero = jnp.zeros((CHUNK, LANES), jnp.bfloat16)
    cols_per_group = KV_GROUP * HEAD_DIM // LANES
    units = [(ci, g) for ci in range(len(chunks)) for g in range(N_KV_HEADS)]

    scores = {}
    for ci, g in units:
        q, k_wins, _, valid = chunks[ci]
        stacked = []
        for c in range(g * cols_per_group, (g + 1) * cols_per_group):
            qc = q[:, c * LANES:(c + 1) * LANES]
            stacked.append(jnp.where(low, qc, zero))
            stacked.append(jnp.where(low, zero, qc))
        qst = jnp.concatenate(stacked, axis=0)
        s = lax.dot_general(qst, k_wins[g], (((1,), (1,)), ((), ())),
                            preferred_element_type=jnp.float32)
        scores[ci, g] = s if valid is None else jnp.where(valid, s, NEG_BIG)

    probs = {}
    for ci, g in units:
        s = scores[ci, g]
        heads = []
        for hh in range(KV_GROUP):
            sh = s[hh * CHUNK:(hh + 1) * CHUNK]
            sink = sinks_ref[g * KV_GROUP + hh]
            m = jnp.maximum(jnp.max(sh, -1, keepdims=True), sink)
            p = jnp.exp(sh - m)
            denom = jnp.sum(p, -1, keepdims=True) + jnp.exp(sink - m)
            heads.append((p * (1.0 / denom)).astype(jnp.bfloat16))
        probs[ci, g] = jnp.concatenate(heads, axis=0)

    out_cols = {ci: [] for ci in range(len(chunks))}
    for ci, g in units:
        o = jnp.dot(probs[ci, g], chunks[ci][2][g], preferred_element_type=jnp.float32)
        for j in range(cols_per_group):
            oa = o[(2 * j) * CHUNK:(2 * j + 1) * CHUNK]
            ob = o[(2 * j + 1) * CHUNK:(2 * j + 2) * CHUNK]
            out_cols[ci].append(jnp.where(low, oa, ob))
    return [jnp.concatenate(out_cols[ci], axis=1) for ci in range(len(chunks))]


def _project_qkv(x, wqkv_ref, bqkv_ref, cos, sin):
    nq = N_HEADS * HEAD_DIM
    nk = N_KV_HEADS * HEAD_DIM
    xb = x.astype(jnp.bfloat16)
    kv = jnp.dot(xb, wqkv_ref[:, nq:], preferred_element_type=jnp.float32) + bqkv_ref[:, nq:]
    k = _rope(kv[:, :nk], cos, sin)
    v = kv[:, nk:]
    q = jnp.dot(xb, wqkv_ref[:, :nq], preferred_element_type=jnp.float32) + bqkv_ref[:, :nq]
    q = _rope(q, cos, sin)
    qs = (q * (HEAD_DIM ** -0.5)).astype(jnp.bfloat16)
    return qs, k, v


SUB_TILE = 256


def _sub_tiles():
    return [pl.ds(i * SUB_TILE, SUB_TILE) for i in range(ROW_TILE // SUB_TILE)]


def _project_out(xs, subs, o, wo_ref, g_ref, b_ref, x1_ref):
    for x, sub, i in zip(xs, subs, range(len(subs))):
        out = jnp.dot(o[i * SUB_TILE:(i + 1) * SUB_TILE].astype(jnp.bfloat16), wo_ref[...],
                      preferred_element_type=jnp.float32)
        x1_ref[sub, :] = _layer_norm(DEEPNORM_ALPHA * x + out, g_ref[...], b_ref[...])


def _swa_kernel(xp_ref, xs_ref, ck_ref, cv_ref, wqkv_ref, bqkv_ref, cosp_ref, sinp_ref, coss_ref, sins_ref,
                sinks_ref, wo_ref, g_ref, b_ref, x1_ref, kp_ref, vp_ref, ks_ref, vs_ref,
                k0_scr, k1_scr, v0_scr, v1_scr):
    c = pl.program_id(0)
    t = pl.program_id(1)
    scrs = (k0_scr, k1_scr, v0_scr, v1_scr)

    @pl.when(c < BATCH)
    def _():
        @pl.when(t == 0)
        def _():
            for scr in scrs:
                scr[0:WINDOW, :] = jnp.zeros((WINDOW, LANES), jnp.bfloat16)

        subs = _sub_tiles()
        xs_ = [xp_ref[sub, :] for sub in subs]
        proj = [_project_qkv(x, wqkv_ref, bqkv_ref, cosp_ref[sub, :], sinp_ref[sub, :]) for x, sub in zip(xs_, subs)]
        qs = jnp.concatenate([p[0] for p in proj], axis=0)
        k = jnp.concatenate([p[1] for p in proj], axis=0)
        v = jnp.concatenate([p[2] for p in proj], axis=0)
        kp_ref[...] = k[ROW_TILE - WINDOW:]
        vp_ref[...] = v[ROW_TILE - WINDOW:]
        k0, k1 = _dup_heads(k)
        v0, v1 = _dup_heads(v)
        for scr, val in zip(scrs, (k0, k1, v0, v1)):
            scr[WINDOW:WINDOW + ROW_TILE, :] = val

        span = WINDOW + CHUNK
        key_pos = lax.broadcasted_iota(jnp.int32, (1, span), 1)
        chunks = []
        for ci in range(ROW_TILE // CHUNK):
            lo = ci * CHUNK
            valid = (t * ROW_TILE + lo - WINDOW + key_pos) >= 0
            k_wins = (k0_scr[lo:lo + span, :], k1_scr[lo:lo + span, :])
            v_wins = (v0_scr[lo:lo + span, :], v1_scr[lo:lo + span, :])
            chunks.append((qs[lo:lo + CHUNK], k_wins, v_wins, valid))
        o = jnp.concatenate(_attend(chunks, sinks_ref), axis=0)
        for scr in scrs:
            scr[0:WINDOW, :] = scr[ROW_TILE:ROW_TILE + WINDOW, :]
        _project_out(xs_, subs, o, wo_ref, g_ref, b_ref, x1_ref)

    @pl.when(c == SAMPLE_CHUNK)
    def _():
        subs = _sub_tiles()
        xs_ = [xs_ref[sub, :] for sub in subs]
        proj = [_project_qkv(x, wqkv_ref, bqkv_ref, coss_ref[sub, :], sins_ref[sub, :]) for x, sub in zip(xs_, subs)]
        qs = jnp.concatenate([p[0] for p in proj], axis=0)
        k = jnp.concatenate([p[1] for p in proj], axis=0)
        v = jnp.concatenate([p[2] for p in proj], axis=0)
        knew = _dup_heads(k)
        vnew = _dup_heads(v)
        chunks = []
        for s in range(SAMPLE_SEGS):
            lo = s * DEC_SEQ
            ck = ck_ref[s]
            cv = cv_ref[s]
            kold = _dup_heads(ck)
            vold = _dup_heads(cv)
            k_wins = [jnp.concatenate([kold[g], knew[g][lo:lo + DEC_SEQ]], axis=0) for g in range(N_KV_HEADS)]
            v_wins = [jnp.concatenate([vold[g], vnew[g][lo:lo + DEC_SEQ]], axis=0) for g in range(N_KV_HEADS)]
            chunks.append((qs[lo:lo + DEC_SEQ], k_wins, v_wins, None))
            ks_ref[s, 0:WINDOW - DEC_SEQ, :] = ck[DEC_SEQ:]
            ks_ref[s, WINDOW - DEC_SEQ:WINDOW, :] = k[lo:lo + DEC_SEQ]
            vs_ref[s, 0:WINDOW - DEC_SEQ, :] = cv[DEC_SEQ:]
            vs_ref[s, WINDOW - DEC_SEQ:WINDOW, :] = v[lo:lo + DEC_SEQ]
        o = jnp.concatenate(_attend(chunks, sinks_ref), axis=0)
        _project_out(xs_, subs, o, wo_ref, g_ref, b_ref, x1_ref)


def _swa(xp, xs, ck, cv, w, cos_p, sin_p, cos_s, sin_s, ln_g, ln_b):
    wqkv, bqkv, sinks, wo = w
    nt_s = DEC_BATCH // SAMPLE_SEGS
    kv_lanes = N_KV_HEADS * HEAD_DIM
    cache_spec = pl.BlockSpec((None, SAMPLE_SEGS, WINDOW, kv_lanes), _PER_SAMPLE4)
    prompt_table = pl.BlockSpec((ROW_TILE, LANES), lambda c, t: (_prompt_tile(c, t), 0))
    kv_prompt = pl.BlockSpec((None, WINDOW, kv_lanes), _PER_PROMPT)
    return pl.pallas_call(
        _swa_kernel,
        grid=_MIXER_GRID,
        in_specs=_mixer_act_specs() + [
            cache_spec, cache_spec,
            _const_spec((D_MODEL, QKV_DIM)),
            _const_spec((1, QKV_DIM)),
            prompt_table, prompt_table,
            _const_spec((ROW_TILE, LANES)), _const_spec((ROW_TILE, LANES)),
            pl.BlockSpec(memory_space=pltpu.SMEM),
            _const_spec((N_HEADS * HEAD_DIM, D_MODEL)),
            _const_spec((1, D_MODEL)),
            _const_spec((1, D_MODEL)),
        ],
        out_specs=[
            pl.BlockSpec((None, ROW_TILE, D_MODEL), lambda c, t: (c, t, 0)),
            kv_prompt, kv_prompt, cache_spec, cache_spec,
        ],
        out_shape=[
            _ACT_SHAPE,
            jax.ShapeDtypeStruct((BATCH, WINDOW, kv_lanes), jnp.float32),
            jax.ShapeDtypeStruct((BATCH, WINDOW, kv_lanes), jnp.float32),
            jax.ShapeDtypeStruct((nt_s, SAMPLE_SEGS, WINDOW, kv_lanes), jnp.float32),
            jax.ShapeDtypeStruct((nt_s, SAMPLE_SEGS, WINDOW, kv_lanes), jnp.float32),
        ],
        scratch_shapes=[pltpu.VMEM((WINDOW + ROW_TILE, LANES), jnp.bfloat16) for _ in range(4)],
        compiler_params=pltpu.CompilerParams(
            dimension_semantics=("arbitrary", "arbitrary"), vmem_limit_bytes=VMEM_LIMIT),
        name="swa",
    )(xp, xs, ck, cv, wqkv, bqkv, cos_p, sin_p, cos_s, sin_s, sinks, wo, ln_g, ln_b)


(META_TILE_EXPERT, META_N_ACTIVE, META_FILL_LO, META_FILL_HI, META_IS_NEW, META_TILE_RUN, META_RUN_EXPERT,
 META_N_RUNS) = range(SUBLANES)
WEIGHT_SLOTS = 3


def _router_kernel(x_ref, wr_ref, br_ref, pos_ref, gate_ref, meta_ref):
    f32 = jnp.float32
    x = x_ref[...]
    w = wr_ref[...]
    xh = x.astype(jnp.bfloat16)
    xl = (x - xh.astype(f32)).astype(jnp.bfloat16)
    wh = w.astype(jnp.bfloat16)
    wl = (w - wh.astype(f32)).astype(jnp.bfloat16)
    nt = (((1,), (1,)), ((), ()))
    logits = (lax.dot_general(wh, xh, nt, preferred_element_type=f32)
              + lax.dot_general(wl, xh, nt, preferred_element_type=f32)
              + lax.dot_general(wh, xl, nt, preferred_element_type=f32))
    m = jnp.max(logits, axis=0, keepdims=True)
    e = jnp.exp(logits - m)
    probs = e / jnp.sum(e, axis=0, keepdims=True)
    sel = probs + br_ref[...]
    p = [probs[i:i + 1] for i in range(N_EXPERTS)]
    s = [sel[i:i + 1] for i in range(N_EXPERTS)]

    def first_argmax(vals):
        best, arg = vals[0], jnp.zeros_like(vals[0], dtype=jnp.int32)
        for i in range(1, len(vals)):
            better = vals[i] > best
            best = jnp.where(better, vals[i], best)
            arg = jnp.where(better, i, arg)
        return arg

    group_scores = []
    for gidx in range(N_GROUPS):
        v = s[gidx * EXPERTS_PER_GROUP:(gidx + 1) * EXPERTS_PER_GROUP]
        best = v[0] + v[1]
        for i in range(EXPERTS_PER_GROUP):
            for j in range(i + 1, EXPERTS_PER_GROUP):
                if (i, j) != (0, 1):
                    best = jnp.maximum(best, v[i] + v[j])
        group_scores.append(best)
    best_group = first_argmax(group_scores)
    neg_inf = jnp.full_like(s[0], -jnp.inf)
    masked = [jnp.where(best_group == (i // EXPERTS_PER_GROUP), s[i], neg_inf) for i in range(N_EXPERTS)]
    idx1 = first_argmax(masked)
    masked2 = [jnp.where(idx1 == i, neg_inf, masked[i]) for i in range(N_EXPERTS)]
    idx2 = first_argmax(masked2)
    zero = jnp.zeros_like(p[0])
    w1, w2 = zero, zero
    for i in range(N_EXPERTS):
        w1 = w1 + jnp.where(idx1 == i, p[i], zero)
        w2 = w2 + jnp.where(idx2 == i, p[i], zero)
    tot = w1 + w2
    gate_ref[0:1, :] = w1 / tot
    gate_ref[1:2, :] = w2 / tot

    expert = lax.broadcasted_iota(jnp.int32, (N_EXPERTS, MOE_CHUNK), 0)
    strict_upper = (lax.broadcasted_iota(jnp.int32, (LANES, LANES), 0)
                    < lax.broadcasted_iota(jnp.int32, (LANES, LANES), 1)).astype(jnp.bfloat16)
    carry = jnp.zeros((N_EXPERTS, 1), f32)
    onehots, ranks = [], []
    for idx in (idx1, idx2):
        onehot = (expert == idx).astype(f32)
        before = []
        for blk in range(MOE_CHUNK // LANES):
            oh = onehot[:, blk * LANES:(blk + 1) * LANES]
            before.append(jnp.dot(oh.astype(jnp.bfloat16), strict_upper, preferred_element_type=f32) + carry)
            carry = carry + jnp.sum(oh, axis=1, keepdims=True)
        onehots.append(onehot)
        ranks.append(jnp.sum(onehot * jnp.concatenate(before, axis=1), axis=0, keepdims=True))
    counts = carry
    padded = jnp.floor((counts + (MOE_TM - 1)) * (1.0 / MOE_TM)) * MOE_TM
    strict_lower = (lax.broadcasted_iota(jnp.int32, (N_EXPERTS, N_EXPERTS), 0)
                    > lax.broadcasted_iota(jnp.int32, (N_EXPERTS, N_EXPERTS), 1)).astype(jnp.bfloat16)
    starts = jnp.dot(strict_lower, jnp.broadcast_to(padded, (N_EXPERTS, LANES)).astype(jnp.bfloat16),
                     preferred_element_type=f32)[:, 0:1]
    ends = starts + padded
    for k in range(2):
        pos = jnp.sum(onehots[k] * starts, axis=0, keepdims=True) + ranks[k]
        pos_ref[k:k + 1, :] = pos.astype(jnp.int32)

    lane = lax.broadcasted_iota(jnp.int32, (N_EXPERTS, LANES), 1)
    sub = lax.broadcasted_iota(jnp.int32, (N_EXPERTS, LANES), 0)
    tile_start = (lane * MOE_TM).astype(f32)
    tile_expert = jnp.sum((ends <= tile_start).astype(f32), axis=0, keepdims=True)
    tile_expert = jnp.minimum(tile_expert, N_EXPERTS - 1.0)
    on_diag = sub == lane

    def to_lanes(col):
        return jnp.sum(jnp.where(on_diag, col, 0.0), axis=0, keepdims=True)

    n_active = jnp.broadcast_to(ends[N_EXPERTS - 1:N_EXPERTS] * (1.0 / MOE_TM), (1, LANES))
    lane1 = lax.broadcasted_iota(jnp.int32, (1, LANES), 1)
    active = lane1.astype(f32) < n_active
    starts_run = jnp.logical_or(lane1 == 0, tile_expert != pltpu.roll(tile_expert, 1, 1))
    is_new = jnp.where(jnp.logical_and(active, starts_run), 1.0, 0.0)
    upper_incl = (lax.broadcasted_iota(jnp.int32, (LANES, LANES), 0)
                  <= lax.broadcasted_iota(jnp.int32, (LANES, LANES), 1)).astype(jnp.bfloat16)
    tile_run = jnp.dot(jnp.broadcast_to(is_new, (SUBLANES, LANES)).astype(jnp.bfloat16), upper_incl,
                       preferred_element_type=f32)[0:1] - 1.0
    first_of_run = jnp.logical_and(is_new > 0.0, tile_run == sub.astype(f32))
    run_expert = jnp.sum(jnp.where(first_of_run, tile_expert, 0.0), axis=1, keepdims=True)
    n_runs = jnp.broadcast_to(jnp.sum(is_new, axis=1, keepdims=True), (1, LANES))

    meta_ref[...] = jnp.zeros(meta_ref.shape, jnp.int32)
    meta_ref[META_IS_NEW:META_IS_NEW + 1, :] = is_new.astype(jnp.int32)
    meta_ref[META_TILE_RUN:META_TILE_RUN + 1, :] = tile_run.astype(jnp.int32)
    meta_ref[META_RUN_EXPERT:META_RUN_EXPERT + 1, :] = to_lanes(run_expert).astype(jnp.int32)
    meta_ref[META_N_RUNS:META_N_RUNS + 1, :] = n_runs.astype(jnp.int32)
    meta_ref[META_TILE_EXPERT:META_TILE_EXPERT + 1, :] = tile_expert.astype(jnp.int32)
    meta_ref[META_N_ACTIVE:META_N_ACTIVE + 1, :] = n_active.astype(jnp.int32)
    meta_ref[META_FILL_LO:META_FILL_LO + 1, :] = to_lanes(starts + counts).astype(jnp.int32)
    meta_ref[META_FILL_HI:META_FILL_HI + 1, :] = to_lanes(ends).astype(jnp.int32)


def _router(x, wr_t, br):
    per_chunk = lambda c: (c, 0, 0)
    return pl.pallas_call(
        _router_kernel,
        grid=(N_CHUNKS,),
        in_specs=[
            pl.BlockSpec((None, MOE_CHUNK, D_MODEL), per_chunk),
            _const_spec((N_EXPERTS, D_MODEL)),
            _const_spec((N_EXPERTS, 1)),
        ],
        out_specs=[
            pl.BlockSpec((None, 2, MOE_CHUNK), per_chunk),
            pl.BlockSpec((None, 2, MOE_CHUNK), per_chunk),
            pl.BlockSpec((None, SUBLANES, LANES), per_chunk),
        ],
        out_shape=[
            jax.ShapeDtypeStruct((N_CHUNKS, 2, MOE_CHUNK), jnp.int32),
            jax.ShapeDtypeStruct((N_CHUNKS, 2, MOE_CHUNK), jnp.float32),
            jax.ShapeDtypeStruct((N_CHUNKS, SUBLANES, LANES), jnp.int32),
        ],
        compiler_params=pltpu.CompilerParams(
            dimension_semantics=("parallel",), vmem_limit_bytes=VMEM_LIMIT),
        name="router",
    )(x, wr_t, br)


def _gather_rows(pair_scr, x_ref, xs_scr, row0, j):
    tok = pair_scr[row0 + j] & (MOE_CHUNK - 1)
    xs_scr[j // SUBLANES, (j % SUBLANES):(j % SUBLANES) + 1, :] = x_ref[pl.ds(tok, 1), :]


def _scatter_rows(pair_scr, y_scr, yo_scr, row0, j):
    y_scr[pl.ds(pair_scr[row0 + j], 1), :] = yo_scr[j // SUBLANES, (j % SUBLANES):(j % SUBLANES) + 1, :]


def _moe_kernel(layer, meta_ref, pos_ref, x_ref, gate_ref,
                wg_hbm, wu_hbm, wd_hbm, g_ref, b_ref, outp_hbm, outs_hbm,
                pair_scr, xs_a, xs_b, yo_a, yo_b, y_scr, wg_buf, wu_buf, wd_buf, w_sem, out_sem):
    c = pl.program_id(0)

    def plan(row, i):
        return meta_ref[(c * SUBLANES + row) * LANES + i]

    n_act = plan(META_N_ACTIVE, 0)

    n_runs = plan(META_N_RUNS, 0)

    def weight_copies(run):
        e, s = plan(META_RUN_EXPERT, run), lax.rem(run, WEIGHT_SLOTS)
        return (pltpu.make_async_copy(wg_hbm.at[layer, e], wg_buf.at[s], w_sem.at[s, 0]),
                pltpu.make_async_copy(wu_hbm.at[layer, e], wu_buf.at[s], w_sem.at[s, 1]),
                pltpu.make_async_copy(wd_hbm.at[layer, e], wd_buf.at[s], w_sem.at[s, 2]))

    def prompt_out_copy(chunk):
        return pltpu.make_async_copy(y_scr.at[pl.ds(0, MOE_CHUNK)], outp_hbm.at[chunk], out_sem.at[0])

    def sample_out_copy():
        return pltpu.make_async_copy(y_scr.at[pl.ds(0, MOE_CHUNK)], outs_hbm.at[0], out_sem.at[0])

    for ahead in range(WEIGHT_SLOTS - 1):
        @pl.when(ahead < n_runs)
        def _():
            for cp in weight_copies(ahead):
                cp.start()

    def fill8(g, base):
        for u in range(SUBLANES):
            pair_scr[base + g * SUBLANES + u] = JUNK_PAIR
        return base

    def fill_expert(e, carry):
        lo, hi = plan(META_FILL_LO, e), plan(META_FILL_HI, e)
        lax.fori_loop(0, (hi - lo + SUBLANES - 1) >> 3, fill8, lo)
        return carry
    lax.fori_loop(0, N_EXPERTS, fill_expert, 0)
    lax.fori_loop(0, MOE_TM // SUBLANES, fill8, plan(META_FILL_HI, N_EXPERTS - 1))

    def invert(t, carry):
        pair_scr[pos_ref[t]] = t
        pair_scr[pos_ref[MOE_CHUNK + t]] = MOE_CHUNK + t
        return carry
    lax.fori_loop(0, MOE_CHUNK, invert, 0, unroll=8)

    def gather0(gi, carry):
        for u in range(SUBLANES):
            tok = pair_scr[gi * SUBLANES + u] & (MOE_CHUNK - 1)
            xs_a[gi, u:u + 1, :] = x_ref[pl.ds(tok, 1), :]
        return carry
    lax.fori_loop(0, MOE_TM // SUBLANES, gather0, 0)

    @pl.when(c == 0)
    def _():
        yo_b[...] = jnp.zeros_like(yo_b)

    @pl.when(c > 0)
    def _():
        prompt_out_copy(c - 1).wait()

    def expert_step(t, s, xs_cur, yo_cur, xs_nxt, yo_prv):
        row_nxt = (t + 1) * MOE_TM
        row_prv = jnp.maximum(t - 1, 0) * MOE_TM
        for j in range(MOE_TM):
            _scatter_rows(pair_scr, y_scr, yo_prv, row_prv, j)
            _gather_rows(pair_scr, x_ref, xs_nxt, row_nxt, j)
        xb = xs_cur[...].reshape(MOE_TM, D_MODEL).astype(jnp.bfloat16)
        hg = jnp.dot(xb, wg_buf[s], preferred_element_type=jnp.float32)
        hu = jnp.dot(xb, wu_buf[s], preferred_element_type=jnp.float32)
        act = (jax.nn.silu(hg) * hu).astype(jnp.bfloat16)
        yo = jnp.dot(act, wd_buf[s], preferred_element_type=jnp.float32)
        yo_cur[...] = yo.reshape(MOE_TM // SUBLANES, SUBLANES, D_MODEL)

    def tile_iteration(t, carry):
        run = plan(META_TILE_RUN, t)
        starts_run = plan(META_IS_NEW, t) == 1

        @pl.when(starts_run)
        def _():
            for cp in weight_copies(run):
                cp.wait()

        @pl.when(jnp.logical_and(starts_run, run + WEIGHT_SLOTS - 1 < n_runs))
        def _():
            for cp in weight_copies(run + WEIGHT_SLOTS - 1):
                cp.start()

        slot = lax.rem(run, WEIGHT_SLOTS)
        for s in range(WEIGHT_SLOTS):
            in_slot = slot == s

            @pl.when(jnp.logical_and((t & 1) == 0, in_slot))
            def _():
                expert_step(t, s, xs_a, yo_a, xs_b, yo_b)

            @pl.when(jnp.logical_and((t & 1) == 1, in_slot))
            def _():
                expert_step(t, s, xs_b, yo_b, xs_a, yo_a)
        return carry
    lax.fori_loop(0, n_act, tile_iteration, 0)

    def drain(yo_last):
        row_last = (n_act - 1) * MOE_TM

        def scatter(gi, carry):
            for u in range(SUBLANES):
                y_scr[pl.ds(pair_scr[row_last + gi * SUBLANES + u], 1), :] = yo_last[gi, u:u + 1, :]
            return carry
        lax.fori_loop(0, MOE_TM // SUBLANES, scatter, 0)

    @pl.when((n_act & 1) == 1)
    def _():
        drain(yo_a)

    @pl.when((n_act & 1) == 0)
    def _():
        drain(yo_b)

    def combine(k, carry):
        lo = pl.multiple_of(k * ROW_TILE, ROW_TILE)
        gates = gate_ref[pl.ds(lo, ROW_TILE), :]
        moe = (gates[:, 0:1] * y_scr[pl.ds(lo, ROW_TILE), :]
               + gates[:, 1:2] * y_scr[pl.ds(MOE_CHUNK + lo, ROW_TILE), :])
        x = x_ref[pl.ds(lo, ROW_TILE), :]
        y_scr[pl.ds(lo, ROW_TILE), :] = _layer_norm(DEEPNORM_ALPHA * x + moe, g_ref[...], b_ref[...])
        return carry
    lax.fori_loop(0, COMBINE_STEPS, combine, 0)

    @pl.when(c < SAMPLE_CHUNK)
    def _():
        prompt_out_copy(c).start()

    @pl.when(c == SAMPLE_CHUNK)
    def _():
        sample_out_copy().start()
        sample_out_copy().wait()


def _moe(x, meta, pos, gates_t, layer, wg, wu, wd, ln_g, ln_b):
    stage = pltpu.VMEM((MOE_TM // SUBLANES, SUBLANES, D_MODEL), jnp.float32)
    per_chunk = lambda c, *_: (c, 0, 0)
    grid_spec = pltpu.PrefetchScalarGridSpec(
        num_scalar_prefetch=1,
        grid=(N_CHUNKS,),
        in_specs=[
            pl.BlockSpec((None, None, 2 * MOE_CHUNK), per_chunk, memory_space=pltpu.SMEM),
            pl.BlockSpec((None, MOE_CHUNK, D_MODEL), per_chunk),
            pl.BlockSpec((None, MOE_CHUNK, 2), per_chunk),
            pl.BlockSpec(memory_space=pl.ANY),
            pl.BlockSpec(memory_space=pl.ANY),
            pl.BlockSpec(memory_space=pl.ANY),
            pl.BlockSpec((1, D_MODEL), lambda c, *_: (0, 0)),
            pl.BlockSpec((1, D_MODEL), lambda c, *_: (0, 0)),
        ],
        out_specs=[pl.BlockSpec(memory_space=pl.ANY), pl.BlockSpec(memory_space=pl.ANY)],
        scratch_shapes=[
            pltpu.SMEM((MOE_ROWS + MOE_TM + SUBLANES,), jnp.int32),
            stage, stage, stage, stage,
            pltpu.VMEM((2 * MOE_CHUNK + SUBLANES, D_MODEL), jnp.float32),
            pltpu.VMEM((WEIGHT_SLOTS, D_MODEL, D_EXPERT), jnp.bfloat16),
            pltpu.VMEM((WEIGHT_SLOTS, D_MODEL, D_EXPERT), jnp.bfloat16),
            pltpu.VMEM((WEIGHT_SLOTS, D_EXPERT, D_MODEL), jnp.bfloat16),
            pltpu.SemaphoreType.DMA((WEIGHT_SLOTS, 3)),
            pltpu.SemaphoreType.DMA((1,)),
        ],
    )
    return pl.pallas_call(
        functools.partial(_moe_kernel, layer),
        grid_spec=grid_spec,
        out_shape=[jax.ShapeDtypeStruct((BATCH, MOE_CHUNK, D_MODEL), jnp.float32),
                   jax.ShapeDtypeStruct((1, MOE_CHUNK, D_MODEL), jnp.float32)],
        compiler_params=pltpu.CompilerParams(
            dimension_semantics=("arbitrary",), vmem_limit_bytes=VMEM_LIMIT),
        name="moe",
    )(meta.reshape(-1), pos.reshape(N_CHUNKS, 1, 2 * MOE_CHUNK), x, gates_t, wg, wu, wd, ln_g, ln_b)


def _moe_layer(x, wr_t, br, layer, wg, wu, wd, ln_g, ln_b):
    pos, gates, meta = _router(x, wr_t, br)
    return _moe(x, meta, pos, jnp.swapaxes(gates, 1, 2), layer, wg, wu, wd, ln_g, ln_b)


def _rope_tables(pos):
    half = HEAD_DIM // 2
    inv_freq = ROPE_THETA ** (-jnp.arange(half, dtype=jnp.float32) / half)
    ang = pos.astype(jnp.float32)[:, None] * inv_freq[None, :]
    cos = jnp.cos(ang)
    sin = jnp.sin(ang)
    reps = LANES // HEAD_DIM
    cos_t = jnp.tile(jnp.concatenate([cos, cos], axis=1), (1, reps))
    sin_t = jnp.tile(jnp.concatenate([-sin, sin], axis=1), (1, reps))
    return cos_t, sin_t


def _block_diag_tiles(w):
    per = MXU_EDGE // LRU_BLOCK_W
    w4 = w.reshape(LRU_BLOCKS // per, per, LRU_BLOCK_W, LRU_BLOCK_W)
    eye = jnp.eye(per, dtype=w.dtype)
    return jnp.einsum("qaij,ab->qaibj", w4, eye).reshape(LRU_BLOCKS // per, MXU_EDGE, MXU_EDGE)


def kernel(x_prompt, x_sample, state_conv_0, state_h_0, cache_k_1, cache_v_1, state_conv_2, state_h_2,
           cache_k_3, cache_v_3, w_router, b_router, ln_g, ln_b, lru_w_in, lru_conv_w, lru_conv_b,
           lru_w_gates, lru_b_gates, lru_lambda, lru_w_out, attn_w_qkv, attn_b_qkv, attn_sinks, attn_w_o,
           moe_w_gate, moe_w_up, moe_w_down):
    bf = jnp.bfloat16
    in_state = {0: (state_conv_0, state_h_0), 1: (cache_k_1, cache_v_1),
                2: (state_conv_2, state_h_2), 3: (cache_k_3, cache_v_3)}
    nt_s = DEC_BATCH // SAMPLE_SEGS
    kv_lanes = N_KV_HEADS * HEAD_DIM

    wr_t = w_router.T
    br = b_router.reshape(N_EXPERTS, 1)
    cos_p, sin_p = _rope_tables(jnp.arange(SEQ))
    cos_s, sin_s = _rope_tables(PAST_LEN + jnp.arange(DEC_SEQ))
    cos_s = jnp.tile(cos_s, (SAMPLE_SEGS, 1))
    sin_s = jnp.tile(sin_s, (SAMPLE_SEGS, 1))
    wg_all = moe_w_gate.astype(bf)
    wu_all = moe_w_up.astype(bf)
    wd_all = moe_w_down.astype(bf)

    x = x_prompt
    x_s = x_sample.reshape(1, MOE_CHUNK, D_MODEL)
    new_p, new_s = {}, {}
    for i in range(DEPTH):
        slot = i // 2
        g0 = ln_g[i, 0].reshape(1, D_MODEL)
        b0 = ln_b[i, 0].reshape(1, D_MODEL)
        g1 = ln_g[i, 1].reshape(1, D_MODEL)
        b1 = ln_b[i, 1].reshape(1, D_MODEL)
        st_a, st_b = in_state[i]
        if i % 2 == 0:
            gates = lru_w_gates[slot]
            w = (lru_w_in[slot].astype(bf), lru_conv_w[slot], lru_conv_b[slot].reshape(1, D_RNN),
                 _block_diag_tiles(gates[:, :, :LRU_BLOCK_W]).astype(bf),
                 _block_diag_tiles(gates[:, :, LRU_BLOCK_W:]).astype(bf),
                 lru_b_gates[slot][:, :LRU_BLOCK_W].reshape(1, D_RNN),
                 lru_b_gates[slot][:, LRU_BLOCK_W:].reshape(1, D_RNN),
                 lru_lambda[slot].reshape(1, D_RNN), lru_w_out[slot].astype(bf))
            x1, conv_p, h_p, conv_s, h_s = _lru(x, x_s, st_a, st_b, w, g0, b0)
            new_p[i] = (conv_p, h_p.reshape(BATCH, D_RNN))
            new_s[i] = (conv_s, h_s)
        else:
            w = (attn_w_qkv[slot].astype(bf), attn_b_qkv[slot].reshape(1, QKV_DIM), attn_sinks[slot],
                 attn_w_o[slot].astype(bf))
            x1, k_p, v_p, k_s, v_s = _swa(
                x, x_s, st_a.reshape(nt_s, SAMPLE_SEGS, WINDOW, kv_lanes),
                st_b.reshape(nt_s, SAMPLE_SEGS, WINDOW, kv_lanes), w, cos_p, sin_p, cos_s, sin_s, g0, b0)
            new_p[i] = (k_p.reshape(BATCH, WINDOW, N_KV_HEADS, HEAD_DIM),
                        v_p.reshape(BATCH, WINDOW, N_KV_HEADS, HEAD_DIM))
            new_s[i] = (k_s.reshape(DEC_BATCH, WINDOW, N_KV_HEADS, HEAD_DIM),
                        v_s.reshape(DEC_BATCH, WINDOW, N_KV_HEADS, HEAD_DIM))
        x, x_s = _moe_layer(x1, wr_t, br, i, wg_all, wu_all, wd_all, g1, b1)
    return (x, x_s.reshape(DEC_BATCH, DEC_SEQ, D_MODEL),
            new_p[0][0], new_s[0][0], new_p[0][1], new_s[0][1],
            new_p[1][0], new_s[1][0], new_p[1][1], new_s[1][1],
            new_p[2][0], new_s[2][0], new_p[2][1], new_s[2][1],
            new_p[3][0], new_s[3][0], new_p[3][1], new_s[3][1])
```

```python
import functools

import jax
import jax.numpy as jnp
from jax import lax
from jax.experimental import pallas as pl
from jax.experimental.pallas import tpu as pltpu

D_MODEL = 1024
BATCH = 8
SEQ = 2048
DEPTH = 4
DEC_BATCH = 32
DEC_SEQ = 64
PAST_LEN = 4096
CHUNK = 64
D_RNN = D_MODEL
CONV_WIDTH = 4
LRU_BLOCKS = 16
LRU_BLOCK_W = D_RNN // LRU_BLOCKS
LRU_C = 8.0
N_HEADS = 16
N_KV_HEADS = 2
HEAD_DIM = 64
KV_GROUP = N_HEADS // N_KV_HEADS
WINDOW = 128
ROPE_THETA = 10000.0
QKV_DIM = (N_HEADS + 2 * N_KV_HEADS) * HEAD_DIM
N_EXPERTS = 16
N_GROUPS = 4
EXPERTS_PER_GROUP = N_EXPERTS // N_GROUPS
D_EXPERT = 512
DEEPNORM_ALPHA = (2.0 * DEPTH) ** 0.25
LN_EPS = 1e-5

MXU_EDGE = 256
LANES = 128
SUBLANES = 8

ROW_TILE = 512
MOE_CHUNK = SEQ
N_CHUNKS = BATCH + DEC_BATCH * DEC_SEQ // MOE_CHUNK
SAMPLE_CHUNK = BATCH
MOE_TM = 256
MOE_ROWS = 2 * MOE_CHUNK + N_EXPERTS * MOE_TM
MOE_TILES = MOE_ROWS // MOE_TM
COMBINE_STEPS = MOE_CHUNK // ROW_TILE
PAIR_ROWS = MOE_ROWS + MOE_TM
JUNK_PAIR = 2 * MOE_CHUNK
SAMPLE_SEGS = ROW_TILE // DEC_SEQ
NEG_BIG = -1e30
VMEM_LIMIT = 52 * 1024 * 1024

assert DEC_BATCH * DEC_SEQ == MOE_CHUNK and MOE_TILES <= LANES


def _layer_norm(x, g, b):
    mu = jnp.mean(x, -1, keepdims=True)
    xc = x - mu
    var = jnp.mean(xc * xc, -1, keepdims=True)
    return xc * lax.rsqrt(var + LN_EPS) * g + b


def _const_spec(shape):
    nd = len(shape)
    return pl.BlockSpec(shape, lambda *_: (0,) * nd)


_ACT_SHAPE = jax.ShapeDtypeStruct((N_CHUNKS, MOE_CHUNK, D_MODEL), jnp.float32)


LRU_SEGS = SUBLANES
LRU_SEG = ROW_TILE // LRU_SEGS
assert LRU_SEG == DEC_SEQ


def _lru_tile(x_tm, x_sm, pre, h_init, chain, perm_ref, win_ref, cw_ref, cb_ref, wr_ref, wi_ref, br_ref, bi_ref,
              lam_ref, wout_ref, g_ref, b_ref):
    rows = ROW_TILE
    xb = x_tm.astype(jnp.bfloat16)
    rec = jnp.dot(xb, win_ref[:, D_RNN:], preferred_element_type=jnp.float32)
    if chain:
        sub = lax.broadcasted_iota(jnp.int32, (SUBLANES, D_RNN), 0)
        pre = [jnp.where(sub == 0, pre[j], pltpu.roll(rec[rows - (CONV_WIDTH - 1 - j) * SUBLANES:
                                                          rows - (CONV_WIDTH - 2 - j) * SUBLANES], 1, 0))
               for j in range(CONV_WIDTH - 1)]
    ext = jnp.concatenate(pre + [rec], axis=0)
    cw = cw_ref[...]
    xc = cb_ref[...] + cw[CONV_WIDTH - 1:CONV_WIDTH] * rec
    for j in range(CONV_WIDTH - 1):
        xc = xc + cw[j:j + 1] * ext[j * SUBLANES:j * SUBLANES + rows]

    xcb = xc.astype(jnp.bfloat16)
    r_parts, i_parts = [], []
    for q in range(D_RNN // MXU_EDGE):
        blk = xcb[:, q * MXU_EDGE:(q + 1) * MXU_EDGE]
        r_parts.append(jnp.dot(blk, wr_ref[q], preferred_element_type=jnp.float32))
        i_parts.append(jnp.dot(blk, wi_ref[q], preferred_element_type=jnp.float32))
    r = jax.nn.sigmoid(jnp.concatenate(r_parts, axis=1) + br_ref[...])
    ig = jax.nn.sigmoid(jnp.concatenate(i_parts, axis=1) + bi_ref[...])

    z = -lam_ref[...]
    softplus = jnp.maximum(z, 0.0) + jnp.log1p(jnp.exp(-jnp.abs(z)))
    log_a = (-LRU_C * r) * softplus
    a = jnp.exp(log_a)
    b = jnp.sqrt(-jnp.tanh(log_a) * (a * a + 1.0)) * (ig * xc)

    steps = lambda v, t: v[t * SUBLANES:(t + 1) * SUBLANES]
    if chain:
        h = jnp.zeros((SUBLANES, D_RNN), jnp.float32)
        prod = jnp.ones((SUBLANES, D_RNN), jnp.float32)
        local, prods = [], []
        for t in range(LRU_SEG):
            h = steps(a, t) * h + steps(b, t)
            prod = steps(a, t) * prod
            local.append(h)
            prods.append(prod)
        carry, inits = h_init, []
        for s in range(LRU_SEGS):
            inits.append(carry)
            carry = prod[s:s + 1] * carry + h[s:s + 1]
        init = jnp.concatenate(inits, axis=0)
        hs = [local[t] + prods[t] * init for t in range(LRU_SEG)]
    else:
        h = h_init
        hs = []
        for t in range(LRU_SEG):
            h = steps(a, t) * h + steps(b, t)
            hs.append(h)
    h_end = hs[-1]

    gate_in = jnp.dot(xb, win_ref[:, :D_RNN], preferred_element_type=jnp.float32)
    y = (jnp.concatenate(hs, axis=0) * jax.nn.gelu(gate_in, approximate=True)).astype(jnp.bfloat16)
    y_sm = jnp.dot(perm_ref[...], y, preferred_element_type=jnp.float32).astype(jnp.bfloat16)
    out = jnp.dot(y_sm, wout_ref[...], preferred_element_type=jnp.float32)
    return _layer_norm(DEEPNORM_ALPHA * x_sm + out, g_ref[...], b_ref[...]), rec, h_end


def _lru_kernel(xp_ref, xs_ref, conv_in_ref, h0_ref, perm_ref, win_ref, cw_ref, cb_ref, wr_ref, wi_ref, br_ref,
                bi_ref, lam_ref, wout_ref, g_ref, b_ref, x1_ref, conv_p_ref, h_p_ref, conv_s_ref, h_s_ref,
                tail_scr, h_scr):
    c = pl.program_id(0)
    t = pl.program_id(1)
    weights = (perm_ref, win_ref, cw_ref, cb_ref, wr_ref, wi_ref, br_ref, bi_ref, lam_ref, wout_ref, g_ref, b_ref)
    rows = ROW_TILE

    def interleaved(x_ref):
        return jnp.concatenate([x_ref[:, g, :] for g in range(LRU_SEG)], axis=0)

    @pl.when(c < BATCH)
    def _():
        @pl.when(t == 0)
        def _():
            tail_scr[...] = jnp.zeros_like(tail_scr)
            h_scr[...] = jnp.zeros_like(h_scr)

        pre = [tail_scr[j:j + 1, :] for j in range(CONV_WIDTH - 1)]
        x1, rec, h_end = _lru_tile(interleaved(xp_ref), xp_ref[...].reshape(rows, D_MODEL), pre, h_scr[...], True,
                                   *weights)
        last = LRU_SEGS - 1
        for j in range(CONV_WIDTH - 1):
            row = (LRU_SEG - (CONV_WIDTH - 1) + j) * SUBLANES + last
            tail_scr[j:j + 1, :] = rec[row:row + 1]
            conv_p_ref[j:j + 1, :] = rec[row:row + 1]
        h_scr[...] = h_end[last:last + 1]
        h_p_ref[...] = h_end[last:last + 1]
        x1_ref[...] = x1

    @pl.when(c == SAMPLE_CHUNK)
    def _():
        pre = [conv_in_ref[:, j, :] for j in range(CONV_WIDTH - 1)]
        x1, rec, h_end = _lru_tile(interleaved(xs_ref), xs_ref[...].reshape(rows, D_MODEL), pre, h0_ref[...], False,
                                   *weights)
        for j in range(CONV_WIDTH - 1):
            lo = (LRU_SEG - (CONV_WIDTH - 1) + j) * SUBLANES
            conv_s_ref[:, j, :] = rec[lo:lo + SUBLANES]
        h_s_ref[...] = h_end
        x1_ref[...] = x1


_TILES_PER_CHUNK = MOE_CHUNK // ROW_TILE


def _prompt_tile(c, t):
    return jnp.where(c < BATCH, t, _TILES_PER_CHUNK - 1)


def _sample_tile(c, t):
    return jnp.where(c == SAMPLE_CHUNK, t, 0)


def _mixer_act_specs():
    return [
        pl.BlockSpec((None, ROW_TILE, D_MODEL), lambda c, t: (jnp.minimum(c, BATCH - 1), _prompt_tile(c, t), 0)),
        pl.BlockSpec((None, ROW_TILE, D_MODEL), lambda c, t: (0, _sample_tile(c, t), 0)),
    ]


_MIXER_GRID = (N_CHUNKS, _TILES_PER_CHUNK)
_PER_PROMPT = lambda c, t: (jnp.minimum(c, BATCH - 1), 0, 0)
_PER_SAMPLE3 = lambda c, t: (_sample_tile(c, t), 0, 0)
_PER_SAMPLE4 = lambda c, t: (_sample_tile(c, t), 0, 0, 0)


def _lru(xp, xs, conv_in, h0, w, ln_g, ln_b):
    seg_tile = (None, LRU_SEGS, LRU_SEG, D_MODEL)
    row = lax.broadcasted_iota(jnp.int32, (ROW_TILE, ROW_TILE), 0)
    col = lax.broadcasted_iota(jnp.int32, (ROW_TILE, ROW_TILE), 1)
    perm = (col == (row % LRU_SEG) * LRU_SEGS + row // LRU_SEG).astype(jnp.bfloat16)
    return pl.pallas_call(
        _lru_kernel,
        grid=_MIXER_GRID,
        in_specs=[
            pl.BlockSpec(seg_tile, lambda c, t: (jnp.minimum(c, BATCH - 1), _prompt_tile(c, t), 0, 0)),
            pl.BlockSpec(seg_tile, lambda c, t: (0, _sample_tile(c, t), 0, 0)),
            pl.BlockSpec((LRU_SEGS, CONV_WIDTH - 1, D_RNN), _PER_SAMPLE3),
            pl.BlockSpec((LRU_SEGS, D_RNN), lambda c, t: (_sample_tile(c, t), 0)),
            _const_spec((ROW_TILE, ROW_TILE)),
            _const_spec((D_MODEL, 2 * D_RNN)),
            _const_spec((CONV_WIDTH, D_RNN)),
            _const_spec((1, D_RNN)),
            _const_spec((D_RNN // MXU_EDGE, MXU_EDGE, MXU_EDGE)),
            _const_spec((D_RNN // MXU_EDGE, MXU_EDGE, MXU_EDGE)),
            _const_spec((1, D_RNN)),
            _const_spec((1, D_RNN)),
            _const_spec((1, D_RNN)),
            _const_spec((D_RNN, D_MODEL)),
            _const_spec((1, D_MODEL)),
            _const_spec((1, D_MODEL)),
        ],
        out_specs=[
            pl.BlockSpec((None, ROW_TILE, D_MODEL), lambda c, t: (c, t, 0)),
            pl.BlockSpec((None, CONV_WIDTH - 1, D_RNN), _PER_PROMPT),
            pl.BlockSpec((None, 1, D_RNN), _PER_PROMPT),
            pl.BlockSpec((LRU_SEGS, CONV_WIDTH - 1, D_RNN), _PER_SAMPLE3),
            pl.BlockSpec((LRU_SEGS, D_RNN), lambda c, t: (_sample_tile(c, t), 0)),
        ],
        out_shape=[
            _ACT_SHAPE,
            jax.ShapeDtypeStruct((BATCH, CONV_WIDTH - 1, D_RNN), jnp.float32),
            jax.ShapeDtypeStruct((BATCH, 1, D_RNN), jnp.float32),
            jax.ShapeDtypeStruct((DEC_BATCH, CONV_WIDTH - 1, D_RNN), jnp.float32),
            jax.ShapeDtypeStruct((DEC_BATCH, D_RNN), jnp.float32),
        ],
        scratch_shapes=[
            pltpu.VMEM((SUBLANES, D_RNN), jnp.float32),
            pltpu.VMEM((1, D_RNN), jnp.float32),
        ],
        compiler_params=pltpu.CompilerParams(
            dimension_semantics=("arbitrary", "arbitrary"), vmem_limit_bytes=VMEM_LIMIT),
        name="lru",
    )(xp.reshape(BATCH, SEQ // LRU_SEG, LRU_SEG, D_MODEL), xs.reshape(1, MOE_CHUNK // LRU_SEG, LRU_SEG, D_MODEL),
      conv_in, h0, perm, *w, ln_g, ln_b)


def _rope(x, cos, sin_signed):
    lane = lax.broadcasted_iota(jnp.int32, (x.shape[0], LANES), 1)
    first_half = (lane & (HEAD_DIM - 1)) < HEAD_DIM // 2
    cols = []
    for c in range(x.shape[1] // LANES):
        xc = x[:, c * LANES:(c + 1) * LANES]
        partner = jnp.where(first_half, pltpu.roll(xc, LANES - HEAD_DIM // 2, 1),
                            pltpu.roll(xc, HEAD_DIM // 2, 1))
        cols.append(xc * cos + partner * sin_signed)
    return cols[0] if len(cols) == 1 else jnp.concatenate(cols, axis=1)


def _dup_heads(kv):
    lane = lax.broadcasted_iota(jnp.int32, kv.shape, 1)
    low = lane < HEAD_DIM
    swapped = pltpu.roll(kv, HEAD_DIM, 1)
    return (jnp.where(low, kv, swapped).astype(jnp.bfloat16),
            jnp.where(low, swapped, kv).astype(jnp.bfloat16))


def _attend(chunks, sinks_ref):
    lane = lax.broadcasted_iota(jnp.int32, (CHUNK, LANES), 1)
    low = lane < HEAD_DIM
    zero = jnp.zeros((CHUNK, LANES), jnp.bfloat16)
    cols_per_group = KV_GROUP * HEAD_DIM // LANES
    units = [(ci, g) for ci in range(len(chunks)) for g in range(N_KV_HEADS)]

    scores = {}
    for ci, g in units:
        q, k_wins, _, valid = chunks[ci]
        stacked = []
        for c in range(g * cols_per_group, (g + 1) * cols_per_group):
            qc = q[:, c * LANES:(c + 1) * LANES]
            stacked.append(jnp.where(low, qc, zero))
            stacked.append(jnp.where(low, zero, qc))
        qst = jnp.concatenate(stacked, axis=0)
        s = lax.dot_general(qst, k_wins[g], (((1,), (1,)), ((), ())),
                            preferred_element_type=jnp.float32)
        scores[ci, g] = s if valid is None else jnp.where(valid, s, NEG_BIG)

    probs = {}
    for ci, g in units:
        s = scores[ci, g]
        heads = []
        for hh in range(KV_GROUP):
            sh = s[hh * CHUNK:(hh + 1) * CHUNK]
            sink = sinks_ref[g * KV_GROUP + hh]
            m = jnp.maximum(jnp.max(sh, -1, keepdims=True), sink)
            p = jnp.exp(sh - m)
            denom = jnp.sum(p, -1, keepdims=True) + jnp.exp(sink - m)
            heads.append((p * (1.0 / denom)).astype(jnp.bfloat16))
        probs[ci, g] = jnp.concatenate(heads, axis=0)

    out_cols = {ci: [] for ci in range(len(chunks))}
    for ci, g in units:
        o = jnp.dot(probs[ci, g], chunks[ci][2][g], preferred_element_type=jnp.float32)
        for j in range(cols_per_group):
            oa = o[(2 * j) * CHUNK:(2 * j + 1) * CHUNK]
            ob = o[(2 * j + 1) * CHUNK:(2 * j + 2) * CHUNK]
            out_cols[ci].append(jnp.where(low, oa, ob))
    return [jnp.concatenate(out_cols[ci], axis=1) for ci in range(len(chunks))]


def _project_qkv(x, wqkv_ref, bqkv_ref, cos, sin):
    nq = N_HEADS * HEAD_DIM
    nk = N_KV_HEADS * HEAD_DIM
    xb = x.astype(jnp.bfloat16)
    kv = jnp.dot(xb, wqkv_ref[:, nq:], preferred_element_type=jnp.float32) + bqkv_ref[:, nq:]
    k = _rope(kv[:, :nk], cos, sin)
    v = kv[:, nk:]
    q = jnp.dot(xb, wqkv_ref[:, :nq], preferred_element_type=jnp.float32) + bqkv_ref[:, :nq]
    q = _rope(q, cos, sin)
    qs = (q * (HEAD_DIM ** -0.5)).astype(jnp.bfloat16)
    return qs, k, v


SUB_TILE = 256


def _sub_tiles():
    return [pl.ds(i * SUB_TILE, SUB_TILE) for i in range(ROW_TILE // SUB_TILE)]


def _project_out(xs, subs, o, wo_ref, g_ref, b_ref, x1_ref):
    for x, sub, i in zip(xs, subs, range(len(subs))):
        out = jnp.dot(o[i * SUB_TILE:(i + 1) * SUB_TILE].astype(jnp.bfloat16), wo_ref[...],
                      preferred_element_type=jnp.float32)
        x1_ref[sub, :] = _layer_norm(DEEPNORM_ALPHA * x + out, g_ref[...], b_ref[...])


def _swa_kernel(xp_ref, xs_ref, ck_ref, cv_ref, wqkv_ref, bqkv_ref, cosp_ref, sinp_ref, coss_ref, sins_ref,
                sinks_ref, wo_ref, g_ref, b_ref, x1_ref, kp_ref, vp_ref, ks_ref, vs_ref,
                k0_scr, k1_scr, v0_scr, v1_scr):
    c = pl.program_id(0)
    t = pl.program_id(1)
    scrs = (k0_scr, k1_scr, v0_scr, v1_scr)

    @pl.when(c < BATCH)
    def _():
        @pl.when(t == 0)
        def _():
            for scr in scrs:
                scr[0:WINDOW, :] = jnp.zeros((WINDOW, LANES), jnp.bfloat16)

        subs = _sub_tiles()
        xs_ = [xp_ref[sub, :] for sub in subs]
        proj = [_project_qkv(x, wqkv_ref, bqkv_ref, cosp_ref[sub, :], sinp_ref[sub, :]) for x, sub in zip(xs_, subs)]
        qs = jnp.concatenate([p[0] for p in proj], axis=0)
        k = jnp.concatenate([p[1] for p in proj], axis=0)
        v = jnp.concatenate([p[2] for p in proj], axis=0)
        kp_ref[...] = k[ROW_TILE - WINDOW:]
        vp_ref[...] = v[ROW_TILE - WINDOW:]
        k0, k1 = _dup_heads(k)
        v0, v1 = _dup_heads(v)
        for scr, val in zip(scrs, (k0, k1, v0, v1)):
            scr[WINDOW:WINDOW + ROW_TILE, :] = val

        span = WINDOW + CHUNK
        key_pos = lax.broadcasted_iota(jnp.int32, (1, span), 1)
        chunks = []
        for ci in range(ROW_TILE // CHUNK):
            lo = ci * CHUNK
            valid = (t * ROW_TILE + lo - WINDOW + key_pos) >= 0
            k_wins = (k0_scr[lo:lo + span, :], k1_scr[lo:lo + span, :])
            v_wins = (v0_scr[lo:lo + span, :], v1_scr[lo:lo + span, :])
            chunks.append((qs[lo:lo + CHUNK], k_wins, v_wins, valid))
        o = jnp.concatenate(_attend(chunks, sinks_ref), axis=0)
        for scr in scrs:
            scr[0:WINDOW, :] = scr[ROW_TILE:ROW_TILE + WINDOW, :]
        _project_out(xs_, subs, o, wo_ref, g_ref, b_ref, x1_ref)

    @pl.when(c == SAMPLE_CHUNK)
    def _():
        subs = _sub_tiles()
        xs_ = [xs_ref[sub, :] for sub in subs]
        proj = [_project_qkv(x, wqkv_ref, bqkv_ref, coss_ref[sub, :], sins_ref[sub, :]) for x, sub in zip(xs_, subs)]
        qs = jnp.concatenate([p[0] for p in proj], axis=0)
        k = jnp.concatenate([p[1] for p in proj], axis=0)
        v = jnp.concatenate([p[2] for p in proj], axis=0)
        knew = _dup_heads(k)
        vnew = _dup_heads(v)
        chunks = []
        for s in range(SAMPLE_SEGS):
            lo = s * DEC_SEQ
            ck = ck_ref[s]
            cv = cv_ref[s]
            kold = _dup_heads(ck)
            vold = _dup_heads(cv)
            k_wins = [jnp.concatenate([kold[g], knew[g][lo:lo + DEC_SEQ]], axis=0) for g in range(N_KV_HEADS)]
            v_wins = [jnp.concatenate([vold[g], vnew[g][lo:lo + DEC_SEQ]], axis=0) for g in range(N_KV_HEADS)]
            chunks.append((qs[lo:lo + DEC_SEQ], k_wins, v_wins, None))
            ks_ref[s, 0:WINDOW - DEC_SEQ, :] = ck[DEC_SEQ:]
            ks_ref[s, WINDOW - DEC_SEQ:WINDOW, :] = k[lo:lo + DEC_SEQ]
            vs_ref[s, 0:WINDOW - DEC_SEQ, :] = cv[DEC_SEQ:]
            vs_ref[s, WINDOW - DEC_SEQ:WINDOW, :] = v[lo:lo + DEC_SEQ]
        o = jnp.concatenate(_attend(chunks, sinks_ref), axis=0)
        _project_out(xs_, subs, o, wo_ref, g_ref, b_ref, x1_ref)


def _swa(xp, xs, ck, cv, w, cos_p, sin_p, cos_s, sin_s, ln_g, ln_b):
    wqkv, bqkv, sinks, wo = w
    nt_s = DEC_BATCH // SAMPLE_SEGS
    kv_lanes = N_KV_HEADS * HEAD_DIM
    cache_spec = pl.BlockSpec((None, SAMPLE_SEGS, WINDOW, kv_lanes), _PER_SAMPLE4)
    prompt_table = pl.BlockSpec((ROW_TILE, LANES), lambda c, t: (_prompt_tile(c, t), 0))
    kv_prompt = pl.BlockSpec((None, WINDOW, kv_lanes), _PER_PROMPT)
    return pl.pallas_call(
        _swa_kernel,
        grid=_MIXER_GRID,
        in_specs=_mixer_act_specs() + [
            cache_spec, cache_spec,
            _const_spec((D_MODEL, QKV_DIM)),
            _const_spec((1, QKV_DIM)),
            prompt_table, prompt_table,
            _const_spec((ROW_TILE, LANES)), _const_spec((ROW_TILE, LANES)),
            pl.BlockSpec(memory_space=pltpu.SMEM),
            _const_spec((N_HEADS * HEAD_DIM, D_MODEL)),
            _const_spec((1, D_MODEL)),
            _const_spec((1, D_MODEL)),
        ],
        out_specs=[
            pl.BlockSpec((None, ROW_TILE, D_MODEL), lambda c, t: (c, t, 0)),
            kv_prompt, kv_prompt, cache_spec, cache_spec,
        ],
        out_shape=[
            _ACT_SHAPE,
            jax.ShapeDtypeStruct((BATCH, WINDOW, kv_lanes), jnp.float32),
            jax.ShapeDtypeStruct((BATCH, WINDOW, kv_lanes), jnp.float32),
            jax.ShapeDtypeStruct((nt_s, SAMPLE_SEGS, WINDOW, kv_lanes), jnp.float32),
            jax.ShapeDtypeStruct((nt_s, SAMPLE_SEGS, WINDOW, kv_lanes), jnp.float32),
        ],
        scratch_shapes=[pltpu.VMEM((WINDOW + ROW_TILE, LANES), jnp.bfloat16) for _ in range(4)],
        compiler_params=pltpu.CompilerParams(
            dimension_semantics=("arbitrary", "arbitrary"), vmem_limit_bytes=VMEM_LIMIT),
        name="swa",
    )(xp, xs, ck, cv, wqkv, bqkv, cos_p, sin_p, cos_s, sin_s, sinks, wo, ln_g, ln_b)


(META_TILE_EXPERT, META_N_ACTIVE, META_FILL_LO, META_FILL_HI, META_IS_NEW, META_TILE_RUN, META_RUN_EXPERT,
 META_N_RUNS) = range(SUBLANES)
WEIGHT_SLOTS = 3


def _router_kernel(x_ref, wr_ref, br_ref, pos_ref, gate_ref, meta_ref):
    f32 = jnp.float32
    x = x_ref[...]
    w = wr_ref[...]
    xh = x.astype(jnp.bfloat16)
    xl = (x - xh.astype(f32)).astype(jnp.bfloat16)
    wh = w.astype(jnp.bfloat16)
    wl = (w - wh.astype(f32)).astype(jnp.bfloat16)
    nt = (((1,), (1,)), ((), ()))
    logits = (lax.dot_general(wh, xh, nt, preferred_element_type=f32)
              + lax.dot_general(wl, xh, nt, preferred_element_type=f32)
              + lax.dot_general(wh, xl, nt, preferred_element_type=f32))
    m = jnp.max(logits, axis=0, keepdims=True)
    e = jnp.exp(logits - m)
    probs = e / jnp.sum(e, axis=0, keepdims=True)
    sel = probs + br_ref[...]
    p = [probs[i:i + 1] for i in range(N_EXPERTS)]
    s = [sel[i:i + 1] for i in range(N_EXPERTS)]

    def first_argmax(vals):
        best, arg = vals[0], jnp.zeros_like(vals[0], dtype=jnp.int32)
        for i in range(1, len(vals)):
            better = vals[i] > best
            best = jnp.where(better, vals[i], best)
            arg = jnp.where(better, i, arg)
        return arg

    group_scores = []
    for gidx in range(N_GROUPS):
        v = s[gidx * EXPERTS_PER_GROUP:(gidx + 1) * EXPERTS_PER_GROUP]
        best = v[0] + v[1]
        for i in range(EXPERTS_PER_GROUP):
            for j in range(i + 1, EXPERTS_PER_GROUP):
                if (i, j) != (0, 1):
                    best = jnp.maximum(best, v[i] + v[j])
        group_scores.append(best)
    best_group = first_argmax(group_scores)
    neg_inf = jnp.full_like(s[0], -jnp.inf)
    masked = [jnp.where(best_group == (i // EXPERTS_PER_GROUP), s[i], neg_inf) for i in range(N_EXPERTS)]
    idx1 = first_argmax(masked)
    masked2 = [jnp.where(idx1 == i, neg_inf, masked[i]) for i in range(N_EXPERTS)]
    idx2 = first_argmax(masked2)
    zero = jnp.zeros_like(p[0])
    w1, w2 = zero, zero
    for i in range(N_EXPERTS):
        w1 = w1 + jnp.where(idx1 == i, p[i], zero)
        w2 = w2 + jnp.where(idx2 == i, p[i], zero)
    tot = w1 + w2
    gate_ref[0:1, :] = w1 / tot
    gate_ref[1:2, :] = w2 / tot

    expert = lax.broadcasted_iota(jnp.int32, (N_EXPERTS, MOE_CHUNK), 0)
    strict_upper = (lax.broadcasted_iota(jnp.int32, (LANES, LANES), 0)
                    < lax.broadcasted_iota(jnp.int32, (LANES, LANES), 1)).astype(jnp.bfloat16)
    carry = jnp.zeros((N_EXPERTS, 1), f32)
    onehots, ranks = [], []
    for idx in (idx1, idx2):
        onehot = (expert == idx).astype(f32)
        before = []
        for blk in range(MOE_CHUNK // LANES):
            oh = onehot[:, blk * LANES:(blk + 1) * LANES]
            before.append(jnp.dot(oh.astype(jnp.bfloat16), strict_upper, preferred_element_type=f32) + carry)
            carry = carry + jnp.sum(oh, axis=1, keepdims=True)
        onehots.append(onehot)
        ranks.append(jnp.sum(onehot * jnp.concatenate(before, axis=1), axis=0, keepdims=True))
    counts = carry
    padded = jnp.floor((counts + (MOE_TM - 1)) * (1.0 / MOE_TM)) * MOE_TM
    strict_lower = (lax.broadcasted_iota(jnp.int32, (N_EXPERTS, N_EXPERTS), 0)
                    > lax.broadcasted_iota(jnp.int32, (N_EXPERTS, N_EXPERTS), 1)).astype(jnp.bfloat16)
    starts = jnp.dot(strict_lower, jnp.broadcast_to(padded, (N_EXPERTS, LANES)).astype(jnp.bfloat16),
                     preferred_element_type=f32)[:, 0:1]
    ends = starts + padded
    for k in range(2):
        pos = jnp.sum(onehots[k] * starts, axis=0, keepdims=True) + ranks[k]
        pos_ref[k:k + 1, :] = pos.astype(jnp.int32)

    lane = lax.broadcasted_iota(jnp.int32, (N_EXPERTS, LANES), 1)
    sub = lax.broadcasted_iota(jnp.int32, (N_EXPERTS, LANES), 0)
    tile_start = (lane * MOE_TM).astype(f32)
    tile_expert = jnp.sum((ends <= tile_start).astype(f32), axis=0, keepdims=True)
    tile_expert = jnp.minimum(tile_expert, N_EXPERTS - 1.0)
    on_diag = sub == lane

    def to_lanes(col):
        return jnp.sum(jnp.where(on_diag, col, 0.0), axis=0, keepdims=True)

    n_active = jnp.broadcast_to(ends[N_EXPERTS - 1:N_EXPERTS] * (1.0 / MOE_TM), (1, LANES))
    lane1 = lax.broadcasted_iota(jnp.int32, (1, LANES), 1)
    active = lane1.astype(f32) < n_active
    starts_run = jnp.logical_or(lane1 == 0, tile_expert != pltpu.roll(tile_expert, 1, 1))
    is_new = jnp.where(jnp.logical_and(active, starts_run), 1.0, 0.0)
    upper_incl = (lax.broadcasted_iota(jnp.int32, (LANES, LANES), 0)
                  <= lax.broadcasted_iota(jnp.int32, (LANES, LANES), 1)).astype(jnp.bfloat16)
    tile_run = jnp.dot(jnp.broadcast_to(is_new, (SUBLANES, LANES)).astype(jnp.bfloat16), upper_incl,
                       preferred_element_type=f32)[0:1] - 1.0
    first_of_run = jnp.logical_and(is_new > 0.0, tile_run == sub.astype(f32))
    run_expert = jnp.sum(jnp.where(first_of_run, tile_expert, 0.0), axis=1, keepdims=True)
    n_runs = jnp.broadcast_to(jnp.sum(is_new, axis=1, keepdims=True), (1, LANES))

    meta_ref[...] = jnp.zeros(meta_ref.shape, jnp.int32)
    meta_ref[META_IS_NEW:META_IS_NEW + 1, :] = is_new.astype(jnp.int32)
    meta_ref[META_TILE_RUN:META_TILE_RUN + 1, :] = tile_run.astype(jnp.int32)
    meta_ref[META_RUN_EXPERT:META_RUN_EXPERT + 1, :] = to_lanes(run_expert).astype(jnp.int32)
    meta_ref[META_N_RUNS:META_N_RUNS + 1, :] = n_runs.astype(jnp.int32)
    meta_ref[META_TILE_EXPERT:META_TILE_EXPERT + 1, :] = tile_expert.astype(jnp.int32)
    meta_ref[META_N_ACTIVE:META_N_ACTIVE + 1, :] = n_active.astype(jnp.int32)
    meta_ref[META_FILL_LO:META_FILL_LO + 1, :] = to_lanes(starts + counts).astype(jnp.int32)
    meta_ref[META_FILL_HI:META_FILL_HI + 1, :] = to_lanes(ends).astype(jnp.int32)


def _router(x, wr_t, br):
    per_chunk = lambda c: (c, 0, 0)
    return pl.pallas_call(
        _router_kernel,
        grid=(N_CHUNKS,),
        in_specs=[
            pl.BlockSpec((None, MOE_CHUNK, D_MODEL), per_chunk),
            _const_spec((N_EXPERTS, D_MODEL)),
            _const_spec((N_EXPERTS, 1)),
        ],
        out_specs=[
            pl.BlockSpec((None, 2, MOE_CHUNK), per_chunk),
            pl.BlockSpec((None, 2, MOE_CHUNK), per_chunk),
            pl.BlockSpec((None, SUBLANES, LANES), per_chunk),
        ],
        out_shape=[
            jax.ShapeDtypeStruct((N_CHUNKS, 2, MOE_CHUNK), jnp.int32),
            jax.ShapeDtypeStruct((N_CHUNKS, 2, MOE_CHUNK), jnp.float32),
            jax.ShapeDtypeStruct((N_CHUNKS, SUBLANES, LANES), jnp.int32),
        ],
        compiler_params=pltpu.CompilerParams(
            dimension_semantics=("parallel",), vmem_limit_bytes=VMEM_LIMIT),
        name="router",
    )(x, wr_t, br)


def _gather_rows(pair_scr, x_ref, xs_scr, row0, j):
    tok = pair_scr[row0 + j] & (MOE_CHUNK - 1)
    xs_scr[j // SUBLANES, (j % SUBLANES):(j % SUBLANES) + 1, :] = x_ref[pl.ds(tok, 1), :]


def _scatter_rows(pair_scr, y_scr, yo_scr, row0, j):
    y_scr[pl.ds(pair_scr[row0 + j], 1), :] = yo_scr[j // SUBLANES, (j % SUBLANES):(j % SUBLANES) + 1, :]


def _moe_kernel(layer, meta_ref, pos_ref, x_ref, gate_ref,
                wg_hbm, wu_hbm, wd_hbm, g_ref, b_ref, outp_hbm, outs_hbm,
                pair_scr, xs_a, xs_b, yo_a, yo_b, y_scr, wg_buf, wu_buf, wd_buf, junk_buf, w_sem, out_sem, fill_sem):
    c = pl.program_id(0)

    def plan(row, i):
        return meta_ref[(c * SUBLANES + row) * LANES + i]

    n_act = plan(META_N_ACTIVE, 0)

    n_runs = plan(META_N_RUNS, 0)

    def weight_copies(run):
        e, s = plan(META_RUN_EXPERT, run), lax.rem(run, WEIGHT_SLOTS)
        return (pltpu.make_async_copy(wg_hbm.at[layer, e], wg_buf.at[s], w_sem.at[s, 0]),
                pltpu.make_async_copy(wu_hbm.at[layer, e], wu_buf.at[s], w_sem.at[s, 1]),
                pltpu.make_async_copy(wd_hbm.at[layer, e], wd_buf.at[s], w_sem.at[s, 2]))

    def prompt_out_copy(chunk):
        return pltpu.make_async_copy(y_scr.at[pl.ds(0, MOE_CHUNK)], outp_hbm.at[chunk], out_sem.at[0])

    def sample_out_copy():
        return pltpu.make_async_copy(y_scr.at[pl.ds(0, MOE_CHUNK)], outs_hbm.at[0], out_sem.at[0])

    for ahead in range(WEIGHT_SLOTS - 1):
        @pl.when(ahead < n_runs)
        def _():
            for cp in weight_copies(ahead):
                cp.start()

    @pl.when(c == 0)
    def _():
        junk_buf[...] = jnp.full(junk_buf.shape, JUNK_PAIR, jnp.int32)
    fill = pltpu.make_async_copy(junk_buf, pair_scr, fill_sem.at[0])
    fill.start()
    fill.wait()

    def invert(t, carry):
        pair_scr[pos_ref[t]] = t
        pair_scr[pos_ref[MOE_CHUNK + t]] = MOE_CHUNK + t
        return carry
    lax.fori_loop(0, MOE_CHUNK, invert, 0, unroll=8)

    def gather0(gi, carry):
        for u in range(SUBLANES):
            tok = pair_scr[gi * SUBLANES + u] & (MOE_CHUNK - 1)
            xs_a[gi, u:u + 1, :] = x_ref[pl.ds(tok, 1), :]
        return carry
    lax.fori_loop(0, MOE_TM // SUBLANES, gather0, 0)

    @pl.when(c == 0)
    def _():
        yo_b[...] = jnp.zeros_like(yo_b)

    @pl.when(c > 0)
    def _():
        prompt_out_copy(c - 1).wait()

    def expert_step(t, s, xs_cur, yo_cur, xs_nxt, yo_prv):
        row_nxt = (t + 1) * MOE_TM
        row_prv = jnp.maximum(t - 1, 0) * MOE_TM
        for j in range(MOE_TM):
            _scatter_rows(pair_scr, y_scr, yo_prv, row_prv, j)
            _gather_rows(pair_scr, x_ref, xs_nxt, row_nxt, j)
        xb = xs_cur[...].reshape(MOE_TM, D_MODEL).astype(jnp.bfloat16)
        hg = jnp.dot(xb, wg_buf[s], preferred_element_type=jnp.float32)
        hu = jnp.dot(xb, wu_buf[s], preferred_element_type=jnp.float32)
        act = (jax.nn.silu(hg) * hu).astype(jnp.bfloat16)
        yo = jnp.dot(act, wd_buf[s], preferred_element_type=jnp.float32)
        yo_cur[...] = yo.reshape(MOE_TM // SUBLANES, SUBLANES, D_MODEL)

    def tile_iteration(t, carry):
        run = plan(META_TILE_RUN, t)
        starts_run = plan(META_IS_NEW, t) == 1

        @pl.when(starts_run)
        def _():
            for cp in weight_copies(run):
                cp.wait()

        @pl.when(jnp.logical_and(starts_run, run + WEIGHT_SLOTS - 1 < n_runs))
        def _():
            for cp in weight_copies(run + WEIGHT_SLOTS - 1):
                cp.start()

        slot = lax.rem(run, WEIGHT_SLOTS)
        for s in range(WEIGHT_SLOTS):
            in_slot = slot == s

            @pl.when(jnp.logical_and((t & 1) == 0, in_slot))
            def _():
                expert_step(t, s, xs_a, yo_a, xs_b, yo_b)

            @pl.when(jnp.logical_and((t & 1) == 1, in_slot))
            def _():
                expert_step(t, s, xs_b, yo_b, xs_a, yo_a)
        return carry
    lax.fori_loop(0, n_act, tile_iteration, 0)

    def drain(yo_last):
        row_last = (n_act - 1) * MOE_TM

        def scatter(gi, carry):
            for u in range(SUBLANES):
                y_scr[pl.ds(pair_scr[row_last + gi * SUBLANES + u], 1), :] = yo_last[gi, u:u + 1, :]
            return carry
        lax.fori_loop(0, MOE_TM // SUBLANES, scatter, 0)

    @pl.when((n_act & 1) == 1)
    def _():
        drain(yo_a)

    @pl.when((n_act & 1) == 0)
    def _():
        drain(yo_b)

    def combine(k, carry):
        lo = pl.multiple_of(k * ROW_TILE, ROW_TILE)
        gates = gate_ref[pl.ds(lo, ROW_TILE), :]
        moe = (gates[:, 0:1] * y_scr[pl.ds(lo, ROW_TILE), :]
               + gates[:, 1:2] * y_scr[pl.ds(MOE_CHUNK + lo, ROW_TILE), :])
        x = x_ref[pl.ds(lo, ROW_TILE), :]
        y_scr[pl.ds(lo, ROW_TILE), :] = _layer_norm(DEEPNORM_ALPHA * x + moe, g_ref[...], b_ref[...])
        return carry
    lax.fori_loop(0, COMBINE_STEPS, combine, 0)

    @pl.when(c < SAMPLE_CHUNK)
    def _():
        prompt_out_copy(c).start()

    @pl.when(c == SAMPLE_CHUNK)
    def _():
        sample_out_copy().start()
        sample_out_copy().wait()


def _moe(x, meta, pos, gates_t, layer, wg, wu, wd, ln_g, ln_b):
    stage = pltpu.VMEM((MOE_TM // SUBLANES, SUBLANES, D_MODEL), jnp.float32)
    per_chunk = lambda c, *_: (c, 0, 0)
    grid_spec = pltpu.PrefetchScalarGridSpec(
        num_scalar_prefetch=1,
        grid=(N_CHUNKS,),
        in_specs=[
            pl.BlockSpec((None, None, 2 * MOE_CHUNK), per_chunk, memory_space=pltpu.SMEM),
            pl.BlockSpec((None, MOE_CHUNK, D_MODEL), per_chunk),
            pl.BlockSpec((None, MOE_CHUNK, 2), per_chunk),
            pl.BlockSpec(memory_space=pl.ANY),
            pl.BlockSpec(memory_space=pl.ANY),
            pl.BlockSpec(memory_space=pl.ANY),
            pl.BlockSpec((1, D_MODEL), lambda c, *_: (0, 0)),
            pl.BlockSpec((1, D_MODEL), lambda c, *_: (0, 0)),
        ],
        out_specs=[pl.BlockSpec(memory_space=pl.ANY), pl.BlockSpec(memory_space=pl.ANY)],
        scratch_shapes=[
            pltpu.SMEM((PAIR_ROWS,), jnp.int32),
            stage, stage, stage, stage,
            pltpu.VMEM((2 * MOE_CHUNK + SUBLANES, D_MODEL), jnp.float32),
            pltpu.VMEM((WEIGHT_SLOTS, D_MODEL, D_EXPERT), jnp.bfloat16),
            pltpu.VMEM((WEIGHT_SLOTS, D_MODEL, D_EXPERT), jnp.bfloat16),
            pltpu.VMEM((WEIGHT_SLOTS, D_EXPERT, D_MODEL), jnp.bfloat16),
            pltpu.VMEM((PAIR_ROWS,), jnp.int32),
            pltpu.SemaphoreType.DMA((WEIGHT_SLOTS, 3)),
            pltpu.SemaphoreType.DMA((1,)),
            pltpu.SemaphoreType.DMA((1,)),
        ],
    )
    return pl.pallas_call(
        functools.partial(_moe_kernel, layer),
        grid_spec=grid_spec,
        out_shape=[jax.ShapeDtypeStruct((BATCH, MOE_CHUNK, D_MODEL), jnp.float32),
                   jax.ShapeDtypeStruct((1, MOE_CHUNK, D_MODEL), jnp.float32)],
        compiler_params=pltpu.CompilerParams(
            dimension_semantics=("arbitrary",), vmem_limit_bytes=VMEM_LIMIT),
        name="moe",
    )(meta.reshape(-1), pos.reshape(N_CHUNKS, 1, 2 * MOE_CHUNK), x, gates_t, wg, wu, wd, ln_g, ln_b)


def _moe_layer(x, wr_t, br, layer, wg, wu, wd, ln_g, ln_b):
    pos, gates, meta = _router(x, wr_t, br)
    return _moe(x, meta, pos, jnp.swapaxes(gates, 1, 2), layer, wg, wu, wd, ln_g, ln_b)


def _rope_tables(pos):
    half = HEAD_DIM // 2
    inv_freq = ROPE_THETA ** (-jnp.arange(half, dtype=jnp.float32) / half)
    ang = pos.astype(jnp.float32)[:, None] * inv_freq[None, :]
    cos = jnp.cos(ang)
    sin = jnp.sin(ang)
    reps = LANES // HEAD_DIM
    cos_t = jnp.tile(jnp.concatenate([cos, cos], axis=1), (1, reps))
    sin_t = jnp.tile(jnp.concatenate([-sin, sin], axis=1), (1, reps))
    return cos_t, sin_t


def _block_diag_tiles(w):
    per = MXU_EDGE // LRU_BLOCK_W
    w4 = w.reshape(LRU_BLOCKS // per, per, LRU_BLOCK_W, LRU_BLOCK_W)
    eye = jnp.eye(per, dtype=w.dtype)
    return jnp.einsum("qaij,ab->qaibj", w4, eye).reshape(LRU_BLOCKS // per, MXU_EDGE, MXU_EDGE)


def kernel(x_prompt, x_sample, state_conv_0, state_h_0, cache_k_1, cache_v_1, state_conv_2, state_h_2,
           cache_k_3, cache_v_3, w_router, b_router, ln_g, ln_b, lru_w_in, lru_conv_w, lru_conv_b,
           lru_w_gates, lru_b_gates, lru_lambda, lru_w_out, attn_w_qkv, attn_b_qkv, attn_sinks, attn_w_o,
           moe_w_gate, moe_w_up, moe_w_down):
    bf = jnp.bfloat16
    in_state = {0: (state_conv_0, state_h_0), 1: (cache_k_1, cache_v_1),
                2: (state_conv_2, state_h_2), 3: (cache_k_3, cache_v_3)}
    nt_s = DEC_BATCH // SAMPLE_SEGS
    kv_lanes = N_KV_HEADS * HEAD_DIM

    wr_t = w_router.T
    br = b_router.reshape(N_EXPERTS, 1)
    cos_p, sin_p = _rope_tables(jnp.arange(SEQ))
    cos_s, sin_s = _rope_tables(PAST_LEN + jnp.arange(DEC_SEQ))
    cos_s = jnp.tile(cos_s, (SAMPLE_SEGS, 1))
    sin_s = jnp.tile(sin_s, (SAMPLE_SEGS, 1))
    wg_all = moe_w_gate.astype(bf)
    wu_all = moe_w_up.astype(bf)
    wd_all = moe_w_down.astype(bf)

    x = x_prompt
    x_s = x_sample.reshape(1, MOE_CHUNK, D_MODEL)
    new_p, new_s = {}, {}
    for i in range(DEPTH):
        slot = i // 2
        g0 = ln_g[i, 0].reshape(1, D_MODEL)
        b0 = ln_b[i, 0].reshape(1, D_MODEL)
        g1 = ln_g[i, 1].reshape(1, D_MODEL)
        b1 = ln_b[i, 1].reshape(1, D_MODEL)
        st_a, st_b = in_state[i]
        if i % 2 == 0:
            gates = lru_w_gates[slot]
            w = (lru_w_in[slot].astype(bf), lru_conv_w[slot], lru_conv_b[slot].reshape(1, D_RNN),
                 _block_diag_tiles(gates[:, :, :LRU_BLOCK_W]).astype(bf),
                 _block_diag_tiles(gates[:, :, LRU_BLOCK_W:]).astype(bf),
                 lru_b_gates[slot][:, :LRU_BLOCK_W].reshape(1, D_RNN),
                 lru_b_gates[slot][:, LRU_BLOCK_W:].reshape(1, D_RNN),
                 lru_lambda[slot].reshape(1, D_RNN), lru_w_out[slot].astype(bf))
            x1, conv_p, h_p, conv_s, h_s = _lru(x, x_s, st_a, st_b, w, g0, b0)
            new_p[i] = (conv_p, h_p.reshape(BATCH, D_RNN))
            new_s[i] = (conv_s, h_s)
        else:
            w = (attn_w_qkv[slot].astype(bf), attn_b_qkv[slot].reshape(1, QKV_DIM), attn_sinks[slot],
                 attn_w_o[slot].astype(bf))
            x1, k_p, v_p, k_s, v_s = _swa(
                x, x_s, st_a.reshape(nt_s, SAMPLE_SEGS, WINDOW, kv_lanes),
                st_b.reshape(nt_s, SAMPLE_SEGS, WINDOW, kv_lanes), w, cos_p, sin_p, cos_s, sin_s, g0, b0)
            new_p[i] = (k_p.reshape(BATCH, WINDOW, N_KV_HEADS, HEAD_DIM),
                        v_p.reshape(BATCH, WINDOW, N_KV_HEADS, HEAD_DIM))
            new_s[i] = (k_s.reshape(DEC_BATCH, WINDOW, N_KV_HEADS, HEAD_DIM),
                        v_s.reshape(DEC_BATCH, WINDOW, N_KV_HEADS, HEAD_DIM))
        x, x_s = _moe_layer(x1, wr_t, br, i, wg_all, wu_all, wd_all, g1, b1)
    return (x, x_s.reshape(DEC_BATCH, DEC_SEQ, D_MODEL),
            new_p[0][0], new_s[0][0], new_p[0][1], new_s[0][1],
            new_p[1][0], new_s[1][0], new_p[1][1], new_s[1][1],
            new_p[2][0], new_s[2][0], new_p[2][1], new_s[2][1],
            new_p[3][0], new_s[3][0], new_p[3][1], new_s[3][1])
```

```python
import functools

import jax
import jax.numpy as jnp
from jax import lax
from jax.experimental import pallas as pl
from jax.experimental.pallas import tpu as pltpu

D_MODEL = 1024
BATCH = 8
SEQ = 2048
DEPTH = 4
DEC_BATCH = 32
DEC_SEQ = 64
PAST_LEN = 4096
CHUNK = 64
D_RNN = D_MODEL
CONV_WIDTH = 4
LRU_BLOCKS = 16
LRU_BLOCK_W = D_RNN // LRU_BLOCKS
LRU_C = 8.0
N_HEADS = 16
N_KV_HEADS = 2
HEAD_DIM = 64
KV_GROUP = N_HEADS // N_KV_HEADS
WINDOW = 128
ROPE_THETA = 10000.0
QKV_DIM = (N_HEADS + 2 * N_KV_HEADS) * HEAD_DIM
N_EXPERTS = 16
N_GROUPS = 4
EXPERTS_PER_GROUP = N_EXPERTS // N_GROUPS
D_EXPERT = 512
DEEPNORM_ALPHA = (2.0 * DEPTH) ** 0.25
LN_EPS = 1e-5

MXU_EDGE = 256
LANES = 128
SUBLANES = 8

ROW_TILE = 512
MOE_CHUNK = SEQ
N_CHUNKS = BATCH + DEC_BATCH * DEC_SEQ // MOE_CHUNK
SAMPLE_CHUNK = BATCH
MOE_TM = 256
MOE_ROWS = 2 * MOE_CHUNK + N_EXPERTS * MOE_TM
MOE_TILES = MOE_ROWS // MOE_TM
COMBINE_STEPS = MOE_CHUNK // ROW_TILE
PAIR_ROWS = MOE_ROWS + MOE_TM
JUNK_PAIR = 2 * MOE_CHUNK
SAMPLE_SEGS = ROW_TILE // DEC_SEQ
NEG_BIG = -1e30
VMEM_LIMIT = 52 * 1024 * 1024

assert DEC_BATCH * DEC_SEQ == MOE_CHUNK and MOE_TILES <= LANES


def _layer_norm(x, g, b):
    mu = jnp.mean(x, -1, keepdims=True)
    xc = x - mu
    var = jnp.mean(xc * xc, -1, keepdims=True)
    return xc * lax.rsqrt(var + LN_EPS) * g + b


def _const_spec(shape):
    nd = len(shape)
    return pl.BlockSpec(shape, lambda *_: (0,) * nd)


_ACT_SHAPE = jax.ShapeDtypeStruct((N_CHUNKS, MOE_CHUNK, D_MODEL), jnp.float32)


LRU_SEGS = SUBLANES
LRU_SEG = ROW_TILE // LRU_SEGS
assert LRU_SEG == DEC_SEQ


def _lru_tile(x_tm, x_sm, pre, h_init, chain, perm_ref, win_ref, cw_ref, cb_ref, wr_ref, wi_ref, br_ref, bi_ref,
              lam_ref, wout_ref, g_ref, b_ref):
    rows = ROW_TILE
    xb = x_tm.astype(jnp.bfloat16)
    rec = jnp.dot(xb, win_ref[:, D_RNN:], preferred_element_type=jnp.float32)
    if chain:
        sub = lax.broadcasted_iota(jnp.int32, (SUBLANES, D_RNN), 0)
        pre = [jnp.where(sub == 0, pre[j], pltpu.roll(rec[rows - (CONV_WIDTH - 1 - j) * SUBLANES:
                                                          rows - (CONV_WIDTH - 2 - j) * SUBLANES], 1, 0))
               for j in range(CONV_WIDTH - 1)]
    ext = jnp.concatenate(pre + [rec], axis=0)
    cw = cw_ref[...]
    xc = cb_ref[...] + cw[CONV_WIDTH - 1:CONV_WIDTH] * rec
    for j in range(CONV_WIDTH - 1):
        xc = xc + cw[j:j + 1] * ext[j * SUBLANES:j * SUBLANES + rows]

    xcb = xc.astype(jnp.bfloat16)
    r_parts, i_parts = [], []
    for q in range(D_RNN // MXU_EDGE):
        blk = xcb[:, q * MXU_EDGE:(q + 1) * MXU_EDGE]
        r_parts.append(jnp.dot(blk, wr_ref[q], preferred_element_type=jnp.float32))
        i_parts.append(jnp.dot(blk, wi_ref[q], preferred_element_type=jnp.float32))
    r = jax.nn.sigmoid(jnp.concatenate(r_parts, axis=1) + br_ref[...])
    ig = jax.nn.sigmoid(jnp.concatenate(i_parts, axis=1) + bi_ref[...])

    z = -lam_ref[...]
    softplus = jnp.maximum(z, 0.0) + jnp.log1p(jnp.exp(-jnp.abs(z)))
    log_a = (-LRU_C * r) * softplus
    a = jnp.exp(log_a)
    b = jnp.sqrt(-jnp.tanh(log_a) * (a * a + 1.0)) * (ig * xc)

    steps = lambda v, t: v[t * SUBLANES:(t + 1) * SUBLANES]
    if chain:
        h = jnp.zeros((SUBLANES, D_RNN), jnp.float32)
        prod = jnp.ones((SUBLANES, D_RNN), jnp.float32)
        local, prods = [], []
        for t in range(LRU_SEG):
            h = steps(a, t) * h + steps(b, t)
            prod = steps(a, t) * prod
            local.append(h)
            prods.append(prod)
        carry, inits = h_init, []
        for s in range(LRU_SEGS):
            inits.append(carry)
            carry = prod[s:s + 1] * carry + h[s:s + 1]
        init = jnp.concatenate(inits, axis=0)
        hs = [local[t] + prods[t] * init for t in range(LRU_SEG)]
    else:
        h = h_init
        hs = []
        for t in range(LRU_SEG):
            h = steps(a, t) * h + steps(b, t)
            hs.append(h)
    h_end = hs[-1]

    gate_in = jnp.dot(xb, win_ref[:, :D_RNN], preferred_element_type=jnp.float32)
    y = (jnp.concatenate(hs, axis=0) * jax.nn.gelu(gate_in, approximate=True)).astype(jnp.bfloat16)
    y_sm = jnp.dot(perm_ref[...], y, preferred_element_type=jnp.float32).astype(jnp.bfloat16)
    out = jnp.dot(y_sm, wout_ref[...], preferred_element_type=jnp.float32)
    return _layer_norm(DEEPNORM_ALPHA * x_sm + out, g_ref[...], b_ref[...]), rec, h_end


def _lru_kernel(xp_ref, xs_ref, conv_in_ref, h0_ref, perm_ref, win_ref, cw_ref, cb_ref, wr_ref, wi_ref, br_ref,
                bi_ref, lam_ref, wout_ref, g_ref, b_ref, x1_ref, conv_p_ref, h_p_ref, conv_s_ref, h_s_ref,
                tail_scr, h_scr):
    c = pl.program_id(0)
    t = pl.program_id(1)
    weights = (perm_ref, win_ref, cw_ref, cb_ref, wr_ref, wi_ref, br_ref, bi_ref, lam_ref, wout_ref, g_ref, b_ref)
    rows = ROW_TILE

    def interleaved(x_ref):
        return jnp.concatenate([x_ref[:, g, :] for g in range(LRU_SEG)], axis=0)

    @pl.when(c < BATCH)
    def _():
        @pl.when(t == 0)
        def _():
            tail_scr[...] = jnp.zeros_like(tail_scr)
            h_scr[...] = jnp.zeros_like(h_scr)

        pre = [tail_scr[j:j + 1, :] for j in range(CONV_WIDTH - 1)]
        x1, rec, h_end = _lru_tile(interleaved(xp_ref), xp_ref[...].reshape(rows, D_MODEL), pre, h_scr[...], True,
                                   *weights)
        last = LRU_SEGS - 1
        for j in range(CONV_WIDTH - 1):
            row = (LRU_SEG - (CONV_WIDTH - 1) + j) * SUBLANES + last
            tail_scr[j:j + 1, :] = rec[row:row + 1]
            conv_p_ref[j:j + 1, :] = rec[row:row + 1]
        h_scr[...] = h_end[last:last + 1]
        h_p_ref[...] = h_end[last:last + 1]
        x1_ref[...] = x1

    @pl.when(c == SAMPLE_CHUNK)
    def _():
        pre = [conv_in_ref[:, j, :] for j in range(CONV_WIDTH - 1)]
        x1, rec, h_end = _lru_tile(interleaved(xs_ref), xs_ref[...].reshape(rows, D_MODEL), pre, h0_ref[...], False,
                                   *weights)
        for j in range(CONV_WIDTH - 1):
            lo = (LRU_SEG - (CONV_WIDTH - 1) + j) * SUBLANES
            conv_s_ref[:, j, :] = rec[lo:lo + SUBLANES]
        h_s_ref[...] = h_end
        x1_ref[...] = x1


_TILES_PER_CHUNK = MOE_CHUNK // ROW_TILE


def _prompt_tile(c, t):
    return jnp.where(c < BATCH, t, _TILES_PER_CHUNK - 1)


def _sample_tile(c, t):
    return jnp.where(c == SAMPLE_CHUNK, t, 0)


def _mixer_act_specs():
    return [
        pl.BlockSpec((None, ROW_TILE, D_MODEL), lambda c, t: (jnp.minimum(c, BATCH - 1), _prompt_tile(c, t), 0)),
        pl.BlockSpec((None, ROW_TILE, D_MODEL), lambda c, t: (0, _sample_tile(c, t), 0)),
    ]


_MIXER_GRID = (N_CHUNKS, _TILES_PER_CHUNK)
_PER_PROMPT = lambda c, t: (jnp.minimum(c, BATCH - 1), 0, 0)
_PER_SAMPLE3 = lambda c, t: (_sample_tile(c, t), 0, 0)
_PER_SAMPLE4 = lambda c, t: (_sample_tile(c, t), 0, 0, 0)


def _lru(xp, xs, conv_in, h0, w, ln_g, ln_b):
    seg_tile = (None, LRU_SEGS, LRU_SEG, D_MODEL)
    row = lax.broadcasted_iota(jnp.int32, (ROW_TILE, ROW_TILE), 0)
    col = lax.broadcasted_iota(jnp.int32, (ROW_TILE, ROW_TILE), 1)
    perm = (col == (row % LRU_SEG) * LRU_SEGS + row // LRU_SEG).astype(jnp.bfloat16)
    return pl.pallas_call(
        _lru_kernel,
        grid=_MIXER_GRID,
        in_specs=[
            pl.BlockSpec(seg_tile, lambda c, t: (jnp.minimum(c, BATCH - 1), _prompt_tile(c, t), 0, 0)),
            pl.BlockSpec(seg_tile, lambda c, t: (0, _sample_tile(c, t), 0, 0)),
            pl.BlockSpec((LRU_SEGS, CONV_WIDTH - 1, D_RNN), _PER_SAMPLE3),
            pl.BlockSpec((LRU_SEGS, D_RNN), lambda c, t: (_sample_tile(c, t), 0)),
            _const_spec((ROW_TILE, ROW_TILE)),
            _const_spec((D_MODEL, 2 * D_RNN)),
            _const_spec((CONV_WIDTH, D_RNN)),
            _const_spec((1, D_RNN)),
            _const_spec((D_RNN // MXU_EDGE, MXU_EDGE, MXU_EDGE)),
            _const_spec((D_RNN // MXU_EDGE, MXU_EDGE, MXU_EDGE)),
            _const_spec((1, D_RNN)),
            _const_spec((1, D_RNN)),
            _const_spec((1, D_RNN)),
            _const_spec((D_RNN, D_MODEL)),
            _const_spec((1, D_MODEL)),
            _const_spec((1, D_MODEL)),
        ],
        out_specs=[
            pl.BlockSpec((None, ROW_TILE, D_MODEL), lambda c, t: (c, t, 0)),
            pl.BlockSpec((None, CONV_WIDTH - 1, D_RNN), _PER_PROMPT),
            pl.BlockSpec((None, 1, D_RNN), _PER_PROMPT),
            pl.BlockSpec((LRU_SEGS, CONV_WIDTH - 1, D_RNN), _PER_SAMPLE3),
            pl.BlockSpec((LRU_SEGS, D_RNN), lambda c, t: (_sample_tile(c, t), 0)),
        ],
        out_shape=[
            _ACT_SHAPE,
            jax.ShapeDtypeStruct((BATCH, CONV_WIDTH - 1, D_RNN), jnp.float32),
            jax.ShapeDtypeStruct((BATCH, 1, D_RNN), jnp.float32),
            jax.ShapeDtypeStruct((DEC_BATCH, CONV_WIDTH - 1, D_RNN), jnp.float32),
            jax.ShapeDtypeStruct((DEC_BATCH, D_RNN), jnp.float32),
        ],
        scratch_shapes=[
            pltpu.VMEM((SUBLANES, D_RNN), jnp.float32),
            pltpu.VMEM((1, D_RNN), jnp.float32),
        ],
        compiler_params=pltpu.CompilerParams(
            dimension_semantics=("arbitrary", "arbitrary"), vmem_limit_bytes=VMEM_LIMIT),
        name="lru",
    )(xp.reshape(BATCH, SEQ // LRU_SEG, LRU_SEG, D_MODEL), xs.reshape(1, MOE_CHUNK // LRU_SEG, LRU_SEG, D_MODEL),
      conv_in, h0, perm, *w, ln_g, ln_b)


def _rope(x, cos, sin_signed):
    lane = lax.broadcasted_iota(jnp.int32, (x.shape[0], LANES), 1)
    first_half = (lane & (HEAD_DIM - 1)) < HEAD_DIM // 2
    cols = []
    for c in range(x.shape[1] // LANES):
        xc = x[:, c * LANES:(c + 1) * LANES]
        partner = jnp.where(first_half, pltpu.roll(xc, LANES - HEAD_DIM // 2, 1),
                            pltpu.roll(xc, HEAD_DIM // 2, 1))
        cols.append(xc * cos + partner * sin_signed)
    return cols[0] if len(cols) == 1 else jnp.concatenate(cols, axis=1)


def _dup_heads(kv):
    lane = lax.broadcasted_iota(jnp.int32, kv.shape, 1)
    low = lane < HEAD_DIM
    swapped = pltpu.roll(kv, HEAD_DIM, 1)
    return (jnp.where(low, kv, swapped).astype(jnp.bfloat16),
            jnp.where(low, swapped, kv).astype(jnp.bfloat16))


def _attend(chunks, sinks_ref):
    lane = lax.broadcasted_iota(jnp.int32, (CHUNK, LANES), 1)
    low = lane < HEAD_DIM
    zero = jnp.zeros((CHUNK, LANES), jnp.bfloat16)
    cols_per_group = KV_GROUP * HEAD_DIM // LANES
    units = [(ci, g) for ci in range(len(chunks)) for g in range(N_KV_HEADS)]

    scores = {}
    for ci, g in units:
        q, k_wins, _, valid = chunks[ci]
        stacked = []
        for c in range(g * cols_per_group, (g + 1) * cols_per_group):
            qc = q[:, c * LANES:(c + 1) * LANES]
            stacked.append(jnp.where(low, qc, zero))
            stacked.append(jnp.where(low, zero, qc))
        qst = jnp.concatenate(stacked, axis=0)
        s = lax.dot_general(qst, k_wins[g], (((1,), (1,)), ((), ())),
                            preferred_element_type=jnp.float32)
        scores[ci, g] = s if valid is None else jnp.where(valid, s, NEG_BIG)

    probs = {}
    for ci, g in units:
        s = scores[ci, g]
        heads = []
        for hh in range(KV_GROUP):
            sh = s[hh * CHUNK:(hh + 1) * CHUNK]
            sink = sinks_ref[g * KV_GROUP + hh]
            m = jnp.maximum(jnp.max(sh, -1, keepdims=True), sink)
            p = jnp.exp(sh - m)
            denom = jnp.sum(p, -1, keepdims=True) + jnp.exp(sink - m)
            heads.append((p * (1.0 / denom)).astype(jnp.bfloat16))
        probs[ci, g] = jnp.concatenate(heads, axis=0)

    out_cols = {ci: [] for ci in range(len(chunks))}
    for ci, g in units:
        o = jnp.dot(probs[ci, g], chunks[ci][2][g], preferred_element_type=jnp.float32)
        for j in range(cols_per_group):
            oa = o[(2 * j) * CHUNK:(2 * j + 1) * CHUNK]
            ob = o[(2 * j + 1) * CHUNK:(2 * j + 2) * CHUNK]
            out_cols[ci].append(jnp.where(low, oa, ob))
    return [jnp.concatenate(out_cols[ci], axis=1) for ci in range(len(chunks))]


def _project_qkv(x, wqkv_ref, bqkv_ref, cos, sin):
    nq = N_HEADS * HEAD_DIM
    nk = N_KV_HEADS * HEAD_DIM
    xb = x.astype(jnp.bfloat16)
    kv = jnp.dot(xb, wqkv_ref[:, nq:], preferred_element_type=jnp.float32) + bqkv_ref[:, nq:]
    k = _rope(kv[:, :nk], cos, sin)
    v = kv[:, nk:]
    q = jnp.dot(xb, wqkv_ref[:, :nq], preferred_element_type=jnp.float32) + bqkv_ref[:, :nq]
    q = _rope(q, cos, sin)
    qs = (q * (HEAD_DIM ** -0.5)).astype(jnp.bfloat16)
    return qs, k, v


SUB_TILE = 256


def _sub_tiles():
    return [pl.ds(i * SUB_TILE, SUB_TILE) for i in range(ROW_TILE // SUB_TILE)]


def _project_out(xs, subs, o, wo_ref, g_ref, b_ref, x1_ref):
    for x, sub, i in zip(xs, subs, range(len(subs))):
        out = jnp.dot(o[i * SUB_TILE:(i + 1) * SUB_TILE].astype(jnp.bfloat16), wo_ref[...],
                      preferred_element_type=jnp.float32)
        x1_ref[sub, :] = _layer_norm(DEEPNORM_ALPHA * x + out, g_ref[...], b_ref[...])


def _swa_kernel(xp_ref, xs_ref, ck_ref, cv_ref, wqkv_ref, bqkv_ref, cosp_ref, sinp_ref, coss_ref, sins_ref,
                sinks_ref, wo_ref, g_ref, b_ref, x1_ref, kp_ref, vp_ref, ks_ref, vs_ref,
                k0_scr, k1_scr, v0_scr, v1_scr):
    c = pl.program_id(0)
    t = pl.program_id(1)
    scrs = (k0_scr, k1_scr, v0_scr, v1_scr)

    @pl.when(c < BATCH)
    def _():
        @pl.when(t == 0)
        def _():
            for scr in scrs:
                scr[0:WINDOW, :] = jnp.zeros((WINDOW, LANES), jnp.bfloat16)

        subs = _sub_tiles()
        xs_ = [xp_ref[sub, :] for sub in subs]
        proj = [_project_qkv(x, wqkv_ref, bqkv_ref, cosp_ref[sub, :], sinp_ref[sub, :]) for x, sub in zip(xs_, subs)]
        qs = jnp.concatenate([p[0] for p in proj], axis=0)
        k = jnp.concatenate([p[1] for p in proj], axis=0)
        v = jnp.concatenate([p[2] for p in proj], axis=0)
        kp_ref[...] = k[ROW_TILE - WINDOW:]
        vp_ref[...] = v[ROW_TILE - WINDOW:]
        k0, k1 = _dup_heads(k)
        v0, v1 = _dup_heads(v)
        for scr, val in zip(scrs, (k0, k1, v0, v1)):
            scr[WINDOW:WINDOW + ROW_TILE, :] = val

        span = WINDOW + CHUNK
        key_pos = lax.broadcasted_iota(jnp.int32, (1, span), 1)
        chunks = []
        for ci in range(ROW_TILE // CHUNK):
            lo = ci * CHUNK
            valid = (t * ROW_TILE + lo - WINDOW + key_pos) >= 0
            k_wins = (k0_scr[lo:lo + span, :], k1_scr[lo:lo + span, :])
            v_wins = (v0_scr[lo:lo + span, :], v1_scr[lo:lo + span, :])
            chunks.append((qs[lo:lo + CHUNK], k_wins, v_wins, valid))
        o = jnp.concatenate(_attend(chunks, sinks_ref), axis=0)
        for scr in scrs:
            scr[0:WINDOW, :] = scr[ROW_TILE:ROW_TILE + WINDOW, :]
        _project_out(xs_, subs, o, wo_ref, g_ref, b_ref, x1_ref)

    @pl.when(c == SAMPLE_CHUNK)
    def _():
        subs = _sub_tiles()
        xs_ = [xs_ref[sub, :] for sub in subs]
        proj = [_project_qkv(x, wqkv_ref, bqkv_ref, coss_ref[sub, :], sins_ref[sub, :]) for x, sub in zip(xs_, subs)]
        qs = jnp.concatenate([p[0] for p in proj], axis=0)
        k = jnp.concatenate([p[1] for p in proj], axis=0)
        v = jnp.concatenate([p[2] for p in proj], axis=0)
        knew = _dup_heads(k)
        vnew = _dup_heads(v)
        chunks = []
        for s in range(SAMPLE_SEGS):
            lo = s * DEC_SEQ
            ck = ck_ref[s]
            cv = cv_ref[s]
            kold = _dup_heads(ck)
            vold = _dup_heads(cv)
            k_wins = [jnp.concatenate([kold[g], knew[g][lo:lo + DEC_SEQ]], axis=0) for g in range(N_KV_HEADS)]
            v_wins = [jnp.concatenate([vold[g], vnew[g][lo:lo + DEC_SEQ]], axis=0) for g in range(N_KV_HEADS)]
            chunks.append((qs[lo:lo + DEC_SEQ], k_wins, v_wins, None))
            ks_ref[s, 0:WINDOW - DEC_SEQ, :] = ck[DEC_SEQ:]
            ks_ref[s, WINDOW - DEC_SEQ:WINDOW, :] = k[lo:lo + DEC_SEQ]
            vs_ref[s, 0:WINDOW - DEC_SEQ, :] = cv[DEC_SEQ:]
            vs_ref[s, WINDOW - DEC_SEQ:WINDOW, :] = v[lo:lo + DEC_SEQ]
        o = jnp.concatenate(_attend(chunks, sinks_ref), axis=0)
        _project_out(xs_, subs, o, wo_ref, g_ref, b_ref, x1_ref)


def _swa(xp, xs, ck, cv, w, cos_p, sin_p, cos_s, sin_s, ln_g, ln_b):
    wqkv, bqkv, sinks, wo = w
    nt_s = DEC_BATCH // SAMPLE_SEGS
    kv_lanes = N_KV_HEADS * HEAD_DIM
    cache_spec = pl.BlockSpec((None, SAMPLE_SEGS, WINDOW, kv_lanes), _PER_SAMPLE4)
    prompt_table = pl.BlockSpec((ROW_TILE, LANES), lambda c, t: (_prompt_tile(c, t), 0))
    kv_prompt = pl.BlockSpec((None, WINDOW, kv_lanes), _PER_PROMPT)
    return pl.pallas_call(
        _swa_kernel,
        grid=_MIXER_GRID,
        in_specs=_mixer_act_specs() + [
            cache_spec, cache_spec,
            _const_spec((D_MODEL, QKV_DIM)),
            _const_spec((1, QKV_DIM)),
            prompt_table, prompt_table,
            _const_spec((ROW_TILE, LANES)), _const_spec((ROW_TILE, LANES)),
            pl.BlockSpec(memory_space=pltpu.SMEM),
            _const_spec((N_HEADS * HEAD_DIM, D_MODEL)),
            _const_spec((1, D_MODEL)),
            _const_spec((1, D_MODEL)),
        ],
        out_specs=[
            pl.BlockSpec((None, ROW_TILE, D_MODEL), lambda c, t: (c, t, 0)),
            kv_prompt, kv_prompt, cache_spec, cache_spec,
        ],
        out_shape=[
            _ACT_SHAPE,
            jax.ShapeDtypeStruct((BATCH, WINDOW, kv_lanes), jnp.float32),
            jax.ShapeDtypeStruct((BATCH, WINDOW, kv_lanes), jnp.float32),
            jax.ShapeDtypeStruct((nt_s, SAMPLE_SEGS, WINDOW, kv_lanes), jnp.float32),
            jax.ShapeDtypeStruct((nt_s, SAMPLE_SEGS, WINDOW, kv_lanes), jnp.float32),
        ],
        scratch_shapes=[pltpu.VMEM((WINDOW + ROW_TILE, LANES), jnp.bfloat16) for _ in range(4)],
        compiler_params=pltpu.CompilerParams(
            dimension_semantics=("arbitrary", "arbitrary"), vmem_limit_bytes=VMEM_LIMIT),
        name="swa",
    )(xp, xs, ck, cv, wqkv, bqkv, cos_p, sin_p, cos_s, sin_s, sinks, wo, ln_g, ln_b)


META_TILE_EXPERT, META_N_ACTIVE, META_IS_NEW, META_TILE_RUN, META_RUN_EXPERT, META_N_RUNS = range(6)
WEIGHT_SLOTS = 3


def _router_kernel(x_ref, wr_ref, br_ref, pos_ref, gate_ref, meta_ref):
    f32 = jnp.float32
    x = x_ref[...]
    w = wr_ref[...]
    xh = x.astype(jnp.bfloat16)
    xl = (x - xh.astype(f32)).astype(jnp.bfloat16)
    wh = w.astype(jnp.bfloat16)
    wl = (w - wh.astype(f32)).astype(jnp.bfloat16)
    nt = (((1,), (1,)), ((), ()))
    logits = (lax.dot_general(wh, xh, nt, preferred_element_type=f32)
              + lax.dot_general(wl, xh, nt, preferred_element_type=f32)
              + lax.dot_general(wh, xl, nt, preferred_element_type=f32))
    m = jnp.max(logits, axis=0, keepdims=True)
    e = jnp.exp(logits - m)
    probs = e / jnp.sum(e, axis=0, keepdims=True)
    sel = probs + br_ref[...]
    p = [probs[i:i + 1] for i in range(N_EXPERTS)]
    s = [sel[i:i + 1] for i in range(N_EXPERTS)]

    def first_argmax(vals):
        best, arg = vals[0], jnp.zeros_like(vals[0], dtype=jnp.int32)
        for i in range(1, len(vals)):
            better = vals[i] > best
            best = jnp.where(better, vals[i], best)
            arg = jnp.where(better, i, arg)
        return arg

    group_scores = []
    for gidx in range(N_GROUPS):
        v = s[gidx * EXPERTS_PER_GROUP:(gidx + 1) * EXPERTS_PER_GROUP]
        best = v[0] + v[1]
        for i in range(EXPERTS_PER_GROUP):
            for j in range(i + 1, EXPERTS_PER_GROUP):
                if (i, j) != (0, 1):
                    best = jnp.maximum(best, v[i] + v[j])
        group_scores.append(best)
    best_group = first_argmax(group_scores)
    neg_inf = jnp.full_like(s[0], -jnp.inf)
    masked = [jnp.where(best_group == (i // EXPERTS_PER_GROUP), s[i], neg_inf) for i in range(N_EXPERTS)]
    idx1 = first_argmax(masked)
    masked2 = [jnp.where(idx1 == i, neg_inf, masked[i]) for i in range(N_EXPERTS)]
    idx2 = first_argmax(masked2)
    zero = jnp.zeros_like(p[0])
    w1, w2 = zero, zero
    for i in range(N_EXPERTS):
        w1 = w1 + jnp.where(idx1 == i, p[i], zero)
        w2 = w2 + jnp.where(idx2 == i, p[i], zero)
    tot = w1 + w2
    gate_ref[0:1, :] = w1 / tot
    gate_ref[1:2, :] = w2 / tot

    expert = lax.broadcasted_iota(jnp.int32, (N_EXPERTS, MOE_CHUNK), 0)
    strict_upper = (lax.broadcasted_iota(jnp.int32, (LANES, LANES), 0)
                    < lax.broadcasted_iota(jnp.int32, (LANES, LANES), 1)).astype(jnp.bfloat16)
    carry = jnp.zeros((N_EXPERTS, 1), f32)
    onehots, ranks = [], []
    for idx in (idx1, idx2):
        onehot = (expert == idx).astype(f32)
        before = []
        for blk in range(MOE_CHUNK // LANES):
            oh = onehot[:, blk * LANES:(blk + 1) * LANES]
            before.append(jnp.dot(oh.astype(jnp.bfloat16), strict_upper, preferred_element_type=f32) + carry)
            carry = carry + jnp.sum(oh, axis=1, keepdims=True)
        onehots.append(onehot)
        ranks.append(jnp.sum(onehot * jnp.concatenate(before, axis=1), axis=0, keepdims=True))
    counts = carry
    padded = jnp.floor((counts + (MOE_TM - 1)) * (1.0 / MOE_TM)) * MOE_TM
    strict_lower = (lax.broadcasted_iota(jnp.int32, (N_EXPERTS, N_EXPERTS), 0)
                    > lax.broadcasted_iota(jnp.int32, (N_EXPERTS, N_EXPERTS), 1)).astype(jnp.bfloat16)
    starts = jnp.dot(strict_lower, jnp.broadcast_to(padded, (N_EXPERTS, LANES)).astype(jnp.bfloat16),
                     preferred_element_type=f32)[:, 0:1]
    ends = starts + padded
    for k in range(2):
        pos = jnp.sum(onehots[k] * starts, axis=0, keepdims=True) + ranks[k]
        pos_ref[k:k + 1, :] = pos.astype(jnp.int32)

    lane = lax.broadcasted_iota(jnp.int32, (N_EXPERTS, LANES), 1)
    sub = lax.broadcasted_iota(jnp.int32, (N_EXPERTS, LANES), 0)
    tile_start = (lane * MOE_TM).astype(f32)
    tile_expert = jnp.sum((ends <= tile_start).astype(f32), axis=0, keepdims=True)
    tile_expert = jnp.minimum(tile_expert, N_EXPERTS - 1.0)
    on_diag = sub == lane

    def to_lanes(col):
        return jnp.sum(jnp.where(on_diag, col, 0.0), axis=0, keepdims=True)

    n_active = jnp.broadcast_to(ends[N_EXPERTS - 1:N_EXPERTS] * (1.0 / MOE_TM), (1, LANES))
    lane1 = lax.broadcasted_iota(jnp.int32, (1, LANES), 1)
    active = lane1.astype(f32) < n_active
    starts_run = jnp.logical_or(lane1 == 0, tile_expert != pltpu.roll(tile_expert, 1, 1))
    is_new = jnp.where(jnp.logical_and(active, starts_run), 1.0, 0.0)
    upper_incl = (lax.broadcasted_iota(jnp.int32, (LANES, LANES), 0)
                  <= lax.broadcasted_iota(jnp.int32, (LANES, LANES), 1)).astype(jnp.bfloat16)
    tile_run = jnp.dot(jnp.broadcast_to(is_new, (SUBLANES, LANES)).astype(jnp.bfloat16), upper_incl,
                       preferred_element_type=f32)[0:1] - 1.0
    first_of_run = jnp.logical_and(is_new > 0.0, tile_run == sub.astype(f32))
    run_expert = jnp.sum(jnp.where(first_of_run, tile_expert, 0.0), axis=1, keepdims=True)
    n_runs = jnp.broadcast_to(jnp.sum(is_new, axis=1, keepdims=True), (1, LANES))

    meta_ref[...] = jnp.zeros(meta_ref.shape, jnp.int32)
    meta_ref[META_IS_NEW:META_IS_NEW + 1, :] = is_new.astype(jnp.int32)
    meta_ref[META_TILE_RUN:META_TILE_RUN + 1, :] = tile_run.astype(jnp.int32)
    meta_ref[META_RUN_EXPERT:META_RUN_EXPERT + 1, :] = to_lanes(run_expert).astype(jnp.int32)
    meta_ref[META_N_RUNS:META_N_RUNS + 1, :] = n_runs.astype(jnp.int32)
    meta_ref[META_TILE_EXPERT:META_TILE_EXPERT + 1, :] = tile_expert.astype(jnp.int32)
    meta_ref[META_N_ACTIVE:META_N_ACTIVE + 1, :] = n_active.astype(jnp.int32)


def _router(x, wr_t, br):
    per_chunk = lambda c: (c, 0, 0)
    return pl.pallas_call(
        _router_kernel,
        grid=(N_CHUNKS,),
        in_specs=[
            pl.BlockSpec((None, MOE_CHUNK, D_MODEL), per_chunk),
            _const_spec((N_EXPERTS, D_MODEL)),
            _const_spec((N_EXPERTS, 1)),
        ],
        out_specs=[
            pl.BlockSpec((None, 2, MOE_CHUNK), per_chunk),
            pl.BlockSpec((None, 2, MOE_CHUNK), per_chunk),
            pl.BlockSpec((None, SUBLANES, LANES), per_chunk),
        ],
        out_shape=[
            jax.ShapeDtypeStruct((N_CHUNKS, 2, MOE_CHUNK), jnp.int32),
            jax.ShapeDtypeStruct((N_CHUNKS, 2, MOE_CHUNK), jnp.float32),
            jax.ShapeDtypeStruct((N_CHUNKS, SUBLANES, LANES), jnp.int32),
        ],
        compiler_params=pltpu.CompilerParams(
            dimension_semantics=("parallel",), vmem_limit_bytes=VMEM_LIMIT),
        name="router",
    )(x, wr_t, br)


def _gather_rows(pair_scr, x_ref, xs_scr, row0, j):
    tok = pair_scr[row0 + j] & (MOE_CHUNK - 1)
    xs_scr[j // SUBLANES, (j % SUBLANES):(j % SUBLANES) + 1, :] = x_ref[pl.ds(tok, 1), :]


def _scatter_rows(pair_scr, y_scr, yo_scr, row0, j):
    y_scr[pl.ds(pair_scr[row0 + j], 1), :] = yo_scr[j // SUBLANES, (j % SUBLANES):(j % SUBLANES) + 1, :]


def _moe_kernel(layer, meta_ref, pos_ref, x_ref, gate_ref,
                wg_hbm, wu_hbm, wd_hbm, g_ref, b_ref, outp_hbm, outs_hbm,
                pair_scr, xs_a, xs_b, yo_a, yo_b, y_scr, wg_buf, wu_buf, wd_buf, junk_buf, w_sem, out_sem, fill_sem):
    c = pl.program_id(0)

    def plan(row, i):
        return meta_ref[(c * SUBLANES + row) * LANES + i]

    n_act = plan(META_N_ACTIVE, 0)

    n_runs = plan(META_N_RUNS, 0)

    def weight_copies(run):
        e, s = plan(META_RUN_EXPERT, run), lax.rem(run, WEIGHT_SLOTS)
        return (pltpu.make_async_copy(wg_hbm.at[layer, e], wg_buf.at[s], w_sem.at[s, 0]),
                pltpu.make_async_copy(wu_hbm.at[layer, e], wu_buf.at[s], w_sem.at[s, 1]),
                pltpu.make_async_copy(wd_hbm.at[layer, e], wd_buf.at[s], w_sem.at[s, 2]))

    def prompt_out_copy(chunk):
        return pltpu.make_async_copy(y_scr.at[pl.ds(0, MOE_CHUNK)], outp_hbm.at[chunk], out_sem.at[0])

    def sample_out_copy():
        return pltpu.make_async_copy(y_scr.at[pl.ds(0, MOE_CHUNK)], outs_hbm.at[0], out_sem.at[0])

    for ahead in range(WEIGHT_SLOTS - 1):
        @pl.when(ahead < n_runs)
        def _():
            for cp in weight_copies(ahead):
                cp.start()

    fill = pltpu.make_async_copy(junk_buf, pair_scr, fill_sem.at[0])

    @pl.when(c == 0)
    def _():
        junk_buf[...] = jnp.full(junk_buf.shape, JUNK_PAIR, jnp.int32)
        fill.start()
    fill.wait()

    def invert(t, carry):
        pair_scr[pos_ref[t]] = t
        pair_scr[pos_ref[MOE_CHUNK + t]] = MOE_CHUNK + t
        return carry
    lax.fori_loop(0, MOE_CHUNK, invert, 0, unroll=8)

    def gather0(gi, carry):
        for u in range(SUBLANES):
            tok = pair_scr[gi * SUBLANES + u] & (MOE_CHUNK - 1)
            xs_a[gi, u:u + 1, :] = x_ref[pl.ds(tok, 1), :]
        return carry
    lax.fori_loop(0, MOE_TM // SUBLANES, gather0, 0)

    @pl.when(c == 0)
    def _():
        yo_b[...] = jnp.zeros_like(yo_b)

    @pl.when(c > 0)
    def _():
        prompt_out_copy(c - 1).wait()

    def expert_step(t, s, xs_cur, yo_cur, xs_nxt, yo_prv):
        row_nxt = (t + 1) * MOE_TM
        row_prv = jnp.maximum(t - 1, 0) * MOE_TM
        for j in range(MOE_TM):
            _scatter_rows(pair_scr, y_scr, yo_prv, row_prv, j)
            _gather_rows(pair_scr, x_ref, xs_nxt, row_nxt, j)
        xb = xs_cur[...].reshape(MOE_TM, D_MODEL).astype(jnp.bfloat16)
        hg = jnp.dot(xb, wg_buf[s], preferred_element_type=jnp.float32)
        hu = jnp.dot(xb, wu_buf[s], preferred_element_type=jnp.float32)
        act = (jax.nn.silu(hg) * hu).astype(jnp.bfloat16)
        yo = jnp.dot(act, wd_buf[s], preferred_element_type=jnp.float32)
        yo_cur[...] = yo.reshape(MOE_TM // SUBLANES, SUBLANES, D_MODEL)

    def tile_iteration(t, carry):
        run = plan(META_TILE_RUN, t)
        starts_run = plan(META_IS_NEW, t) == 1

        @pl.when(starts_run)
        def _():
            for cp in weight_copies(run):
                cp.wait()

        @pl.when(jnp.logical_and(starts_run, run + WEIGHT_SLOTS - 1 < n_runs))
        def _():
            for cp in weight_copies(run + WEIGHT_SLOTS - 1):
                cp.start()

        slot = lax.rem(run, WEIGHT_SLOTS)
        for s in range(WEIGHT_SLOTS):
            in_slot = slot == s

            @pl.when(jnp.logical_and((t & 1) == 0, in_slot))
            def _():
                expert_step(t, s, xs_a, yo_a, xs_b, yo_b)

            @pl.when(jnp.logical_and((t & 1) == 1, in_slot))
            def _():
                expert_step(t, s, xs_b, yo_b, xs_a, yo_a)
        return carry
    lax.fori_loop(0, n_act, tile_iteration, 0)

    def drain(yo_last):
        row_last = (n_act - 1) * MOE_TM

        def scatter(gi, carry):
            for u in range(SUBLANES):
                y_scr[pl.ds(pair_scr[row_last + gi * SUBLANES + u], 1), :] = yo_last[gi, u:u + 1, :]
            return carry
        lax.fori_loop(0, MOE_TM // SUBLANES, scatter, 0)

    @pl.when((n_act & 1) == 1)
    def _():
        drain(yo_a)

    @pl.when((n_act & 1) == 0)
    def _():
        drain(yo_b)

    @pl.when(c + 1 < N_CHUNKS)
    def _():
        fill.start()

    def combine(k, carry):
        lo = pl.multiple_of(k * ROW_TILE, ROW_TILE)
        gates = gate_ref[pl.ds(lo, ROW_TILE), :]
        moe = (gates[:, 0:1] * y_scr[pl.ds(lo, ROW_TILE), :]
               + gates[:, 1:2] * y_scr[pl.ds(MOE_CHUNK + lo, ROW_TILE), :])
        x = x_ref[pl.ds(lo, ROW_TILE), :]
        y_scr[pl.ds(lo, ROW_TILE), :] = _layer_norm(DEEPNORM_ALPHA * x + moe, g_ref[...], b_ref[...])
        return carry
    lax.fori_loop(0, COMBINE_STEPS, combine, 0)

    @pl.when(c < SAMPLE_CHUNK)
    def _():
        prompt_out_copy(c).start()

    @pl.when(c == SAMPLE_CHUNK)
    def _():
        sample_out_copy().start()
        sample_out_copy().wait()


def _moe(x, meta, pos, gates_t, layer, wg, wu, wd, ln_g, ln_b):
    stage = pltpu.VMEM((MOE_TM // SUBLANES, SUBLANES, D_MODEL), jnp.float32)
    per_chunk = lambda c, *_: (c, 0, 0)
    grid_spec = pltpu.PrefetchScalarGridSpec(
        num_scalar_prefetch=1,
        grid=(N_CHUNKS,),
        in_specs=[
            pl.BlockSpec((None, None, 2 * MOE_CHUNK), per_chunk, memory_space=pltpu.SMEM),
            pl.BlockSpec((None, MOE_CHUNK, D_MODEL), per_chunk),
            pl.BlockSpec((None, MOE_CHUNK, 2), per_chunk),
            pl.BlockSpec(memory_space=pl.ANY),
            pl.BlockSpec(memory_space=pl.ANY),
            pl.BlockSpec(memory_space=pl.ANY),
            pl.BlockSpec((1, D_MODEL), lambda c, *_: (0, 0)),
            pl.BlockSpec((1, D_MODEL), lambda c, *_: (0, 0)),
        ],
        out_specs=[pl.BlockSpec(memory_space=pl.ANY), pl.BlockSpec(memory_space=pl.ANY)],
        scratch_shapes=[
            pltpu.SMEM((PAIR_ROWS,), jnp.int32),
            stage, stage, stage, stage,
            pltpu.VMEM((2 * MOE_CHUNK + SUBLANES, D_MODEL), jnp.float32),
            pltpu.VMEM((WEIGHT_SLOTS, D_MODEL, D_EXPERT), jnp.bfloat16),
            pltpu.VMEM((WEIGHT_SLOTS, D_MODEL, D_EXPERT), jnp.bfloat16),
            pltpu.VMEM((WEIGHT_SLOTS, D_EXPERT, D_MODEL), jnp.bfloat16),
            pltpu.VMEM((PAIR_ROWS,), jnp.int32),
            pltpu.SemaphoreType.DMA((WEIGHT_SLOTS, 3)),
            pltpu.SemaphoreType.DMA((1,)),
            pltpu.SemaphoreType.DMA((1,)),
        ],
    )
    return pl.pallas_call(
        functools.partial(_moe_kernel, layer),
        grid_spec=grid_spec,
        out_shape=[jax.ShapeDtypeStruct((BATCH, MOE_CHUNK, D_MODEL), jnp.float32),
                   jax.ShapeDtypeStruct((1, MOE_CHUNK, D_MODEL), jnp.float32)],
        compiler_params=pltpu.CompilerParams(
            dimension_semantics=("arbitrary",), vmem_limit_bytes=VMEM_LIMIT),
        name="moe",
    )(meta.reshape(-1), pos.reshape(N_CHUNKS, 1, 2 * MOE_CHUNK), x, gates_t, wg, wu, wd, ln_g, ln_b)


def _moe_layer(x, wr_t, br, layer, wg, wu, wd, ln_g, ln_b):
    pos, gates, meta = _router(x, wr_t, br)
    return _moe(x, meta, pos, jnp.swapaxes(gates, 1, 2), layer, wg, wu, wd, ln_g, ln_b)


def _rope_tables(pos):
    half = HEAD_DIM // 2
    inv_freq = ROPE_THETA ** (-jnp.arange(half, dtype=jnp.float32) / half)
    ang = pos.astype(jnp.float32)[:, None] * inv_freq[None, :]
    cos = jnp.cos(ang)
    sin = jnp.sin(ang)
    reps = LANES // HEAD_DIM
    cos_t = jnp.tile(jnp.concatenate([cos, cos], axis=1), (1, reps))
    sin_t = jnp.tile(jnp.concatenate([-sin, sin], axis=1), (1, reps))
    return cos_t, sin_t


def _block_diag_tiles(w):
    per = MXU_EDGE // LRU_BLOCK_W
    w4 = w.reshape(LRU_BLOCKS // per, per, LRU_BLOCK_W, LRU_BLOCK_W)
    eye = jnp.eye(per, dtype=w.dtype)
    return jnp.einsum("qaij,ab->qaibj", w4, eye).reshape(LRU_BLOCKS // per, MXU_EDGE, MXU_EDGE)


def kernel(x_prompt, x_sample, state_conv_0, state_h_0, cache_k_1, cache_v_1, state_conv_2, state_h_2,
           cache_k_3, cache_v_3, w_router, b_router, ln_g, ln_b, lru_w_in, lru_conv_w, lru_conv_b,
           lru_w_gates, lru_b_gates, lru_lambda, lru_w_out, attn_w_qkv, attn_b_qkv, attn_sinks, attn_w_o,
           moe_w_gate, moe_w_up, moe_w_down):
    bf = jnp.bfloat16
    in_state = {0: (state_conv_0, state_h_0), 1: (cache_k_1, cache_v_1),
                2: (state_conv_2, state_h_2), 3: (cache_k_3, cache_v_3)}
    nt_s = DEC_BATCH // SAMPLE_SEGS
    kv_lanes = N_KV_HEADS * HEAD_DIM

    wr_t = w_router.T
    br = b_router.reshape(N_EXPERTS, 1)
    cos_p, sin_p = _rope_tables(jnp.arange(SEQ))
    cos_s, sin_s = _rope_tables(PAST_LEN + jnp.arange(DEC_SEQ))
    cos_s = jnp.tile(cos_s, (SAMPLE_SEGS, 1))
    sin_s = jnp.tile(sin_s, (SAMPLE_SEGS, 1))
    wg_all = moe_w_gate.astype(bf)
    wu_all = moe_w_up.astype(bf)
    wd_all = moe_w_down.astype(bf)

    x = x_prompt
    x_s = x_sample.reshape(1, MOE_CHUNK, D_MODEL)
    new_p, new_s = {}, {}
    for i in range(DEPTH):
        slot = i // 2
        g0 = ln_g[i, 0].reshape(1, D_MODEL)
        b0 = ln_b[i, 0].reshape(1, D_MODEL)
        g1 = ln_g[i, 1].reshape(1, D_MODEL)
        b1 = ln_b[i, 1].reshape(1, D_MODEL)
        st_a, st_b = in_state[i]
        if i % 2 == 0:
            gates = lru_w_gates[slot]
            w = (lru_w_in[slot].astype(bf), lru_conv_w[slot], lru_conv_b[slot].reshape(1, D_RNN),
                 _block_diag_tiles(gates[:, :, :LRU_BLOCK_W]).astype(bf),
                 _block_diag_tiles(gates[:, :, LRU_BLOCK_W:]).astype(bf),
                 lru_b_gates[slot][:, :LRU_BLOCK_W].reshape(1, D_RNN),
                 lru_b_gates[slot][:, LRU_BLOCK_W:].reshape(1, D_RNN),
                 lru_lambda[slot].reshape(1, D_RNN), lru_w_out[slot].astype(bf))
            x1, conv_p, h_p, conv_s, h_s = _lru(x, x_s, st_a, st_b, w, g0, b0)
            new_p[i] = (conv_p, h_p.reshape(BATCH, D_RNN))
            new_s[i] = (conv_s, h_s)
        else:
            w = (attn_w_qkv[slot].astype(bf), attn_b_qkv[slot].reshape(1, QKV_DIM), attn_sinks[slot],
                 attn_w_o[slot].astype(bf))
            x1, k_p, v_p, k_s, v_s = _swa(
                x, x_s, st_a.reshape(nt_s, SAMPLE_SEGS, WINDOW, kv_lanes),
                st_b.reshape(nt_s, SAMPLE_SEGS, WINDOW, kv_lanes), w, cos_p, sin_p, cos_s, sin_s, g0, b0)
            new_p[i] = (k_p.reshape(BATCH, WINDOW, N_KV_HEADS, HEAD_DIM),
                        v_p.reshape(BATCH, WINDOW, N_KV_HEADS, HEAD_DIM))
            new_s[i] = (k_s.reshape(DEC_BATCH, WINDOW, N_KV_HEADS, HEAD_DIM),
                        v_s.reshape(DEC_BATCH, WINDOW, N_KV_HEADS, HEAD_DIM))
        x, x_s = _moe_layer(x1, wr_t, br, i, wg_all, wu_all, wd_all, g1, b1)
    return (x, x_s.reshape(DEC_BATCH, DEC_SEQ, D_MODEL),
            new_p[0][0], new_s[0][0], new_p[0][1], new_s[0][1],
            new_p[1][0], new_s[1][0], new_p[1][1], new_s[1][1],
            new_p[2][0], new_s[2][0], new_p[2][1], new_s[2][1],
            new_p[3][0], new_s[3][0], new_p[3][1], new_s[3][1])
```
